```python
import jax, jax.numpy as jnp
from jax import lax
import numpy as np

D_MODEL = 1024
BATCH = 4
SEQ = 4096
DEPTH = 2

CHUNK = 64
N_MIXERS = 2
N_RWKV = (DEPTH + 1) // 2
N_MLSTM = DEPTH // 2
EPS = 1e-6
NEG = -1e30

RW_HEAD = 64
RW_HEADS = D_MODEL // RW_HEAD
LORA_W = 64
LORA_A = 64
LORA_G = 160
RW_SEGS = (D_MODEL, LORA_W, D_MODEL, D_MODEL, LORA_A, LORA_G)
RW_IN = sum(RW_SEGS)
GN_EPS = 64e-5

ML_HEADS = 8
ML_QK = D_MODEL // 2 // ML_HEADS
ML_V = D_MODEL // ML_HEADS
ML_QKW = 2 * ML_HEADS * ML_QK
ML_VW = ML_HEADS * ML_V
ML_IN = ML_QKW + 2 * ML_VW + 2 * ML_HEADS
CONV_W = 4
GATE_CAP = 15.0

D_FF = -(-(8 * D_MODEL) // (3 * 256)) * 256

kernel_name = "rwkv7_mlstm_interleaved_adaln_trunk"


def _rms(x):
    xf = x.astype(jnp.float32)
    return (xf * lax.rsqrt(jnp.mean(xf * xf, -1, keepdims=True) + EPS)).astype(x.dtype)


def _softcap(t):
    return GATE_CAP * jnp.tanh(t / GATE_CAP)


def _token_shift(x):
    return jnp.pad(x[:, :-1], ((0, 0), (1, 0), (0, 0)))


def _rwkv7_step(state, inp):
    r_t, w_t, k_t, v_t, a_t, b_t = inp
    sa = jnp.einsum('bhvk,bhk->bhv', state, a_t)
    state = state * w_t[:, :, None, :] + sa[..., None] * b_t[:, :, None, :] + v_t[..., None] * k_t[:, :, None, :]
    y = jnp.einsum('bhvk,bhk->bhv', state, r_t)
    return state, y


def _rwkv7_mix(x, mix, w_in, w0, w2, a0, a2, g2, k_k, k_a, r_k, gn_w, gn_b, w_out):
    B, S, D = x.shape
    xx = _token_shift(x) - x
    xs = x[None] + xx[None] * mix[:, None, None, :]
    outs = []
    off = 0
    for i, n in enumerate(RW_SEGS):
        outs.append(xs[i] @ w_in[:, off:off + n])
        off += n
    r, wl, k, v, al, gl = outs
    w_log = -jax.nn.softplus(-(w0 + jnp.tanh(wl) @ w2)) - 0.5
    decay = jnp.exp(-jnp.exp(w_log.astype(jnp.float32)))
    a = jax.nn.sigmoid(a0 + al @ a2)
    g = jax.nn.sigmoid(gl) @ g2
    hs = lambda t: t.reshape(B, S, RW_HEADS, RW_HEAD).astype(jnp.float32)
    kk = hs(k * k_k)
    kk = kk / jnp.maximum(jnp.linalg.norm(kk, axis=-1, keepdims=True), 1e-12)
    k = k * (1.0 + (a - 1.0) * k_a)
    rh, kh, vh, ah, wh = hs(r), hs(k), hs(v), hs(a), hs(decay)
    tm = lambda t: jnp.moveaxis(t, 1, 0)
    state0 = jnp.zeros((B, RW_HEADS, RW_HEAD, RW_HEAD), jnp.float32)
    _, y = lax.scan(_rwkv7_step, state0, (tm(rh), tm(wh), tm(kh), tm(vh), tm(-kk), tm(kk * ah)))
    y = jnp.moveaxis(y, 0, 1)
    mu = jnp.mean(y, -1, keepdims=True)
    var = jnp.mean(jnp.square(y - mu), -1, keepdims=True)
    y = ((y - mu) * lax.rsqrt(var + GN_EPS)).reshape(B, S, D) * gn_w + gn_b
    bonus = (jnp.sum(rh * kh * r_k, -1, keepdims=True) * vh).reshape(B, S, D)
    return ((y + bonus).astype(x.dtype) * g) @ w_out


def _causal_dwconv(x, w, b):
    y = lax.conv_general_dilated(x, w[:, None, :], window_strides=(1,), padding=[(CONV_W - 1, 0)],
                                 dimension_numbers=('NWC', 'WIO', 'NWC'), feature_group_count=x.shape[-1])
    return y + b


def _mlstm_chunk_step(carry, inp):
    C, n, m = carry
    g_c, m_l, C_l, n_l = inp
    m_new = jnp.maximum(g_c + m, m_l)
    sa = jnp.exp(g_c + m - m_new)
    sb = jnp.exp(m_l - m_new)
    C_new = sa[..., None, None] * C + sb[..., None, None] * C_l
    n_new = sa[..., None] * n + sb[..., None] * n_l
    return (C_new, n_new, m_new), (C, n, m)


def _mlstm_mix(x, w_in, conv_w, conv_b, b_i, b_f, hn_w, w_out):
    B, S, D = x.shape
    NC = S // CHUNK
    p = x @ w_in
    qk = jax.nn.silu(_causal_dwconv(p[..., :ML_QKW], conv_w, conv_b))
    v = p[..., ML_QKW:ML_QKW + ML_VW]
    o = p[..., ML_QKW + ML_VW:ML_QKW + 2 * ML_VW]
    gi = p[..., ML_QKW + 2 * ML_VW:ML_QKW + 2 * ML_VW + ML_HEADS]
    gf = p[..., ML_QKW + 2 * ML_VW + ML_HEADS:]
    chunks = lambda t, dh: t.reshape(B, NC, CHUNK, ML_HEADS, dh).transpose(0, 3, 1, 2, 4).astype(jnp.float32)
    q = chunks(qk[..., :ML_QKW // 2], ML_QK) * (ML_QK ** -0.5)
    k = chunks(qk[..., ML_QKW // 2:], ML_QK)
    v = chunks(v, ML_V)
    gate_chunks = lambda t: t.reshape(B, NC, CHUNK, ML_HEADS).transpose(0, 3, 1, 2)
    log_i = gate_chunks(_softcap(gi.astype(jnp.float32) + b_i))
    log_f = gate_chunks(jax.nn.log_sigmoid(_softcap(gf.astype(jnp.float32) + b_f)))
    bcum = jnp.cumsum(log_f, -1)
    causal = jnp.tril(jnp.ones((CHUNK, CHUNK), dtype=bool))
    d_log = jnp.where(causal, bcum[..., :, None] - bcum[..., None, :] + log_i[..., None, :], NEG)
    m_intra = jnp.max(d_log, -1)
    g_tot = bcum[..., -1]
    loc = g_tot[..., None] - bcum + log_i
    m_loc = jnp.max(loc, -1)
    wl = jnp.exp(loc - m_loc[..., None])
    C_loc = jnp.einsum('bhcs,bhcsd,bhcse->bhcde', wl, k, v)
    n_loc = jnp.einsum('bhcs,bhcsd->bhcd', wl, k)
    init = (jnp.zeros((B, ML_HEADS, ML_QK, ML_V), jnp.float32),
            jnp.zeros((B, ML_HEADS, ML_QK), jnp.float32),
            jnp.zeros((B, ML_HEADS), jnp.float32))
    cm = lambda t: jnp.moveaxis(t, 2, 0)
    _, (C_prev, n_prev, m_prev) = lax.scan(_mlstm_chunk_step, init, (cm(g_tot), cm(m_loc), cm(C_loc), cm(n_loc)))
    C_prev, n_prev, m_prev = jnp.moveaxis(C_prev, 0, 2), jnp.moveaxis(n_prev, 0, 2), jnp.moveaxis(m_prev, 0, 2)
    inter = bcum + m_prev[..., None]
    m_t = jnp.maximum(inter, m_intra)
    w_inter = jnp.exp(inter - m_t)
    w_intra = jnp.exp(d_log - m_t[..., None]) * jnp.einsum('bhctd,bhcsd->bhcts', q, k)
    num = w_inter[..., None] * jnp.einsum('bhctd,bhcde->bhcte', q, C_prev) + jnp.einsum('bhcts,bhcse->bhcte', w_intra, v)
    den = w_inter * jnp.einsum('bhctd,bhcd->bhct', q, n_prev) + jnp.sum(w_intra, -1)
    den = jnp.maximum(jnp.abs(den), jnp.exp(-m_t))
    h = num / den[..., None]
    h = h * lax.rsqrt(jnp.mean(h * h, -1, keepdims=True) + EPS) * hn_w[:, None, None, :]
    h = h.transpose(0, 2, 3, 1, 4).reshape(B, S, ML_VW).astype(x.dtype)
    return (h * jax.nn.sigmoid(o)) @ w_out


def _swiglu(x, w_gu, w_down):
    gu = x @ w_gu
    return (jax.nn.silu(gu[..., :D_FF]) * gu[..., D_FF:]) @ w_down


def setup_inputs(seed: int = 0) -> dict:
    key = jax.random.key(seed)
    ks = jax.random.split(key, 32)
    nrm = lambda i, shape, s: jax.random.normal(ks[i], shape, jnp.float32) * s
    D = D_MODEL
    lin = jnp.arange(D, dtype=jnp.float32) / (D - 1)
    return {
        "x": nrm(0, (BATCH, SEQ, D), 1.0),
        "c": nrm(1, (BATCH, D), 1.0),
        "ada_w": nrm(2, (DEPTH, D, 6 * D), 0.3 * D ** -0.5),
        "ada_b": nrm(3, (DEPTH, 6 * D), 0.02),
        "rw_mix": jax.random.uniform(ks[4], (N_RWKV, 6, D), jnp.float32),
        "rw_w_in": nrm(5, (N_RWKV, D, RW_IN), D ** -0.5),
        "rw_w0": jnp.broadcast_to(-6.0 + 5.0 * lin ** 1.5, (N_RWKV, D)) + nrm(6, (N_RWKV, D), 0.1),
        "rw_w2": nrm(7, (N_RWKV, LORA_W, D), 0.3 * LORA_W ** -0.5),
        "rw_a0": nrm(8, (N_RWKV, D), 0.1),
        "rw_a2": nrm(9, (N_RWKV, LORA_A, D), LORA_A ** -0.5),
        "rw_g2": nrm(10, (N_RWKV, LORA_G, D), LORA_G ** -0.5),
        "rw_k_k": 0.85 + nrm(11, (N_RWKV, D), 0.05),
        "rw_k_a": 1.0 + nrm(12, (N_RWKV, D), 0.05),
        "rw_r_k": nrm(13, (N_RWKV, RW_HEADS, RW_HEAD), 0.1),
        "rw_gn_w": 1.0 + nrm(14, (N_RWKV, D), 0.05),
        "rw_gn_b": nrm(15, (N_RWKV, D), 0.02),
        "rw_w_out": nrm(16, (N_RWKV, D, D), D ** -0.5),
        "ml_w_in": nrm(17, (N_MLSTM, D, ML_IN), D ** -0.5),
        "ml_conv_w": nrm(18, (N_MLSTM, CONV_W, ML_QKW), CONV_W ** -0.5),
        "ml_conv_b": nrm(19, (N_MLSTM, ML_QKW), 0.02),
        "ml_b_i": nrm(20, (N_MLSTM, ML_HEADS), 0.1),
        "ml_b_f": jnp.broadcast_to(jnp.linspace(3.0, 6.0, ML_HEADS, dtype=jnp.float32), (N_MLSTM, ML_HEADS)) + nrm(21, (N_MLSTM, ML_HEADS), 0.1),
        "ml_hn_w": 1.0 + nrm(22, (N_MLSTM, ML_HEADS, ML_V), 0.05),
        "ml_w_out": nrm(23, (N_MLSTM, D, D), D ** -0.5),
        "ffn_w_gu": nrm(24, (DEPTH, D, 2 * D_FF), D ** -0.5),
        "ffn_w_down": nrm(25, (DEPTH, D_FF, D), D_FF ** -0.5),
        "final_w": 1.0 + nrm(26, (D,), 0.05),
    }


def reference(x, c, ada_w, ada_b, rw_mix, rw_w_in, rw_w0, rw_w2, rw_a0, rw_a2, rw_g2, rw_k_k, rw_k_a,
              rw_r_k, rw_gn_w, rw_gn_b, rw_w_out, ml_w_in, ml_conv_w, ml_conv_b, ml_b_i, ml_b_f, ml_hn_w,
              ml_w_out, ffn_w_gu, ffn_w_down, final_w):
    c_act = jax.nn.silu(c)
    for i in range(DEPTH):
        mod = c_act @ ada_w[i] + ada_b[i]
        sh1, sc1, g1, sh2, sc2, g2 = [t[:, None, :] for t in jnp.split(mod, 6, axis=-1)]
        h = _rms(x) * (1.0 + sc1) + sh1
        j = i // N_MIXERS
        if i % N_MIXERS == 0:
            y = _rwkv7_mix(h, rw_mix[j], rw_w_in[j], rw_w0[j], rw_w2[j], rw_a0[j], rw_a2[j], rw_g2[j],
                           rw_k_k[j], rw_k_a[j], rw_r_k[j], rw_gn_w[j], rw_gn_b[j], rw_w_out[j])
        else:
            y = _mlstm_mix(h, ml_w_in[j], ml_conv_w[j], ml_conv_b[j], ml_b_i[j], ml_b_f[j], ml_hn_w[j], ml_w_out[j])
        x = x + g1 * y
        h = _rms(x) * (1.0 + sc2) + sh2
        x = x + g2 * _swiglu(h, ffn_w_gu[i], ffn_w_down[i])
    return _rms(x) * final_w
```

```python
import functools

import jax
import jax.numpy as jnp
from jax import lax
from jax.experimental import pallas as pl
from jax.experimental.pallas import tpu as pltpu

F32 = jnp.float32
BF16 = jnp.bfloat16

EPS = 1e-6
GN_EPS = 64e-5
NEG = -1e30
GATE_CAP = 15.0
CHUNK = 64
LANES = 128
RW_HEAD = 64
LORA_W = 64
LORA_A = 64
LORA_G = 160
ML_HEADS = 8
ML_QK = 64
ML_V = 128
CONV_W = 4
VMEM_LIMIT = 56 * 1024 * 1024

NT_DIMS = (((1,), (1,)), ((), ()))
TN_DIMS = (((0,), (0,)), ((), ()))


def _dot(a, b):
    return jnp.dot(a, b, preferred_element_type=F32)


def _dot_nt(a, b):
    return lax.dot_general(a, b, NT_DIMS, preferred_element_type=F32)


def _dot_tn(a, b):
    return lax.dot_general(a, b, TN_DIMS, preferred_element_type=F32)


def _sigmoid(z):
    return 1.0 / (1.0 + jnp.exp(-z))


def _softplus(z):
    return jnp.maximum(z, 0.0) + jnp.log(1.0 + jnp.exp(-jnp.abs(z)))


def _rms_mod(xv, sc, sh):
    inv = lax.rsqrt(jnp.mean(xv * xv, axis=-1, keepdims=True) + EPS)
    return xv * inv * (1.0 + sc) + sh


def _split_bf16(z):
    hi = z.astype(BF16)
    lo = (z - hi.astype(F32)).astype(BF16)
    return hi, lo


def _params(sem):
    return pltpu.CompilerParams(dimension_semantics=sem, vmem_limit_bytes=VMEM_LIMIT)


def _const_spec(shape):
    nd = len(shape)
    return pl.BlockSpec(shape, lambda *_: (0,) * nd, pipeline_mode=pl.Buffered(1))


def _mod_kernel(c_ref, w_ref, b_ref, o_ref):
    c = c_ref[...]
    ca = (c * _sigmoid(c)).astype(BF16)
    o_ref[0] = _dot(ca, w_ref[0].astype(BF16)) + b_ref[0]


def _adaln_mod(c, ada_w, ada_b):
    depth, d, d6 = ada_w.shape
    b = c.shape[0]
    bp = 8
    c_pad = jnp.zeros((bp, d), F32).at[:b].set(c)
    nblk = d6 // d
    out = pl.pallas_call(
        _mod_kernel,
        grid=(depth, nblk),
        in_specs=[
            pl.BlockSpec((bp, d), lambda i, j: (0, 0)),
            pl.BlockSpec((1, d, d), lambda i, j: (i, 0, j)),
            pl.BlockSpec((1, 1, d), lambda i, j: (i, 0, j)),
        ],
        out_specs=pl.BlockSpec((1, bp, d), lambda i, j: (i, 0, j)),
        out_shape=jax.ShapeDtypeStruct((depth, bp, d6), F32),
        compiler_params=_params(("arbitrary", "arbitrary")),
        name="adaln_mod",
    )(c_pad, ada_w, ada_b.reshape(depth, 1, d6))
    mod = out[:, :b].reshape(depth, b, nblk, d)
    return jnp.pad(mod, ((0, 0), (0, 0), (0, 8 - nblk), (0, 0)))


def _rwkv_proj_kernel(x_ref, xp_ref, mod_ref, mix_ref, vec_ref, wr_ref, wk_ref, wv_ref,
                      wwl_ref, wal_ref, wgl_ref, w2_ref, a2_ref, g2_ref,
                      r_out, k_out, v_out, kk_out, a_out, lw_out, g_out, hbuf):
    tm = x_ref.shape[1]
    t = pl.program_id(1)
    sh = mod_ref[0, 0:1, :]
    sc = mod_ref[0, 1:2, :]
    h = _rms_mod(x_ref[0], sc, sh)
    hp = _rms_mod(xp_ref[0], sc, sh)
    hp = jnp.where(t > 0, hp, 0.0)
    hbuf[0:8, :] = hp
    hbuf[8:, :] = h
    xx = hbuf[pl.ds(7, tm), :] - h

    def mixed(i):
        return (h + xx * mix_ref[i:i + 1, :]).astype(BF16)

    w0 = vec_ref[0:1, :]
    a0 = vec_ref[1:2, :]
    k_k = vec_ref[2:3, :]
    k_a = vec_ref[3:4, :]

    r = _dot(mixed(0), wr_ref[...])
    wl = _dot(mixed(1), wwl_ref[...])
    k = _dot(mixed(2), wk_ref[...])
    v = _dot(mixed(3), wv_ref[...])
    al = _dot(mixed(4), wal_ref[...])
    gl = _dot(mixed(5), wgl_ref[...])

    w_pre = w0 + _dot(jnp.tanh(wl).astype(BF16), w2_ref[...])
    w_log = -_softplus(-w_pre) - 0.5
    lw = -jnp.exp(w_log)
    a = _sigmoid(a0 + _dot(al.astype(BF16), a2_ref[...]))
    g = _dot(_sigmoid(gl).astype(BF16), g2_ref[...])
    kk = k * k_k
    k2 = k * (1.0 + (a - 1.0) * k_a)

    for p in range(r_out.shape[1]):
        sl = slice(p * LANES, (p + 1) * LANES)
        r_out[0, p] = r[:, sl]
        k_out[0, p] = k2[:, sl]
        v_out[0, p] = v[:, sl]
        kk_out[0, p] = kk[:, sl]
        a_out[0, p] = a[:, sl]
        lw_out[0, p] = lw[:, sl]
        g_out[0, p] = g[:, sl]


def _rwkv_proj(x, mod, mix, vecs, weights, tm):
    b, s, d = x.shape
    npair = d // LANES
    nt = s // tm
    blk8 = tm // 8
    pair_shape = jax.ShapeDtypeStruct((b, npair, s, LANES), F32)
    pair_spec = pl.BlockSpec((1, npair, tm, LANES), lambda i, j: (i, 0, j, 0))
    return pl.pallas_call(
        _rwkv_proj_kernel,
        grid=(b, nt),
        in_specs=[
            pl.BlockSpec((1, tm, d), lambda i, j: (i, j, 0)),
            pl.BlockSpec((1, 8, d), lambda i, j: (i, jnp.maximum(j * blk8 - 1, 0), 0)),
            pl.BlockSpec((1, 8, d), lambda i, j: (i, 0, 0)),
            _const_spec(mix.shape),
            _const_spec(vecs.shape),
        ] + [_const_spec(w.shape) for w in weights],
        out_specs=[pair_spec] * 7,
        out_shape=[pair_shape] * 7,
        scratch_shapes=[pltpu.VMEM((tm + 8, d), F32)],
        compiler_params=_params(("parallel", "arbitrary")),
        name="rwkv_proj",
    )(x, x, mod, mix, vecs, *weights)


def _rwkv_core_kernel(r_ref, k_ref, v_ref, kk_ref, a_ref, lw_ref, g_ref, vec_ref, o_ref, s_ref):
    tt = r_ref.shape[2]
    L = CHUNK
    t = pl.program_id(2)

    @pl.when(t == 0)
    def _():
        s_ref[...] = jnp.zeros_like(s_ref)

    lane = lax.broadcasted_iota(jnp.int32, (L, LANES), 1)
    m0 = lane < RW_HEAD
    row2 = lax.broadcasted_iota(jnp.int32, (2 * L, 2 * L), 0)
    col2 = lax.broadcasted_iota(jnp.int32, (2 * L, 2 * L), 1)
    same = (row2 // L) == (col2 // L)
    strict = same & ((row2 % L) > (col2 % L))
    incl = same & ((row2 % L) >= (col2 % L))
    eye2 = (row2 == col2).astype(F32)
    rowl = lax.broadcasted_iota(jnp.int32, (L, L), 0)
    coll = lax.broadcasted_iota(jnp.int32, (L, L), 1)
    tri = (rowl >= coll).astype(BF16)
    rs = lax.broadcasted_iota(jnp.int32, (LANES, LANES), 0)
    cs = lax.broadcasted_iota(jnp.int32, (LANES, LANES), 1)
    blockdiag = (rs // RW_HEAD) == (cs // RW_HEAD)

    r_k = vec_ref[0:1, :]
    gn_w = vec_ref[1:2, :]
    gn_b = vec_ref[2:3, :]

    def head_sum(z):
        lo = jnp.sum(jnp.where(m0, z, 0.0), axis=-1, keepdims=True)
        hi = jnp.sum(jnp.where(m0, 0.0, z), axis=-1, keepdims=True)
        return jnp.where(m0, lo, hi)

    def split_heads(z):
        return jnp.concatenate([jnp.where(m0, z, 0.0), jnp.where(m0, 0.0, z)], axis=0)

    def pick_heads(zs):
        return jnp.where(m0, zs[:L], zs[L:])

    for c in range(tt // L):
        rows = slice(c * L, (c + 1) * L)
        r = r_ref[0, 0, rows, :]
        k = k_ref[0, 0, rows, :]
        v = v_ref[0, 0, rows, :]
        kk = kk_ref[0, 0, rows, :]
        a = a_ref[0, 0, rows, :]
        lw = lw_ref[0, 0, rows, :]
        g = g_ref[0, 0, rows, :]

        kkn = kk / jnp.maximum(jnp.sqrt(head_sum(kk * kk)), 1e-12)
        bvec = kkn * a

        hi, lo = _split_bf16(lw)
        cum = _dot(tri, hi) + _dot(tri, lo)
        c_last = cum[L - 1:L, :]
        at = -kkn * jnp.exp(cum - lw)
        rt = r * jnp.exp(cum)
        inv_w = jnp.exp(-cum)
        bt = bvec * inv_w
        kt = k * inv_w
        d_last = jnp.exp(c_last - cum)
        bd = bvec * d_last
        kd = k * d_last

        lhs = jnp.concatenate([split_heads(at), split_heads(rt)], axis=0).astype(BF16)
        rhs = jnp.concatenate([split_heads(bt), split_heads(kt)], axis=0).astype(BF16)
        gram = _dot_nt(lhs, rhs)
        a_ab = jnp.where(strict, gram[:2 * L, :2 * L], 0.0)
        a_ak = jnp.where(strict, gram[:2 * L, 2 * L:], 0.0)
        r_b = jnp.where(incl, gram[2 * L:, :2 * L], 0.0)
        r_kk = jnp.where(incl, gram[2 * L:, 2 * L:], 0.0)

        pw = a_ab
        tinv = eye2 + pw
        n_sq = 1
        while (1 << n_sq) < L:
            pwb = pw.astype(BF16)
            pw = _dot(pwb, pwb)
            tinv = tinv + _dot(tinv.astype(BF16), pw.astype(BF16))
            n_sq += 1

        s_old = s_ref[...]
        xr = _dot_nt(jnp.concatenate([at, rt], axis=0).astype(BF16), s_old.astype(BF16))
        a_s = xr[:L]
        r_s = xr[L:]
        vb = v.astype(BF16)
        vs = jnp.concatenate([vb, vb], axis=0)
        xs = jnp.concatenate([a_s, a_s], axis=0) + _dot(a_ak.astype(BF16), vs)
        us = _dot(tinv.astype(BF16), xs.astype(BF16))
        ys = _dot(jnp.concatenate([r_b, r_kk], axis=1).astype(BF16),
                  jnp.concatenate([us.astype(BF16), vs], axis=0))
        y = r_s + pick_heads(ys)
        u = pick_heads(us)

        upd = _dot_tn(jnp.concatenate([u.astype(BF16), vb], axis=0),
                      jnp.concatenate([bd, kd], axis=0).astype(BF16))
        s_ref[...] = s_old * jnp.exp(c_last) + jnp.where(blockdiag, upd, 0.0)

        mu = head_sum(y) * (1.0 / RW_HEAD)
        yc = y - mu
        var = head_sum(yc * yc) * (1.0 / RW_HEAD)
        yn = yc * lax.rsqrt(var + GN_EPS) * gn_w + gn_b
        bonus = head_sum(r * k * r_k) * v
        o_ref[0, rows, :] = ((yn + bonus) * g).astype(o_ref.dtype)


def _rwkv_core(proj, vecs, b, s, d, tt):
    npair = d // LANES
    in_spec = pl.BlockSpec((1, 1, tt, LANES), lambda i, p, j: (i, p, j, 0))
    return pl.pallas_call(
        _rwkv_core_kernel,
        grid=(b, npair, s // tt),
        in_specs=[in_spec] * 7 + [pl.BlockSpec((8, LANES), lambda i, p, j: (0, p))],
        out_specs=pl.BlockSpec((1, tt, LANES), lambda i, p, j: (i, j, p)),
        out_shape=jax.ShapeDtypeStruct((b, s, d), BF16),
        scratch_shapes=[pltpu.VMEM((LANES, LANES), F32)],
        compiler_params=_params(("parallel", "parallel", "arbitrary")),
        name="rwkv_core",
    )(*proj, vecs)


def _post_ffn_kernel(x_ref, y_ref, mod_ref, fw_ref, wo_ref, wg_ref, wu_ref, wd_ref, o_ref,
                     *, ff_chunk, final):
    g1 = mod_ref[0, 2:3, :]
    sh2 = mod_ref[0, 3:4, :]
    sc2 = mod_ref[0, 4:5, :]
    g2 = mod_ref[0, 5:6, :]
    x1 = x_ref[0] + g1 * _dot(y_ref[0], wo_ref[...])
    h = _rms_mod(x1, sc2, sh2).astype(BF16)
    d_ff = wg_ref.shape[1]
    acc = jnp.zeros_like(x1)
    for j in range(d_ff // ff_chunk):
        cols = slice(j * ff_chunk, (j + 1) * ff_chunk)
        gate = _dot(h, wg_ref[:, cols])
        up = _dot(h, wu_ref[:, cols])
        act = (gate * _sigmoid(gate) * up).astype(BF16)
        acc = acc + _dot(act, wd_ref[cols, :])
    x2 = x1 + g2 * acc
    if final:
        inv = lax.rsqrt(jnp.mean(x2 * x2, axis=-1, keepdims=True) + EPS)
        x2 = x2 * inv * fw_ref[0:1, :]
    o_ref[0] = x2


def _post_ffn(x, y, mod, final_w, w_out, w_g, w_u, w_d, tm, final):
    b, s, d = x.shape
    d_ff = w_g.shape[1]
    ff_chunk = d_ff // 2 if (d_ff // 2) % LANES == 0 else d_ff
    kern = functools.partial(_post_ffn_kernel, ff_chunk=ff_chunk, final=final)
    tok = lambda i, j: (i, j, 0)
    return pl.pallas_call(
        kern,
        grid=(b, s // tm),
        in_specs=[
            pl.BlockSpec((1, tm, d), tok),
            pl.BlockSpec((1, tm, d), tok),
            pl.BlockSpec((1, 8, d), lambda i, j: (i, 0, 0)),
            _const_spec(final_w.shape),
            _const_spec(w_out.shape),
            _const_spec(w_g.shape),
            _const_spec(w_u.shape),
            _const_spec(w_d.shape),
        ],
        out_specs=pl.BlockSpec((1, tm, d), tok),
        out_shape=jax.ShapeDtypeStruct((b, s, d), F32),
        compiler_params=_params(("parallel", "parallel")),
        name="post_ffn_final" if final else "post_ffn",
    )(x, y, mod, final_w, w_out, w_g, w_u, w_d)


def _gate_act(z, is_input_gate):
    cap = GATE_CAP * jnp.tanh(z * (1.0 / GATE_CAP))
    return jnp.where(is_input_gate, cap, -_softplus(-cap))


def _mlstm_proj_kernel(x_ref, xp_ref, mod_ref, cv_ref, gbc_ref, gbr_ref, wqk_ref, wv_ref,
                       wo_ref, wgc_ref, wgr_ref,
                       q_out, k_out, v_out, og_out, gc_out, gr_out, pbuf):
    tm = x_ref.shape[1]
    t = pl.program_id(1)
    sh = mod_ref[0, 0:1, :]
    sc = mod_ref[0, 1:2, :]
    h = _rms_mod(x_ref[0], sc, sh).astype(BF16)
    hp = _rms_mod(xp_ref[0], sc, sh).astype(BF16)

    pqk = _dot(h, wqk_ref[...])
    ppv = jnp.where(t > 0, _dot(hp, wqk_ref[...]), 0.0)
    pbuf[0:8, :] = ppv
    pbuf[8:, :] = pqk
    conv = cv_ref[CONV_W:CONV_W + 1, :] + pqk * cv_ref[CONV_W - 1:CONV_W, :]
    for j in range(1, CONV_W):
        conv = conv + pbuf[pl.ds(8 - j, tm), :] * cv_ref[CONV_W - 1 - j:CONV_W - j, :]
    qk = conv * _sigmoid(conv)
    half = qk.shape[1] // 2
    q_out[0] = qk[:, :half] * (ML_QK ** -0.5)
    k_out[0] = qk[:, half:]
    v_out[0] = _dot(h, wv_ref[...])
    og_out[0] = _sigmoid(_dot(h, wo_ref[...]))

    zc = _dot(h, wgc_ref[...]) + gbc_ref[0:1, :]
    lane = lax.broadcasted_iota(jnp.int32, zc.shape, 1)
    gc_out[0] = _gate_act(zc, lane < ML_HEADS)
    zr = _dot_nt(wgr_ref[...], h) + gbr_ref[:, 0:1]
    sub = lax.broadcasted_iota(jnp.int32, zr.shape, 0)
    gr_out[0] = _gate_act(zr, sub < ML_HEADS)


def _mlstm_proj(x, mod, cv, gb_col, gb_row, weights, tm):
    b, s, d = x.shape
    w_qk, w_v, w_o, w_gc, w_gr = weights
    nqk = w_qk.shape[1] // 2
    ng = w_gc.shape[1]
    blk8 = tm // 8
    tok = lambda i, j: (i, j, 0)
    return pl.pallas_call(
        _mlstm_proj_kernel,
        grid=(b, s // tm),
        in_specs=[
            pl.BlockSpec((1, tm, d), tok),
            pl.BlockSpec((1, 8, d), lambda i, j: (i, jnp.maximum(j * blk8 - 1, 0), 0)),
            pl.BlockSpec((1, 8, d), lambda i, j: (i, 0, 0)),
            _const_spec(cv.shape),
            _const_spec(gb_col.shape),
            _const_spec(gb_row.shape),
        ] + [_const_spec(w.shape) for w in weights],
        out_specs=[
            pl.BlockSpec((1, tm, nqk), tok),
            pl.BlockSpec((1, tm, nqk), tok),
            pl.BlockSpec((1, tm, w_v.shape[1]), tok),
            pl.BlockSpec((1, tm, w_o.shape[1]), tok),
            pl.BlockSpec((1, tm, ng), tok),
            pl.BlockSpec((1, ng, tm), lambda i, j: (i, 0, j)),
        ],
        out_shape=[
            jax.ShapeDtypeStruct((b, s, nqk), F32),
            jax.ShapeDtypeStruct((b, s, nqk), F32),
            jax.ShapeDtypeStruct((b, s, w_v.shape[1]), F32),
            jax.ShapeDtypeStruct((b, s, w_o.shape[1]), F32),
            jax.ShapeDtypeStruct((b, s, ng), F32),
            jax.ShapeDtypeStruct((b, ng, s), F32),
        ],
        scratch_shapes=[pltpu.VMEM((tm + 8, w_qk.shape[1]), F32)],
        compiler_params=_params(("parallel", "arbitrary")),
        name="mlstm_proj",
    )(x, x, mod, cv, gb_col, gb_row, *weights)


def _mlstm_core_kernel(q_ref, k_ref, v_ref, og_ref, gc_ref, gr_ref, hn_ref, o_ref,
                       c_ref, n_ref, m_ref):
    tt = q_ref.shape[1]
    L = CHUNK
    t = pl.program_id(1)

    @pl.when(t == 0)
    def _():
        c_ref[...] = jnp.zeros_like(c_ref)
        n_ref[...] = jnp.zeros_like(n_ref)
        m_ref[...] = jnp.zeros_like(m_ref)

    lane = lax.broadcasted_iota(jnp.int32, (L, LANES), 1)
    m0 = lane < ML_QK
    rowl = lax.broadcasted_iota(jnp.int32, (L, L), 0)
    coll = lax.broadcasted_iota(jnp.int32, (L, L), 1)
    causal = rowl >= coll
    tri = causal.astype(BF16)
    tri_t = (rowl <= coll).astype(BF16)
    row_pair = lax.broadcasted_iota(jnp.int32, (LANES, 1), 0) < ML_QK
    lane_row = lax.broadcasted_iota(jnp.int32, (1, LANES), 1) < ML_QK

    for c in range(tt // L):
        rows = slice(c * L, (c + 1) * L)
        gc = gc_ref[0, rows, :]
        gr = gr_ref[0, :, rows]
        hi, lo = _split_bf16(gc)
        cum_c = _dot(tri, hi) + _dot(tri, lo)
        hi, lo = _split_bf16(gr)
        cum_r = _dot(hi, tri_t) + _dot(lo, tri_t)

        for p in range(ML_HEADS // 2):
            qp = q_ref[0, rows, p * LANES:(p + 1) * LANES]
            kp = k_ref[0, rows, p * LANES:(p + 1) * LANES]
            c_old = c_ref[p]
            n_old = n_ref[p:p + 1, :]
            kws = []
            scale_a = []
            for hh in range(2):
                head = 2 * p + hh
                mh = m0 if hh == 0 else jnp.logical_not(m0)
                bcol = cum_c[:, ML_HEADS + head:ML_HEADS + head + 1]
                icol = gc[:, head:head + 1]
                brow = cum_r[ML_HEADS + head:ML_HEADS + head + 1, :]
                irow = gr[head:head + 1, :]
                vh = v_ref[0, rows, head * ML_V:(head + 1) * ML_V].astype(BF16)

                dlog = jnp.where(causal, bcol - brow + irow, NEG)
                m_intra = jnp.max(dlog, axis=-1, keepdims=True)
                g_tot = bcol[L - 1:L, :]
                loc = g_tot - bcol + icol
                m_loc = jnp.max(loc, axis=0, keepdims=True)
                wl = jnp.exp(loc - m_loc)
                m_prev = m_ref[head:head + 1, 0:1]
                inter = bcol + m_prev
                m_t = jnp.maximum(inter, m_intra)
                w_inter = jnp.exp(inter - m_t)

                qm = jnp.where(mh, qp, 0.0)
                qmb = qm.astype(BF16)
                scores = _dot_nt(qmb, kp.astype(BF16))
                w_intra = jnp.exp(dlog - m_t) * scores
                num = w_inter * _dot(qmb, c_old.astype(BF16)) + _dot(w_intra.astype(BF16), vh)
                qn = jnp.sum(qm * n_old, axis=-1, keepdims=True)
                den = w_inter * qn + jnp.sum(w_intra, axis=-1, keepdims=True)
                den = jnp.maximum(jnp.abs(den), jnp.exp(-m_t))
                hv = num / den
                hv = hv * lax.rsqrt(jnp.mean(hv * hv, axis=-1, keepdims=True) + EPS)
                hv = hv * hn_ref[0:1, head * ML_V:(head + 1) * ML_V]
                o_ref[0, rows, head * ML_V:(head + 1) * ML_V] = (
                    hv * og_ref[0, rows, head * ML_V:(head + 1) * ML_V]).astype(o_ref.dtype)

                m_new = jnp.maximum(g_tot + m_prev, m_loc)
                s_a = jnp.exp(g_tot + m_prev - m_new)
                s_b = jnp.exp(m_loc - m_new)
                kws.append((jnp.where(mh, kp, 0.0) * (wl * s_b), vh))
                scale_a.append(s_a)
                m_ref[head:head + 1, :] = jnp.broadcast_to(m_new, (1, LANES))

            kw_all = jnp.concatenate([kws[0][0], kws[1][0]], axis=0)
            v_all = jnp.concatenate([kws[0][1], kws[1][1]], axis=0)
            c_loc = _dot_tn(kw_all.astype(BF16), v_all)
            c_ref[p] = jnp.where(row_pair, scale_a[0], scale_a[1]) * c_old + c_loc
            n_loc = jnp.sum(kws[0][0] + kws[1][0], axis=0, keepdims=True)
            n_ref[p:p + 1, :] = jnp.where(lane_row, scale_a[0], scale_a[1]) * n_old + n_loc


def _mlstm_core(q, k, v, og, gc, gr, hn_w, tt):
    b, s, dv = v.shape
    nqk = q.shape[2]
    ng = gc.shape[2]
    tok = lambda i, j: (i, j, 0)
    return pl.pallas_call(
        _mlstm_core_kernel,
        grid=(b, s // tt),
        in_specs=[
            pl.BlockSpec((1, tt, nqk), tok),
            pl.BlockSpec((1, tt, nqk), tok),
            pl.BlockSpec((1, tt, dv), tok),
            pl.BlockSpec((1, tt, dv), tok),
            pl.BlockSpec((1, tt, ng), tok),
            pl.BlockSpec((1, ng, tt), lambda i, j: (i, 0, j)),
            _const_spec(hn_w.shape),
        ],
        out_specs=pl.BlockSpec((1, tt, dv), tok),
        out_shape=jax.ShapeDtypeStruct((b, s, dv), BF16),
        scratch_shapes=[
            pltpu.VMEM((ML_HEADS // 2, LANES, ML_V), F32),
            pltpu.VMEM((8, LANES), F32),
            pltpu.VMEM((ML_HEADS, LANES), F32),
        ],
        compiler_params=_params(("parallel", "arbitrary")),
        name="mlstm_core",
    )(q, k, v, og, gc, gr, hn_w)


def _pad_rows(a, rows=8):
    return jnp.pad(a, ((0, rows - a.shape[0]), (0, 0)))


def kernel(x, c, ada_w, ada_b, rw_mix, rw_w_in, rw_w0, rw_w2, rw_a0, rw_a2, rw_g2, rw_k_k, rw_k_a, rw_r_k, rw_gn_w, rw_gn_b, rw_w_out, ml_w_in, ml_conv_w, ml_conv_b, ml_b_i, ml_b_f, ml_hn_w, ml_w_out, ffn_w_gu, ffn_w_down, final_w):
    b, s, d = x.shape
    d_ff = ffn_w_down.shape[1]
    tm = 256
    bf = lambda w: w.astype(BF16)
    final_w2 = final_w.reshape(1, d)

    mod = _adaln_mod(c, ada_w, ada_b)

    w_in = rw_w_in[0]
    o1 = d
    o2 = o1 + LORA_W
    o3 = o2 + d
    o4 = o3 + d
    o5 = o4 + LORA_A
    rw_weights = [bf(w_in[:, :o1]), bf(w_in[:, o2:o3]), bf(w_in[:, o3:o4]),
                  bf(w_in[:, o1:o2]), bf(w_in[:, o4:o5]), bf(w_in[:, o5:]),
                  bf(rw_w2[0]), bf(rw_a2[0]), bf(rw_g2[0])]
    mix = _pad_rows(rw_mix[0])
    vecs = _pad_rows(jnp.stack([rw_w0[0], rw_a0[0], rw_k_k[0], rw_k_a[0]]))
    proj = _rwkv_proj(x, mod[0], mix, vecs, rw_weights, tm)
    core_vecs = _pad_rows(jnp.stack([rw_r_k[0].reshape(d), rw_gn_w[0], rw_gn_b[0]]))
    y = _rwkv_core(proj, core_vecs, b, s, d, 256)
    x = _post_ffn(x, y, mod[0], final_w2, bf(rw_w_out[0]), bf(ffn_w_gu[0][:, :d_ff]),
                  bf(ffn_w_gu[0][:, d_ff:]), bf(ffn_w_down[0]), tm, False)

    w_in = ml_w_in[0]
    nqk = 2 * ML_HEADS * ML_QK
    nv = ML_HEADS * ML_V
    w_gates = w_in[:, nqk + 2 * nv:]
    ml_weights = [bf(w_in[:, :nqk]), bf(w_in[:, nqk:nqk + nv]), bf(w_in[:, nqk + nv:nqk + 2 * nv]),
                  bf(w_gates), bf(w_gates.T)]
    cv = _pad_rows(jnp.concatenate([ml_conv_w[0], ml_conv_b[0][None, :]], axis=0))
    gb = jnp.concatenate([ml_b_i[0], ml_b_f[0]])
    gb_col = _pad_rows(gb[None, :])
    gb_row = jnp.broadcast_to(gb[:, None], (gb.shape[0], LANES))
    q, k, v, og, gc, gr = _mlstm_proj(x, mod[1], cv, gb_col, gb_row, ml_weights, tm)
    y = _mlstm_core(q, k, v, og, gc, gr, ml_hn_w[0].reshape(1, nv), 128)
    x = _post_ffn(x, y, mod[1], final_w2, bf(ml_w_out[0]), bf(ffn_w_gu[1][:, :d_ff]),
                  bf(ffn_w_gu[1][:, d_ff:]), bf(ffn_w_down[1]), tm, True)
    return x
```

```python
import functools

import jax
import jax.numpy as jnp
from jax import lax
from jax.experimental import pallas as pl
from jax.experimental.pallas import tpu as pltpu

F32 = jnp.float32
BF16 = jnp.bfloat16

EPS = 1e-6
GN_EPS = 64e-5
NEG = -1e30
GATE_CAP = 15.0
CHUNK = 64
LANES = 128
RW_HEAD = 64
LORA_W = 64
LORA_A = 64
LORA_G = 160
ML_HEADS = 8
ML_QK = 64
ML_V = 128
CONV_W = 4
VMEM_LIMIT = 56 * 1024 * 1024

NT_DIMS = (((1,), (1,)), ((), ()))
TN_DIMS = (((0,), (0,)), ((), ()))


def _dot(a, b):
    return jnp.dot(a, b, preferred_element_type=F32)


def _dot_nt(a, b):
    return lax.dot_general(a, b, NT_DIMS, preferred_element_type=F32)


def _dot_tn(a, b):
    return lax.dot_general(a, b, TN_DIMS, preferred_element_type=F32)


def _sigmoid(z):
    return 1.0 / (1.0 + jnp.exp(-z))


def _softplus(z):
    return jnp.maximum(z, 0.0) + jnp.log(1.0 + jnp.exp(-jnp.abs(z)))


def _rms_mod(xv, sc, sh):
    inv = lax.rsqrt(jnp.mean(xv * xv, axis=-1, keepdims=True) + EPS)
    return xv * inv * (1.0 + sc) + sh


def _split_bf16(z):
    hi = z.astype(BF16)
    lo = (z - hi.astype(F32)).astype(BF16)
    return hi, lo


def _params(sem):
    return pltpu.CompilerParams(dimension_semantics=sem, vmem_limit_bytes=VMEM_LIMIT)


def _const_spec(shape):
    nd = len(shape)
    return pl.BlockSpec(shape, lambda *_: (0,) * nd, pipeline_mode=pl.Buffered(1))


def _mod_kernel(c_ref, w_ref, b_ref, o_ref):
    c = c_ref[...]
    ca = (c * _sigmoid(c)).astype(BF16)
    o_ref[0] = _dot(ca, w_ref[0].astype(BF16)) + b_ref[0]


def _adaln_mod(c, ada_w, ada_b):
    depth, d, d6 = ada_w.shape
    b = c.shape[0]
    bp = 8
    c_pad = jnp.zeros((bp, d), F32).at[:b].set(c)
    nblk = d6 // d
    out = pl.pallas_call(
        _mod_kernel,
        grid=(depth, nblk),
        in_specs=[
            pl.BlockSpec((bp, d), lambda i, j: (0, 0)),
            pl.BlockSpec((1, d, d), lambda i, j: (i, 0, j)),
            pl.BlockSpec((1, 1, d), lambda i, j: (i, 0, j)),
        ],
        out_specs=pl.BlockSpec((1, bp, d), lambda i, j: (i, 0, j)),
        out_shape=jax.ShapeDtypeStruct((depth, bp, d6), F32),
        compiler_params=_params(("arbitrary", "arbitrary")),
        name="adaln_mod",
    )(c_pad, ada_w, ada_b.reshape(depth, 1, d6))
    mod = out[:, :b].reshape(depth, b, nblk, d)
    return jnp.pad(mod, ((0, 0), (0, 0), (0, 8 - nblk), (0, 0)))


def _rwkv_proj_kernel(x_ref, xp_ref, mod_ref, mix_ref, vec_ref, wr_ref, wk_ref, wv_ref,
                      wwl_ref, wal_ref, wgl_ref, w2_ref, a2_ref, g2_ref,
                      r_out, k_out, v_out, kk_out, a_out, lw_out, g_out, hbuf):
    tm = x_ref.shape[1]
    t = pl.program_id(1)
    sh = mod_ref[0, 0:1, :]
    sc = mod_ref[0, 1:2, :]
    h = _rms_mod(x_ref[0], sc, sh)
    hp = _rms_mod(xp_ref[0], sc, sh)
    hp = jnp.where(t > 0, hp, 0.0)
    hbuf[0:8, :] = hp
    hbuf[8:, :] = h
    xx = hbuf[pl.ds(7, tm), :] - h

    def mixed(i):
        return (h + xx * mix_ref[i:i + 1, :]).astype(BF16)

    w0 = vec_ref[0:1, :]
    a0 = vec_ref[1:2, :]
    k_k = vec_ref[2:3, :]
    k_a = vec_ref[3:4, :]

    r = _dot(mixed(0), wr_ref[...])
    wl = _dot(mixed(1), wwl_ref[...])
    k = _dot(mixed(2), wk_ref[...])
    v = _dot(mixed(3), wv_ref[...])
    al = _dot(mixed(4), wal_ref[...])
    gl = _dot(mixed(5), wgl_ref[...])

    w_pre = w0 + _dot(jnp.tanh(wl).astype(BF16), w2_ref[...])
    w_log = -_softplus(-w_pre) - 0.5
    lw = -jnp.exp(w_log)
    a = _sigmoid(a0 + _dot(al.astype(BF16), a2_ref[...]))
    g = _dot(_sigmoid(gl).astype(BF16), g2_ref[...])
    kk = k * k_k
    k2 = k * (1.0 + (a - 1.0) * k_a)

    for p in range(r_out.shape[1]):
        sl = slice(p * LANES, (p + 1) * LANES)
        r_out[0, p] = r[:, sl]
        k_out[0, p] = k2[:, sl]
        v_out[0, p] = v[:, sl]
        kk_out[0, p] = kk[:, sl]
        a_out[0, p] = a[:, sl]
        lw_out[0, p] = lw[:, sl]
        g_out[0, p] = g[:, sl]


def _rwkv_proj(x, mod, mix, vecs, weights, tm):
    b, s, d = x.shape
    npair = d // LANES
    nt = s // tm
    blk8 = tm // 8
    pair_shape = jax.ShapeDtypeStruct((b, npair, s, LANES), F32)
    pair_spec = pl.BlockSpec((1, npair, tm, LANES), lambda i, j: (i, 0, j, 0))
    return pl.pallas_call(
        _rwkv_proj_kernel,
        grid=(b, nt),
        in_specs=[
            pl.BlockSpec((1, tm, d), lambda i, j: (i, j, 0)),
            pl.BlockSpec((1, 8, d), lambda i, j: (i, jnp.maximum(j * blk8 - 1, 0), 0)),
            pl.BlockSpec((1, 8, d), lambda i, j: (i, 0, 0)),
            _const_spec(mix.shape),
            _const_spec(vecs.shape),
        ] + [_const_spec(w.shape) for w in weights],
        out_specs=[pair_spec] * 7,
        out_shape=[pair_shape] * 7,
        scratch_shapes=[pltpu.VMEM((tm + 8, d), F32)],
        compiler_params=_params(("parallel", "arbitrary")),
        name="rwkv_proj",
    )(x, x, mod, mix, vecs, *weights)


def _rwkv_core_kernel(r_ref, k_ref, v_ref, kk_ref, a_ref, lw_ref, g_ref, vec_ref, o_ref, s_ref):
    npg = r_ref.shape[1]
    tt = r_ref.shape[2]
    L = CHUNK
    nch = tt // L
    t = pl.program_id(2)

    @pl.when(t == 0)
    def _():
        s_ref[...] = jnp.zeros_like(s_ref)

    lane = lax.broadcasted_iota(jnp.int32, (L, LANES), 1)
    m0 = lane < RW_HEAD
    row2 = lax.broadcasted_iota(jnp.int32, (2 * L, 2 * L), 0)
    col2 = lax.broadcasted_iota(jnp.int32, (2 * L, 2 * L), 1)
    same = (row2 // L) == (col2 // L)
    strict = same & ((row2 % L) > (col2 % L))
    incl = same & ((row2 % L) >= (col2 % L))
    eye2 = (row2 == col2).astype(F32)
    blockdiag = (row2 // RW_HEAD) == (col2 // RW_HEAD)
    rowt = lax.broadcasted_iota(jnp.int32, (tt, tt), 0)
    colt = lax.broadcasted_iota(jnp.int32, (tt, tt), 1)
    tri = (((rowt // L) == (colt // L)) & (rowt >= colt)).astype(BF16)
    zeros_l = jnp.zeros((L, LANES), BF16)
    zeros_2l = jnp.zeros((2 * L, LANES), BF16)

    def fold_heads(zs):
        return zs[:L] + zs[L:]

    def head_sum(z):
        lo = jnp.sum(jnp.where(m0, z, 0.0), axis=-1, keepdims=True)
        hi = jnp.sum(jnp.where(m0, 0.0, z), axis=-1, keepdims=True)
        return jnp.where(m0, lo, hi)

    def split_heads(z):
        return jnp.concatenate([jnp.where(m0, z, 0.0), jnp.where(m0, 0.0, z)], axis=0)

    def pick_heads(zs):
        return jnp.where(m0, zs[:L], zs[L:])

    units = [(p, c) for p in range(npg) for c in range(nch)]

    cums = []
    for p in range(npg):
        hi, lo = _split_bf16(lw_ref[0, p])
        both = _dot(tri, jnp.concatenate([hi, lo], axis=1))
        cums.append(both[:, :LANES] + both[:, LANES:])

    st = []
    for (p, c) in units:
        rows = slice(c * L, (c + 1) * L)
        r = r_ref[0, p, rows, :]
        k = k_ref[0, p, rows, :]
        v = v_ref[0, p, rows, :]
        kk = kk_ref[0, p, rows, :]
        a = a_ref[0, p, rows, :]
        lw = lw_ref[0, p, rows, :]
        cum = cums[p][rows, :]
        kkn = kk / jnp.maximum(jnp.sqrt(head_sum(kk * kk)), 1e-12)
        bvec = kkn * a
        c_last = cum[L - 1:L, :]
        at = -kkn * jnp.exp(cum - lw)
        rt = r * jnp.exp(cum)
        inv_w = jnp.exp(-cum)
        d_last = jnp.exp(c_last - cum)
        vb = v.astype(BF16)
        st.append(dict(
            p=p, rows=rows, at=at, rt=rt, vb=vb, w_last=jnp.exp(c_last),
            lhs=jnp.concatenate([split_heads(at), split_heads(rt)], axis=0).astype(BF16),
            rhs=jnp.concatenate([split_heads(bvec * inv_w), split_heads(k * inv_w)],
                                axis=0).astype(BF16),
            dk=jnp.concatenate([bvec * d_last, k * d_last], axis=0).astype(BF16),
            vs=jnp.concatenate([vb, vb], axis=0),
        ))

    for u in st:
        gram = _dot_nt(u["lhs"], u["rhs"])
        u["pw"] = jnp.where(strict, gram[:2 * L, :2 * L], 0.0)
        u["a_ak"] = jnp.where(strict, gram[:2 * L, 2 * L:], 0.0).astype(BF16)
        u["rbk"] = jnp.where(jnp.concatenate([incl, incl], axis=1), gram[2 * L:, :], 0.0).astype(BF16)
        u["tinv"] = eye2 + u["pw"]
    for u in st:
        u["akv"] = _dot(u["a_ak"], u["vs"])

    n_sq = 1
    while (1 << n_sq) < L:
        for u in st:
            pwb = u["pw"].astype(BF16)
            u["pw"] = _dot(pwb, pwb)
        for u in st:
            u["tinv"] = u["tinv"] + _dot(u["tinv"].astype(BF16), u["pw"].astype(BF16))
        n_sq += 1

    for u in st:
        xin = jnp.concatenate([split_heads(u["at"]), u["akv"]], axis=1).astype(BF16)
        u["z"] = _dot(u["tinv"].astype(BF16), xin)
    for u in st:
        low = jnp.concatenate([zeros_2l, u["vs"]], axis=1)
        w = _dot(u["rbk"], jnp.concatenate([u["z"].astype(BF16), low], axis=0))
        u["rbar"] = (u["rt"] + fold_heads(w[:, :LANES])).astype(BF16)
        u["y0"] = pick_heads(w[:, LANES:])
    for u in st:
        z = u["z"]
        top = jnp.concatenate([fold_heads(z[:, :LANES]), pick_heads(z[:, LANES:])], axis=1)
        bot = jnp.concatenate([zeros_l, u["vb"]], axis=1)
        ms = _dot_tn(jnp.concatenate([top.astype(BF16), bot], axis=0), u["dk"])
        u["m_lr"] = jnp.where(blockdiag, ms[:LANES], 0.0).astype(BF16)
        u["s0"] = jnp.where(blockdiag, ms[LANES:], 0.0)

    states = [s_ref[p] for p in range(npg)]
    for c in range(nch):
        for p in range(npg):
            u = st[p * nch + c]
            s_old = states[p]
            sb = s_old.astype(BF16)
            u["y"] = _dot_nt(u["rbar"], sb) + u["y0"]
            states[p] = s_old * u["w_last"] + _dot(sb, u["m_lr"]) + u["s0"]
    for p in range(npg):
        s_ref[p] = states[p]

    for u in st:
        p, rows = u["p"], u["rows"]
        lanes = slice(p * LANES, (p + 1) * LANES)
        y = u["y"]
        mu = head_sum(y) * (1.0 / RW_HEAD)
        yc = y - mu
        var = head_sum(yc * yc) * (1.0 / RW_HEAD)
        yn = yc * lax.rsqrt(var + GN_EPS) * vec_ref[1:2, lanes] + vec_ref[2:3, lanes]
        r = r_ref[0, p, rows, :]
        k = k_ref[0, p, rows, :]
        bonus = head_sum(r * k * vec_ref[0:1, lanes]) * v_ref[0, p, rows, :]
        o_ref[0, rows, lanes] = ((yn + bonus) * g_ref[0, p, rows, :]).astype(o_ref.dtype)


def _rwkv_core(proj, vecs, b, s, d, tt, npg):
    npair = d // LANES
    in_spec = pl.BlockSpec((1, npg, tt, LANES), lambda i, p, j: (i, p, j, 0))
    return pl.pallas_call(
        _rwkv_core_kernel,
        grid=(b, npair // npg, s // tt),
        in_specs=[in_spec] * 7 + [pl.BlockSpec((8, npg * LANES), lambda i, p, j: (0, p))],
        out_specs=pl.BlockSpec((1, tt, npg * LANES), lambda i, p, j: (i, j, p)),
        out_shape=jax.ShapeDtypeStruct((b, s, d), BF16),
        scratch_shapes=[pltpu.VMEM((npg, LANES, LANES), F32)],
        compiler_params=_params(("parallel", "parallel", "arbitrary")),
        name="rwkv_core",
    )(*proj, vecs)


def _post_ffn_kernel(x_ref, y_ref, mod_ref, fw_ref, wo_ref, wg_ref, wu_ref, wd_ref, o_ref,
                     *, ff_chunk, final):
    g1 = mod_ref[0, 2:3, :]
    sh2 = mod_ref[0, 3:4, :]
    sc2 = mod_ref[0, 4:5, :]
    g2 = mod_ref[0, 5:6, :]
    x1 = x_ref[0] + g1 * _dot(y_ref[0], wo_ref[...])
    h = _rms_mod(x1, sc2, sh2).astype(BF16)
    d_ff = wg_ref.shape[1]
    acc = jnp.zeros_like(x1)
    for j in range(d_ff // ff_chunk):
        cols = slice(j * ff_chunk, (j + 1) * ff_chunk)
        gate = _dot(h, wg_ref[:, cols])
        up = _dot(h, wu_ref[:, cols])
        act = (gate * _sigmoid(gate) * up).astype(BF16)
        acc = acc + _dot(act, wd_ref[cols, :])
    x2 = x1 + g2 * acc
    if final:
        inv = lax.rsqrt(jnp.mean(x2 * x2, axis=-1, keepdims=True) + EPS)
        x2 = x2 * inv * fw_ref[0:1, :]
    o_ref[0] = x2


def _post_ffn(x, y, mod, final_w, w_out, w_g, w_u, w_d, tm, final):
    b, s, d = x.shape
    d_ff = w_g.shape[1]
    ff_chunk = d_ff // 2 if (d_ff // 2) % LANES == 0 else d_ff
    kern = functools.partial(_post_ffn_kernel, ff_chunk=ff_chunk, final=final)
    tok = lambda i, j: (i, j, 0)
    return pl.pallas_call(
        kern,
        grid=(b, s // tm),
        in_specs=[
            pl.BlockSpec((1, tm, d), tok),
            pl.BlockSpec((1, tm, d), tok),
            pl.BlockSpec((1, 8, d), lambda i, j: (i, 0, 0)),
            _const_spec(final_w.shape),
            _const_spec(w_out.shape),
            _const_spec(w_g.shape),
            _const_spec(w_u.shape),
            _const_spec(w_d.shape),
        ],
        out_specs=pl.BlockSpec((1, tm, d), tok),
        out_shape=jax.ShapeDtypeStruct((b, s, d), F32),
        compiler_params=_params(("parallel", "parallel")),
        name="post_ffn_final" if final else "post_ffn",
    )(x, y, mod, final_w, w_out, w_g, w_u, w_d)


def _gate_act(z, is_input_gate):
    cap = GATE_CAP * jnp.tanh(z * (1.0 / GATE_CAP))
    return jnp.where(is_input_gate, cap, -_softplus(-cap))


def _mlstm_proj_kernel(x_ref, xp_ref, mod_ref, cv_ref, gbc_ref, gbr_ref, wqk_ref, wv_ref,
                       wo_ref, wgc_ref, wgr_ref,
                       q_out, k_out, v_out, og_out, gc_out, gr_out, pbuf):
    tm = x_ref.shape[1]
    t = pl.program_id(1)
    sh = mod_ref[0, 0:1, :]
    sc = mod_ref[0, 1:2, :]
    h = _rms_mod(x_ref[0], sc, sh).astype(BF16)
    hp = _rms_mod(xp_ref[0], sc, sh).astype(BF16)

    pqk = _dot(h, wqk_ref[...])
    ppv = jnp.where(t > 0, _dot(hp, wqk_ref[...]), 0.0)
    pbuf[0:8, :] = ppv
    pbuf[8:, :] = pqk
    conv = cv_ref[CONV_W:CONV_W + 1, :] + pqk * cv_ref[CONV_W - 1:CONV_W, :]
    for j in range(1, CONV_W):
        conv = conv + pbuf[pl.ds(8 - j, tm), :] * cv_ref[CONV_W - 1 - j:CONV_W - j, :]
    qk = conv * _sigmoid(conv)
    half = qk.shape[1] // 2
    q_out[0] = qk[:, :half] * (ML_QK ** -0.5)
    k_out[0] = qk[:, half:]
    v_out[0] = _dot(h, wv_ref[...])
    og_out[0] = _sigmoid(_dot(h, wo_ref[...]))

    zc = _dot(h, wgc_ref[...]) + gbc_ref[0:1, :]
    lane = lax.broadcasted_iota(jnp.int32, zc.shape, 1)
    gc_out[0] = _gate_act(zc, lane < ML_HEADS)
    zr = _dot_nt(wgr_ref[...], h) + gbr_ref[:, 0:1]
    sub = lax.broadcasted_iota(jnp.int32, zr.shape, 0)
    gr_out[0] = _gate_act(zr, sub < ML_HEADS)


def _mlstm_proj(x, mod, cv, gb_col, gb_row, weights, tm):
    b, s, d = x.shape
    w_qk, w_v, w_o, w_gc, w_gr = weights
    nqk = w_qk.shape[1] // 2
    ng = w_gc.shape[1]
    blk8 = tm // 8
    tok = lambda i, j: (i, j, 0)
    return pl.pallas_call(
        _mlstm_proj_kernel,
        grid=(b, s // tm),
        in_specs=[
            pl.BlockSpec((1, tm, d), tok),
            pl.BlockSpec((1, 8, d), lambda i, j: (i, jnp.maximum(j * blk8 - 1, 0), 0)),
            pl.BlockSpec((1, 8, d), lambda i, j: (i, 0, 0)),
            _const_spec(cv.shape),
            _const_spec(gb_col.shape),
            _const_spec(gb_row.shape),
        ] + [_const_spec(w.shape) for w in weights],
        out_specs=[
            pl.BlockSpec((1, tm, nqk), tok),
            pl.BlockSpec((1, tm, nqk), tok),
            pl.BlockSpec((1, tm, w_v.shape[1]), tok),
            pl.BlockSpec((1, tm, w_o.shape[1]), tok),
            pl.BlockSpec((1, tm, ng), tok),
            pl.BlockSpec((1, ng, tm), lambda i, j: (i, 0, j)),
        ],
        out_shape=[
            jax.ShapeDtypeStruct((b, s, nqk), F32),
            jax.ShapeDtypeStruct((b, s, nqk), F32),
            jax.ShapeDtypeStruct((b, s, w_v.shape[1]), F32),
            jax.ShapeDtypeStruct((b, s, w_o.shape[1]), F32),
            jax.ShapeDtypeStruct((b, s, ng), F32),
            jax.ShapeDtypeStruct((b, ng, s), F32),
        ],
        scratch_shapes=[pltpu.VMEM((tm + 8, w_qk.shape[1]), F32)],
        compiler_params=_params(("parallel", "arbitrary")),
        name="mlstm_proj",
    )(x, x, mod, cv, gb_col, gb_row, *weights)


def _mlstm_core_kernel(q_ref, k_ref, v_ref, og_ref, gc_ref, gr_ref, hn_ref, o_ref,
                       c_ref, n_ref, m_ref):
    tt = q_ref.shape[1]
    L = CHUNK
    t = pl.program_id(1)

    @pl.when(t == 0)
    def _():
        c_ref[...] = jnp.zeros_like(c_ref)
        n_ref[...] = jnp.zeros_like(n_ref)
        m_ref[...] = jnp.zeros_like(m_ref)

    lane = lax.broadcasted_iota(jnp.int32, (L, LANES), 1)
    m0 = lane < ML_QK
    rowl = lax.broadcasted_iota(jnp.int32, (L, L), 0)
    coll = lax.broadcasted_iota(jnp.int32, (L, L), 1)
    causal = rowl >= coll
    rowt = lax.broadcasted_iota(jnp.int32, (tt, tt), 0)
    colt = lax.broadcasted_iota(jnp.int32, (tt, tt), 1)
    same_chunk = (rowt // L) == (colt // L)
    tri = (same_chunk & (rowt >= colt)).astype(BF16)
    tri_t = (same_chunk & (rowt <= colt)).astype(BF16)
    row_pair = lax.broadcasted_iota(jnp.int32, (LANES, 1), 0) < ML_QK
    lane_row = lax.broadcasted_iota(jnp.int32, (1, LANES), 1) < ML_QK

    nch = tt // L
    npair = ML_HEADS // 2

    hi, lo = _split_bf16(gc_ref[0])
    both = _dot(tri, jnp.concatenate([hi, lo], axis=1))
    cum_c = both[:, :2 * ML_HEADS] + both[:, 2 * ML_HEADS:]
    hi, lo = _split_bf16(gr_ref[0])
    both = _dot(jnp.concatenate([hi, lo], axis=0), tri_t)
    cum_r = both[:2 * ML_HEADS] + both[2 * ML_HEADS:]

    m_run = [m_ref[h:h + 1, 0:1] for h in range(ML_HEADS)]
    un = {}
    for c in range(nch):
        rows = slice(c * L, (c + 1) * L)
        for head in range(ML_HEADS):
            p, hh = divmod(head, 2)
            mh = m0 if hh == 0 else jnp.logical_not(m0)
            bcol = cum_c[rows, ML_HEADS + head:ML_HEADS + head + 1]
            icol = gc_ref[0, rows, head:head + 1]
            brow = cum_r[ML_HEADS + head:ML_HEADS + head + 1, rows]
            irow = gr_ref[0, head:head + 1, rows]
            dlog = jnp.where(causal, bcol - brow + irow, NEG)
            m_intra = jnp.max(dlog, axis=-1, keepdims=True)
            g_tot = bcol[L - 1:L, :]
            loc = g_tot - bcol + icol
            m_loc = jnp.max(loc, axis=0, keepdims=True)
            m_prev = m_run[head]
            inter = bcol + m_prev
            m_t = jnp.maximum(inter, m_intra)
            m_new = jnp.maximum(g_tot + m_prev, m_loc)
            m_run[head] = m_new
            qp = q_ref[0, rows, p * LANES:(p + 1) * LANES]
            kp = k_ref[0, rows, p * LANES:(p + 1) * LANES]
            qm = jnp.where(mh, qp, 0.0)
            un[(c, head)] = dict(
                rows=rows, m_t=m_t, qm=qm, qmb=qm.astype(BF16), kb=kp.astype(BF16),
                w_inter=jnp.exp(inter - m_t), expd=jnp.exp(dlog - m_t),
                s_a=jnp.exp(g_tot + m_prev - m_new),
                kw=jnp.where(mh, kp, 0.0) * (jnp.exp(loc - m_loc) * jnp.exp(m_loc - m_new)),
                vh=v_ref[0, rows, head * ML_V:(head + 1) * ML_V].astype(BF16),
            )
    for head in range(ML_HEADS):
        m_ref[head:head + 1, :] = jnp.broadcast_to(m_run[head], (1, LANES))

    for u in un.values():
        u["w_intra"] = u["expd"] * _dot_nt(u["qmb"], u["kb"])

    c_loc = {}
    for c in range(nch):
        for p in range(npair):
            u0, u1 = un[(c, 2 * p)], un[(c, 2 * p + 1)]
            kw_all = jnp.concatenate([u0["kw"], u1["kw"]], axis=0).astype(BF16)
            v_all = jnp.concatenate([u0["vh"], u1["vh"]], axis=0)
            c_loc[(c, p)] = _dot_tn(kw_all, v_all)
    c_state = [c_ref[p] for p in range(npair)]
    n_state = [n_ref[p:p + 1, :] for p in range(npair)]
    for c in range(nch):
        for p in range(npair):
            u0, u1 = un[(c, 2 * p)], un[(c, 2 * p + 1)]
            for u in (u0, u1):
                u["c_old"] = c_state[p].astype(BF16)
                u["n_old"] = n_state[p]
            c_state[p] = jnp.where(row_pair, u0["s_a"], u1["s_a"]) * c_state[p] + c_loc[(c, p)]
            n_loc = jnp.sum(u0["kw"] + u1["kw"], axis=0, keepdims=True)
            n_state[p] = jnp.where(lane_row, u0["s_a"], u1["s_a"]) * n_state[p] + n_loc
    for p in range(npair):
        c_ref[p] = c_state[p]
        n_ref[p:p + 1, :] = n_state[p]

    for u in un.values():
        u["qc"] = _dot(u["qmb"], u["c_old"])
    for u in un.values():
        u["wv"] = _dot(u["w_intra"].astype(BF16), u["vh"])

    for (c, head), u in un.items():
        rows = u["rows"]
        cols = slice(head * ML_V, (head + 1) * ML_V)
        num = u["w_inter"] * u["qc"] + u["wv"]
        qn = jnp.sum(u["qm"] * u["n_old"], axis=-1, keepdims=True)
        den = u["w_inter"] * qn + jnp.sum(u["w_intra"], axis=-1, keepdims=True)
        den = jnp.maximum(jnp.abs(den), jnp.exp(-u["m_t"]))
        hv = num / den
        hv = hv * lax.rsqrt(jnp.mean(hv * hv, axis=-1, keepdims=True) + EPS)
        hv = hv * hn_ref[0:1, cols]
        o_ref[0, rows, cols] = (hv * og_ref[0, rows, cols]).astype(o_ref.dtype)


def _mlstm_core(q, k, v, og, gc, gr, hn_w, tt):
    b, s, dv = v.shape
    nqk = q.shape[2]
    ng = gc.shape[2]
    tok = lambda i, j: (i, j, 0)
    return pl.pallas_call(
        _mlstm_core_kernel,
        grid=(b, s // tt),
        in_specs=[
            pl.BlockSpec((1, tt, nqk), tok),
            pl.BlockSpec((1, tt, nqk), tok),
            pl.BlockSpec((1, tt, dv), tok),
            pl.BlockSpec((1, tt, dv), tok),
            pl.BlockSpec((1, tt, ng), tok),
            pl.BlockSpec((1, ng, tt), lambda i, j: (i, 0, j)),
            _const_spec(hn_w.shape),
        ],
        out_specs=pl.BlockSpec((1, tt, dv), tok),
        out_shape=jax.ShapeDtypeStruct((b, s, dv), BF16),
        scratch_shapes=[
            pltpu.VMEM((ML_HEADS // 2, LANES, ML_V), F32),
            pltpu.VMEM((8, LANES), F32),
            pltpu.VMEM((ML_HEADS, LANES), F32),
        ],
        compiler_params=_params(("parallel", "arbitrary")),
        name="mlstm_core",
    )(q, k, v, og, gc, gr, hn_w)


def _pad_rows(a, rows=8):
    return jnp.pad(a, ((0, rows - a.shape[0]), (0, 0)))


def kernel(x, c, ada_w, ada_b, rw_mix, rw_w_in, rw_w0, rw_w2, rw_a0, rw_a2, rw_g2, rw_k_k, rw_k_a, rw_r_k, rw_gn_w, rw_gn_b, rw_w_out, ml_w_in, ml_conv_w, ml_conv_b, ml_b_i, ml_b_f, ml_hn_w, ml_w_out, ffn_w_gu, ffn_w_down, final_w):
    b, s, d = x.shape
    d_ff = ffn_w_down.shape[1]
    tm = 256
    bf = lambda w: w.astype(BF16)
    final_w2 = final_w.reshape(1, d)

    mod = _adaln_mod(c, ada_w, ada_b)

    w_in = rw_w_in[0]
    o1 = d
    o2 = o1 + LORA_W
    o3 = o2 + d
    o4 = o3 + d
    o5 = o4 + LORA_A
    rw_weights = [bf(w_in[:, :o1]), bf(w_in[:, o2:o3]), bf(w_in[:, o3:o4]),
                  bf(w_in[:, o1:o2]), bf(w_in[:, o4:o5]), bf(w_in[:, o5:]),
                  bf(rw_w2[0]), bf(rw_a2[0]), bf(rw_g2[0])]
    mix = _pad_rows(rw_mix[0])
    vecs = _pad_rows(jnp.stack([rw_w0[0], rw_a0[0], rw_k_k[0], rw_k_a[0]]))
    proj = _rwkv_proj(x, mod[0], mix, vecs, rw_weights, tm)
    core_vecs = _pad_rows(jnp.stack([rw_r_k[0].reshape(d), rw_gn_w[0], rw_gn_b[0]]))
    y = _rwkv_core(proj, core_vecs, b, s, d, 256, 2)
    x = _post_ffn(x, y, mod[0], final_w2, bf(rw_w_out[0]), bf(ffn_w_gu[0][:, :d_ff]),
                  bf(ffn_w_gu[0][:, d_ff:]), bf(ffn_w_down[0]), tm, False)

    w_in = ml_w_in[0]
    nqk = 2 * ML_HEADS * ML_QK
    nv = ML_HEADS * ML_V
    w_gates = w_in[:, nqk + 2 * nv:]
    ml_weights = [bf(w_in[:, :nqk]), bf(w_in[:, nqk:nqk + nv]), bf(w_in[:, nqk + nv:nqk + 2 * nv]),
                  bf(w_gates), bf(w_gates.T)]
    cv = _pad_rows(jnp.concatenate([ml_conv_w[0], ml_conv_b[0][None, :]], axis=0))
    gb = jnp.concatenate([ml_b_i[0], ml_b_f[0]])
    gb_col = _pad_rows(gb[None, :])
    gb_row = jnp.broadcast_to(gb[:, None], (gb.shape[0], LANES))
    q, k, v, og, gc, gr = _mlstm_proj(x, mod[1], cv, gb_col, gb_row, ml_weights, tm)
    y = _mlstm_core(q, k, v, og, gc, gr, ml_hn_w[0].reshape(1, nv), 128)
    x = _post_ffn(x, y, mod[1], final_w2, bf(ml_w_out[0]), bf(ffn_w_gu[1][:, :d_ff]),
                  bf(ffn_w_gu[1][:, d_ff:]), bf(ffn_w_down[1]), tm, True)
    return x
```

```python
import functools

import jax
import jax.numpy as jnp
from jax import lax
from jax.experimental import pallas as pl
from jax.experimental.pallas import tpu as pltpu

F32 = jnp.float32
BF16 = jnp.bfloat16

EPS = 1e-6
GN_EPS = 64e-5
NEG = -1e30
GATE_CAP = 15.0
CHUNK = 64
LANES = 128
RW_HEAD = 64
LORA_W = 64
LORA_A = 64
LORA_G = 160
ML_HEADS = 8
ML_QK = 64
ML_V = 128
CONV_W = 4
VMEM_LIMIT = 56 * 1024 * 1024

NT_DIMS = (((1,), (1,)), ((), ()))
TN_DIMS = (((0,), (0,)), ((), ()))


def _dot(a, b):
    return jnp.dot(a, b, preferred_element_type=F32)


def _dot_nt(a, b):
    return lax.dot_general(a, b, NT_DIMS, preferred_element_type=F32)


def _dot_tn(a, b):
    return lax.dot_general(a, b, TN_DIMS, preferred_element_type=F32)


def _sigmoid(z):
    return 1.0 / (1.0 + jnp.exp(-z))


def _softplus(z):
    return jnp.maximum(z, 0.0) + jnp.log(1.0 + jnp.exp(-jnp.abs(z)))


def _rms_mod(xv, sc, sh):
    inv = lax.rsqrt(jnp.mean(xv * xv, axis=-1, keepdims=True) + EPS)
    return xv * inv * (1.0 + sc) + sh


def _split_bf16(z):
    hi = z.astype(BF16)
    lo = (z - hi.astype(F32)).astype(BF16)
    return hi, lo


def _params(sem):
    return pltpu.CompilerParams(dimension_semantics=sem, vmem_limit_bytes=VMEM_LIMIT)


def _const_spec(shape):
    nd = len(shape)
    return pl.BlockSpec(shape, lambda *_: (0,) * nd, pipeline_mode=pl.Buffered(1))


def _mod_kernel(c_ref, w_ref, b_ref, o_ref):
    c = c_ref[...]
    ca = (c * _sigmoid(c)).astype(BF16)
    o_ref[0] = _dot(ca, w_ref[0].astype(BF16)) + b_ref[0]


def _adaln_mod(c, ada_w, ada_b):
    depth, d, d6 = ada_w.shape
    b = c.shape[0]
    bp = 8
    c_pad = jnp.zeros((bp, d), F32).at[:b].set(c)
    nblk = d6 // d
    out = pl.pallas_call(
        _mod_kernel,
        grid=(depth, nblk),
        in_specs=[
            pl.BlockSpec((bp, d), lambda i, j: (0, 0)),
            pl.BlockSpec((1, d, d), lambda i, j: (i, 0, j)),
            pl.BlockSpec((1, 1, d), lambda i, j: (i, 0, j)),
        ],
        out_specs=pl.BlockSpec((1, bp, d), lambda i, j: (i, 0, j)),
        out_shape=jax.ShapeDtypeStruct((depth, bp, d6), F32),
        compiler_params=_params(("arbitrary", "arbitrary")),
        name="adaln_mod",
    )(c_pad, ada_w, ada_b.reshape(depth, 1, d6))
    mod = out[:, :b].reshape(depth, b, nblk, d)
    return jnp.pad(mod, ((0, 0), (0, 0), (0, 8 - nblk), (0, 0)))


def _rwkv_proj_kernel(x_ref, xp_ref, mod_ref, mix_ref, vec_ref, wr_ref, wk_ref, wv_ref,
                      wwl_ref, wal_ref, wgl_ref, w2_ref, a2_ref, g2_ref,
                      r_out, k_out, v_out, kk_out, a_out, lw_out, g_out, hbuf):
    tm = x_ref.shape[1]
    t = pl.program_id(1)
    sh = mod_ref[0, 0:1, :]
    sc = mod_ref[0, 1:2, :]
    h = _rms_mod(x_ref[0], sc, sh)
    hp = _rms_mod(xp_ref[0], sc, sh)
    hp = jnp.where(t > 0, hp, 0.0)
    hbuf[0:8, :] = hp
    hbuf[8:, :] = h
    xx = hbuf[pl.ds(7, tm), :] - h

    def mixed(i):
        return (h + xx * mix_ref[i:i + 1, :]).astype(BF16)

    w0 = vec_ref[0:1, :]
    a0 = vec_ref[1:2, :]
    k_k = vec_ref[2:3, :]
    k_a = vec_ref[3:4, :]

    r = _dot(mixed(0), wr_ref[...])
    wl = _dot(mixed(1), wwl_ref[...])
    k = _dot(mixed(2), wk_ref[...])
    v = _dot(mixed(3), wv_ref[...])
    al = _dot(mixed(4), wal_ref[...])
    gl = _dot(mixed(5), wgl_ref[...])

    w_pre = w0 + _dot(jnp.tanh(wl).astype(BF16), w2_ref[...])
    w_log = -_softplus(-w_pre) - 0.5
    lw = -jnp.exp(w_log)
    a = _sigmoid(a0 + _dot(al.astype(BF16), a2_ref[...]))
    g = _dot(_sigmoid(gl).astype(BF16), g2_ref[...])
    kk = k * k_k
    k2 = k * (1.0 + (a - 1.0) * k_a)

    for p in range(r_out.shape[1]):
        sl = slice(p * LANES, (p + 1) * LANES)
        r_out[0, p] = r[:, sl].astype(r_out.dtype)
        k_out[0, p] = k2[:, sl].astype(k_out.dtype)
        v_out[0, p] = v[:, sl].astype(v_out.dtype)
        kk_out[0, p] = kk[:, sl].astype(kk_out.dtype)
        a_out[0, p] = a[:, sl].astype(a_out.dtype)
        lw_out[0, p] = lw[:, sl]
        g_out[0, p] = g[:, sl].astype(g_out.dtype)


def _rwkv_proj(x, mod, mix, vecs, weights, tm):
    b, s, d = x.shape
    npair = d // LANES
    nt = s // tm
    blk8 = tm // 8
    pair_shape = lambda dt: jax.ShapeDtypeStruct((b, npair, s, LANES), dt)
    pair_spec = pl.BlockSpec((1, npair, tm, LANES), lambda i, j: (i, 0, j, 0))
    return pl.pallas_call(
        _rwkv_proj_kernel,
        grid=(b, nt),
        in_specs=[
            pl.BlockSpec((1, tm, d), lambda i, j: (i, j, 0)),
            pl.BlockSpec((1, 8, d), lambda i, j: (i, jnp.maximum(j * blk8 - 1, 0), 0)),
            pl.BlockSpec((1, 8, d), lambda i, j: (i, 0, 0)),
            _const_spec(mix.shape),
            _const_spec(vecs.shape),
        ] + [_const_spec(w.shape) for w in weights],
        out_specs=[pair_spec] * 7,
        out_shape=[pair_shape(BF16)] * 5 + [pair_shape(F32), pair_shape(BF16)],
        scratch_shapes=[pltpu.VMEM((tm + 8, d), F32)],
        compiler_params=_params(("parallel", "arbitrary")),
        name="rwkv_proj",
    )(x, x, mod, mix, vecs, *weights)


def _rwkv_core_kernel(r_ref, k_ref, v_ref, kk_ref, a_ref, lw_ref, g_ref, vec_ref, o_ref, s_ref):
    npg = r_ref.shape[1]
    tt = r_ref.shape[2]
    L = CHUNK
    nch = tt // L
    t = pl.program_id(2)

    @pl.when(t == 0)
    def _():
        s_ref[...] = jnp.zeros_like(s_ref)

    lane = lax.broadcasted_iota(jnp.int32, (L, LANES), 1)
    m0 = lane < RW_HEAD
    row2 = lax.broadcasted_iota(jnp.int32, (2 * L, 2 * L), 0)
    col2 = lax.broadcasted_iota(jnp.int32, (2 * L, 2 * L), 1)
    same = (row2 // L) == (col2 // L)
    strict = same & ((row2 % L) > (col2 % L))
    incl = same & ((row2 % L) >= (col2 % L))
    eye2 = (row2 == col2).astype(F32)
    blockdiag = (row2 // RW_HEAD) == (col2 // RW_HEAD)
    rowt = lax.broadcasted_iota(jnp.int32, (tt, tt), 0)
    colt = lax.broadcasted_iota(jnp.int32, (tt, tt), 1)
    tri = (((rowt // L) == (colt // L)) & (rowt >= colt)).astype(BF16)
    zeros_l = jnp.zeros((L, LANES), BF16)
    zeros_2l = jnp.zeros((2 * L, LANES), BF16)

    def fold_heads(zs):
        return zs[:L] + zs[L:]

    def head_sum(z):
        lo = jnp.sum(jnp.where(m0, z, 0.0), axis=-1, keepdims=True)
        hi = jnp.sum(jnp.where(m0, 0.0, z), axis=-1, keepdims=True)
        return jnp.where(m0, lo, hi)

    def split_heads(z):
        return jnp.concatenate([jnp.where(m0, z, 0.0), jnp.where(m0, 0.0, z)], axis=0)

    def pick_heads(zs):
        return jnp.where(m0, zs[:L], zs[L:])

    units = [(p, c) for p in range(npg) for c in range(nch)]

    cums = []
    for p in range(npg):
        hi, lo = _split_bf16(lw_ref[0, p])
        both = _dot(tri, jnp.concatenate([hi, lo], axis=1))
        cums.append(both[:, :LANES] + both[:, LANES:])

    st = []
    for (p, c) in units:
        rows = slice(c * L, (c + 1) * L)
        r = r_ref[0, p, rows, :].astype(F32)
        k = k_ref[0, p, rows, :].astype(F32)
        v = v_ref[0, p, rows, :]
        kk = kk_ref[0, p, rows, :].astype(F32)
        a = a_ref[0, p, rows, :].astype(F32)
        lw = lw_ref[0, p, rows, :]
        cum = cums[p][rows, :]
        kkn = kk / jnp.maximum(jnp.sqrt(head_sum(kk * kk)), 1e-12)
        bvec = kkn * a
        c_last = cum[L - 1:L, :]
        at = -kkn * jnp.exp(cum - lw)
        rt = r * jnp.exp(cum)
        inv_w = jnp.exp(-cum)
        d_last = jnp.exp(c_last - cum)
        vb = v.astype(BF16)
        st.append(dict(
            p=p, rows=rows, at=at, rt=rt, vb=vb, w_last=jnp.exp(c_last),
            lhs=jnp.concatenate([split_heads(at), split_heads(rt)], axis=0).astype(BF16),
            rhs=jnp.concatenate([split_heads(bvec * inv_w), split_heads(k * inv_w)],
                                axis=0).astype(BF16),
            dk=jnp.concatenate([bvec * d_last, k * d_last], axis=0).astype(BF16),
            vs=jnp.concatenate([vb, vb], axis=0),
        ))

    for u in st:
        gram = _dot_nt(u["lhs"], u["rhs"])
        u["pw"] = jnp.where(strict, gram[:2 * L, :2 * L], 0.0)
        u["a_ak"] = jnp.where(strict, gram[:2 * L, 2 * L:], 0.0).astype(BF16)
        u["rbk"] = jnp.where(jnp.concatenate([incl, incl], axis=1), gram[2 * L:, :], 0.0).astype(BF16)
        u["tinv"] = eye2 + u["pw"]
    for u in st:
        u["akv"] = _dot(u["a_ak"], u["vs"])

    n_sq = 1
    while (1 << n_sq) < L:
        for u in st:
            pwb = u["pw"].astype(BF16)
            u["pw"] = _dot(pwb, pwb)
        for u in st:
            u["tinv"] = u["tinv"] + _dot(u["tinv"].astype(BF16), u["pw"].astype(BF16))
        n_sq += 1

    for u in st:
        xin = jnp.concatenate([split_heads(u["at"]), u["akv"]], axis=1).astype(BF16)
        u["z"] = _dot(u["tinv"].astype(BF16), xin)
    for u in st:
        low = jnp.concatenate([zeros_2l, u["vs"]], axis=1)
        w = _dot(u["rbk"], jnp.concatenate([u["z"].astype(BF16), low], axis=0))
        u["rbar"] = (u["rt"] + fold_heads(w[:, :LANES])).astype(BF16)
        u["y0"] = pick_heads(w[:, LANES:])
    for u in st:
        z = u["z"]
        top = jnp.concatenate([fold_heads(z[:, :LANES]), pick_heads(z[:, LANES:])], axis=1)
        bot = jnp.concatenate([zeros_l, u["vb"]], axis=1)
        ms = _dot_tn(jnp.concatenate([top.astype(BF16), bot], axis=0), u["dk"])
        u["m_lr"] = jnp.where(blockdiag, ms[:LANES], 0.0).astype(BF16)
        u["s0"] = jnp.where(blockdiag, ms[LANES:], 0.0)

    states = [s_ref[p] for p in range(npg)]
    for c in range(nch):
        for p in range(npg):
            u = st[p * nch + c]
            s_old = states[p]
            sb = s_old.astype(BF16)
            u["y"] = _dot_nt(u["rbar"], sb) + u["y0"]
            states[p] = s_old * u["w_last"] + _dot(sb, u["m_lr"]) + u["s0"]
    for p in range(npg):
        s_ref[p] = states[p]

    for u in st:
        p, rows = u["p"], u["rows"]
        lanes = slice(p * LANES, (p + 1) * LANES)
        y = u["y"]
        mu = head_sum(y) * (1.0 / RW_HEAD)
        yc = y - mu
        var = head_sum(yc * yc) * (1.0 / RW_HEAD)
        yn = yc * lax.rsqrt(var + GN_EPS) * vec_ref[1:2, lanes] + vec_ref[2:3, lanes]
        r = r_ref[0, p, rows, :].astype(F32)
        k = k_ref[0, p, rows, :].astype(F32)
        bonus = head_sum(r * k * vec_ref[0:1, lanes]) * v_ref[0, p, rows, :].astype(F32)
        o_ref[0, rows, lanes] = ((yn + bonus) * g_ref[0, p, rows, :].astype(F32)).astype(o_ref.dtype)


def _rwkv_core(proj, vecs, b, s, d, tt, npg):
    npair = d // LANES
    in_spec = pl.BlockSpec((1, npg, tt, LANES), lambda i, p, j: (i, p, j, 0))
    return pl.pallas_call(
        _rwkv_core_kernel,
        grid=(b, npair // npg, s // tt),
        in_specs=[in_spec] * 7 + [pl.BlockSpec((8, npg * LANES), lambda i, p, j: (0, p))],
        out_specs=pl.BlockSpec((1, tt, npg * LANES), lambda i, p, j: (i, j, p)),
        out_shape=jax.ShapeDtypeStruct((b, s, d), BF16),
        scratch_shapes=[pltpu.VMEM((npg, LANES, LANES), F32)],
        compiler_params=_params(("parallel", "parallel", "arbitrary")),
        name="rwkv_core",
    )(*proj, vecs)


def _post_ffn_kernel(x_ref, y_ref, mod_ref, fw_ref, wo_ref, wg_ref, wu_ref, wd_ref, o_ref,
                     *, ff_chunk, final):
    g1 = mod_ref[0, 2:3, :]
    sh2 = mod_ref[0, 3:4, :]
    sc2 = mod_ref[0, 4:5, :]
    g2 = mod_ref[0, 5:6, :]
    x1 = x_ref[0] + g1 * _dot(y_ref[0], wo_ref[...])
    h = _rms_mod(x1, sc2, sh2).astype(BF16)
    d_ff = wg_ref.shape[1]
    acc = jnp.zeros_like(x1)
    for j in range(d_ff // ff_chunk):
        cols = slice(j * ff_chunk, (j + 1) * ff_chunk)
        gate = _dot(h, wg_ref[:, cols])
        up = _dot(h, wu_ref[:, cols])
        act = (gate * _sigmoid(gate) * up).astype(BF16)
        acc = acc + _dot(act, wd_ref[cols, :])
    x2 = x1 + g2 * acc
    if final:
        inv = lax.rsqrt(jnp.mean(x2 * x2, axis=-1, keepdims=True) + EPS)
        x2 = x2 * inv * fw_ref[0:1, :]
    o_ref[0] = x2


def _post_ffn(x, y, mod, final_w, w_out, w_g, w_u, w_d, tm, final):
    b, s, d = x.shape
    d_ff = w_g.shape[1]
    ff_chunk = d_ff // 2 if (d_ff // 2) % LANES == 0 else d_ff
    kern = functools.partial(_post_ffn_kernel, ff_chunk=ff_chunk, final=final)
    tok = lambda i, j: (i, j, 0)
    return pl.pallas_call(
        kern,
        grid=(b, s // tm),
        in_specs=[
            pl.BlockSpec((1, tm, d), tok),
            pl.BlockSpec((1, tm, d), tok),
            pl.BlockSpec((1, 8, d), lambda i, j: (i, 0, 0)),
            _const_spec(final_w.shape),
            _const_spec(w_out.shape),
            _const_spec(w_g.shape),
            _const_spec(w_u.shape),
            _const_spec(w_d.shape),
        ],
        out_specs=pl.BlockSpec((1, tm, d), tok),
        out_shape=jax.ShapeDtypeStruct((b, s, d), F32),
        compiler_params=_params(("parallel", "parallel")),
        name="post_ffn_final" if final else "post_ffn",
    )(x, y, mod, final_w, w_out, w_g, w_u, w_d)


def _softcap(z):
    return GATE_CAP * jnp.tanh(z * (1.0 / GATE_CAP))


def _mlstm_proj_kernel(x_ref, xp_ref, mod_ref, cv_ref, gb_ref, wqk_ref, wv_ref,
                       wo_ref, wgi_ref, wgf_ref,
                       q_out, k_out, v_out, og_out, gi_out, gf_out, pbuf):
    tm = x_ref.shape[1]
    t = pl.program_id(1)
    sh = mod_ref[0, 0:1, :]
    sc = mod_ref[0, 1:2, :]
    h = _rms_mod(x_ref[0], sc, sh).astype(BF16)
    hp = _rms_mod(xp_ref[0], sc, sh).astype(BF16)

    pqk = _dot(h, wqk_ref[...])
    ppv = jnp.where(t > 0, _dot(hp, wqk_ref[...]), 0.0)
    pbuf[0:8, :] = ppv
    pbuf[8:, :] = pqk
    conv = cv_ref[CONV_W:CONV_W + 1, :] + pqk * cv_ref[CONV_W - 1:CONV_W, :]
    for j in range(1, CONV_W):
        conv = conv + pbuf[pl.ds(8 - j, tm), :] * cv_ref[CONV_W - 1 - j:CONV_W - j, :]
    qk = conv * _sigmoid(conv)
    half = qk.shape[1] // 2
    q_out[0] = (qk[:, :half] * (ML_QK ** -0.5)).astype(q_out.dtype)
    k_out[0] = qk[:, half:].astype(k_out.dtype)
    v_out[0] = _dot(h, wv_ref[...]).astype(v_out.dtype)
    og_out[0] = _sigmoid(_dot(h, wo_ref[...])).astype(og_out.dtype)

    lane = lax.broadcasted_iota(jnp.int32, gi_out.shape[1:], 1)
    zi = _dot(h, wgi_ref[...]) + gb_ref[0:1, :]
    zf = _dot(h, wgf_ref[...]) + gb_ref[1:2, :]
    gi_out[0] = jnp.where(lane < ML_HEADS, _softcap(zi), 0.0)
    gf_out[0] = jnp.where(lane < ML_HEADS, -_softplus(-_softcap(zf)), 0.0)


def _mlstm_proj(x, mod, cv, gb, weights, tm):
    b, s, d = x.shape
    w_qk, w_v, w_o, w_gi, w_gf = weights
    nqk = w_qk.shape[1] // 2
    ng = w_gi.shape[1]
    blk8 = tm // 8
    tok = lambda i, j: (i, j, 0)
    return pl.pallas_call(
        _mlstm_proj_kernel,
        grid=(b, s // tm),
        in_specs=[
            pl.BlockSpec((1, tm, d), tok),
            pl.BlockSpec((1, 8, d), lambda i, j: (i, jnp.maximum(j * blk8 - 1, 0), 0)),
            pl.BlockSpec((1, 8, d), lambda i, j: (i, 0, 0)),
            _const_spec(cv.shape),
            _const_spec(gb.shape),
        ] + [_const_spec(w.shape) for w in weights],
        out_specs=[
            pl.BlockSpec((1, tm, nqk), tok),
            pl.BlockSpec((1, tm, nqk), tok),
            pl.BlockSpec((1, tm, w_v.shape[1]), tok),
            pl.BlockSpec((1, tm, w_o.shape[1]), tok),
            pl.BlockSpec((1, tm, ng), tok),
            pl.BlockSpec((1, tm, ng), tok),
        ],
        out_shape=[
            jax.ShapeDtypeStruct((b, s, nqk), BF16),
            jax.ShapeDtypeStruct((b, s, nqk), BF16),
            jax.ShapeDtypeStruct((b, s, w_v.shape[1]), BF16),
            jax.ShapeDtypeStruct((b, s, w_o.shape[1]), BF16),
            jax.ShapeDtypeStruct((b, s, ng), F32),
            jax.ShapeDtypeStruct((b, s, ng), F32),
        ],
        scratch_shapes=[pltpu.VMEM((tm + 8, w_qk.shape[1]), F32)],
        compiler_params=_params(("parallel", "arbitrary")),
        name="mlstm_proj",
    )(x, x, mod, cv, gb, *weights)


def _split3_bf16(z):
    hi = z.astype(BF16)
    r1 = z - hi.astype(F32)
    mid = r1.astype(BF16)
    lo = (r1 - mid.astype(F32)).astype(BF16)
    return jnp.concatenate([hi, mid, lo], axis=1)


def _mlstm_core_kernel(q_ref, k_ref, v_ref, og_ref, gi_ref, gf_ref, hn_ref, e64_ref, e128_ref,
                       o_ref, c_ref, m_ref):
    tt = q_ref.shape[1]
    L = CHUNK
    nch = tt // L
    npair = ML_HEADS // 2
    t = pl.program_id(1)

    @pl.when(t == 0)
    def _():
        c_ref[...] = jnp.zeros_like(c_ref)
        m_ref[...] = jnp.zeros_like(m_ref)

    lane = lax.broadcasted_iota(jnp.int32, (L, LANES), 1)
    rowl = lax.broadcasted_iota(jnp.int32, (L, LANES), 0)
    m0 = lane < ML_QK
    src = lane % L
    rowt = lax.broadcasted_iota(jnp.int32, (tt, tt), 0)
    colt = lax.broadcasted_iota(jnp.int32, (tt, tt), 1)
    tri = (((rowt // L) == (colt // L)) & (rowt >= colt)).astype(BF16)
    row_in_chunk = lax.broadcasted_iota(jnp.int32, (tt, LANES), 0) % L
    srow = lax.broadcasted_iota(jnp.int32, (LANES, 4 * LANES), 0)
    scol = lax.broadcasted_iota(jnp.int32, (LANES, 4 * LANES), 1)
    state_block = (srow // ML_QK) == (scol // (2 * LANES))
    row_first = lax.broadcasted_iota(jnp.int32, (LANES, 1), 0) < ML_QK
    ones = jnp.ones((L, LANES), BF16)
    zer = jnp.zeros((L, LANES), BF16)

    gi = gi_ref[0]
    b3 = _dot(tri, _split3_bf16(gf_ref[0]))
    bcum = b3[:, :LANES] + b3[:, LANES:2 * LANES] + b3[:, 2 * LANES:]
    z = gi - bcum
    cmax = z
    shift = 1
    while shift < L:
        cmax = jnp.where(row_in_chunk >= shift, jnp.maximum(cmax, pltpu.roll(cmax, shift, 0)), cmax)
        shift *= 2
    m_intra = bcum + cmax

    m_prev = m_ref[0:1, :]
    mp_rows, mn_rows, gt_rows, s_a = [], [], [], []
    for c in range(nch):
        last = c * L + L - 1
        g_tot = bcum[last:last + 1, :]
        m_loc = g_tot + cmax[last:last + 1, :]
        m_new = jnp.maximum(g_tot + m_prev, m_loc)
        mp_rows.append(jnp.broadcast_to(m_prev, (L, LANES)))
        mn_rows.append(jnp.broadcast_to(m_new, (L, LANES)))
        gt_rows.append(jnp.broadcast_to(g_tot, (L, LANES)))
        s_a.append(jnp.exp(g_tot + m_prev - m_new))
        m_prev = m_new
    m_ref[0:1, :] = m_prev
    inter = bcum + jnp.concatenate(mp_rows, axis=0)
    m_t = jnp.maximum(inter, m_intra)
    w_inter = jnp.exp(inter - m_t)
    kws = jnp.exp(jnp.concatenate(gt_rows, axis=0) + z - jnp.concatenate(mn_rows, axis=0))

    a_exp = _dot(_split3_bf16(bcum - m_t), e64_ref[...])
    z_exp = _dot(_split3_bf16(z), e64_ref[...])
    wi_exp = _dot(_split3_bf16(w_inter), e64_ref[...])
    kw_exp = _dot(_split3_bf16(kws), e64_ref[...])
    em_exp = _dot(_split3_bf16(jnp.exp(-m_t)), e128_ref[...])

    un = []
    for c in range(nch):
        rows = slice(c * L, (c + 1) * L)
        for p in range(npair):
            lanes = slice(p * LANES, (p + 1) * LANES)
            qp = q_ref[0, rows, lanes]
            kp = k_ref[0, rows, lanes]
            v0 = v_ref[0, rows, 2 * p * ML_V:(2 * p + 1) * ML_V]
            v1 = v_ref[0, rows, (2 * p + 1) * ML_V:(2 * p + 2) * ML_V]
            z_src = jnp.sum(jnp.where(rowl == src, z_exp[rows, lanes], 0.0), axis=0, keepdims=True)
            un.append(dict(
                c=c, p=p, rows=rows, qp=qp,
                k2=jnp.concatenate([jnp.where(m0, kp, 0), jnp.where(m0, 0, kp)], axis=0),
                expd=jnp.where(rowl >= src, jnp.exp(a_exp[rows, lanes] + z_src), 0.0),
                qs=(qp * wi_exp[rows, lanes]).astype(BF16),
                kw=(kp * kw_exp[rows, lanes]).astype(BF16),
                vaug=jnp.concatenate([v0, ones, v1, ones], axis=1),
                rhs=jnp.concatenate([jnp.concatenate([v0, ones, zer, zer], axis=1),
                                     jnp.concatenate([zer, zer, v1, ones], axis=1)], axis=0),
            ))

    for u in un:
        scores = _dot_nt(u["qp"], u["k2"])
        u["w_intra"] = (u["expd"] * scores).astype(BF16)
    for u in un:
        u["c_loc"] = jnp.where(state_block, _dot_tn(u["kw"], u["vaug"]), 0.0)

    states = [c_ref[p] for p in range(npair)]
    for u in un:
        c, p = u["c"], u["p"]
        u["c_old"] = states[p].astype(BF16)
        sa = jnp.where(row_first, s_a[c][:, 2 * p:2 * p + 1], s_a[c][:, 2 * p + 1:2 * p + 2])
        states[p] = sa * states[p] + u["c_loc"]
    for p in range(npair):
        c_ref[p] = states[p]

    for u in un:
        u["nd"] = _dot(jnp.concatenate([u["qs"], u["w_intra"]], axis=1),
                       jnp.concatenate([u["c_old"], u["rhs"]], axis=0))

    for u in un:
        rows = u["rows"]
        for hh in range(2):
            head = 2 * u["p"] + hh
            cols = slice(head * ML_V, (head + 1) * ML_V)
            num = u["nd"][:, 2 * hh * LANES:(2 * hh + 1) * LANES]
            den = u["nd"][:, (2 * hh + 1) * LANES:(2 * hh + 2) * LANES]
            den = jnp.maximum(jnp.abs(den), em_exp[rows, cols])
            hv = num / den
            hv = hv * lax.rsqrt(jnp.mean(hv * hv, axis=-1, keepdims=True) + EPS)
            hv = hv * hn_ref[0:1, cols]
            o_ref[0, rows, cols] = (hv * og_ref[0, rows, cols]).astype(o_ref.dtype)


def _expand_matrix(width):
    rows = jnp.arange(LANES)[:, None]
    cols = jnp.arange(ML_HEADS * width)[None, :]
    e = (rows == cols // width).astype(BF16)
    return jnp.concatenate([e, e, e], axis=0)


def _mlstm_core(q, k, v, og, gi, gf, hn_w, tt):
    b, s, dv = v.shape
    nqk = q.shape[2]
    ng = gi.shape[2]
    e64 = _expand_matrix(ML_QK)
    e128 = _expand_matrix(ML_V)
    tok = lambda i, j: (i, j, 0)
    return pl.pallas_call(
        _mlstm_core_kernel,
        grid=(b, s // tt),
        in_specs=[
            pl.BlockSpec((1, tt, nqk), tok),
            pl.BlockSpec((1, tt, nqk), tok),
            pl.BlockSpec((1, tt, dv), tok),
            pl.BlockSpec((1, tt, dv), tok),
            pl.BlockSpec((1, tt, ng), tok),
            pl.BlockSpec((1, tt, ng), tok),
            _const_spec(hn_w.shape),
            _const_spec(e64.shape),
            _const_spec(e128.shape),
        ],
        out_specs=pl.BlockSpec((1, tt, dv), tok),
        out_shape=jax.ShapeDtypeStruct((b, s, dv), BF16),
        scratch_shapes=[
            pltpu.VMEM((ML_HEADS // 2, LANES, 4 * LANES), F32),
            pltpu.VMEM((8, LANES), F32),
        ],
        compiler_params=_params(("parallel", "arbitrary")),
        name="mlstm_core",
    )(q, k, v, og, gi, gf, hn_w, e64, e128)


def _pad_rows(a, rows=8):
    return jnp.pad(a, ((0, rows - a.shape[0]), (0, 0)))


def kernel(x, c, ada_w, ada_b, rw_mix, rw_w_in, rw_w0, rw_w2, rw_a0, rw_a2, rw_g2, rw_k_k, rw_k_a, rw_r_k, rw_gn_w, rw_gn_b, rw_w_out, ml_w_in, ml_conv_w, ml_conv_b, ml_b_i, ml_b_f, ml_hn_w, ml_w_out, ffn_w_gu, ffn_w_down, final_w):
    b, s, d = x.shape
    d_ff = ffn_w_down.shape[1]
    tm = 256
    bf = lambda w: w.astype(BF16)
    final_w2 = final_w.reshape(1, d)

    mod = _adaln_mod(c, ada_w, ada_b)

    w_in = rw_w_in[0]
    o1 = d
    o2 = o1 + LORA_W
    o3 = o2 + d
    o4 = o3 + d
    o5 = o4 + LORA_A
    rw_weights = [bf(w_in[:, :o1]), bf(w_in[:, o2:o3]), bf(w_in[:, o3:o4]),
                  bf(w_in[:, o1:o2]), bf(w_in[:, o4:o5]), bf(w_in[:, o5:]),
                  bf(rw_w2[0]), bf(rw_a2[0]), bf(rw_g2[0])]
    mix = _pad_rows(rw_mix[0])
    vecs = _pad_rows(jnp.stack([rw_w0[0], rw_a0[0], rw_k_k[0], rw_k_a[0]]))
    proj = _rwkv_proj(x, mod[0], mix, vecs, rw_weights, tm)
    core_vecs = _pad_rows(jnp.stack([rw_r_k[0].reshape(d), rw_gn_w[0], rw_gn_b[0]]))
    y = _rwkv_core(proj, core_vecs, b, s, d, 256, 2)
    x = _post_ffn(x, y, mod[0], final_w2, bf(rw_w_out[0]), bf(ffn_w_gu[0][:, :d_ff]),
                  bf(ffn_w_gu[0][:, d_ff:]), bf(ffn_w_down[0]), tm, False)

    w_in = ml_w_in[0]
    nqk = 2 * ML_HEADS * ML_QK
    nv = ML_HEADS * ML_V
    pad_lanes = lambda w: jnp.pad(w, ((0, 0), (0, LANES - w.shape[1])))
    o_g = nqk + 2 * nv
    ml_weights = [bf(w_in[:, :nqk]), bf(w_in[:, nqk:nqk + nv]), bf(w_in[:, nqk + nv:o_g]),
                  bf(pad_lanes(w_in[:, o_g:o_g + ML_HEADS])), bf(pad_lanes(w_in[:, o_g + ML_HEADS:]))]
    cv = _pad_rows(jnp.concatenate([ml_conv_w[0], ml_conv_b[0][None, :]], axis=0))
    gb = _pad_rows(pad_lanes(jnp.stack([ml_b_i[0], ml_b_f[0]])))
    q, k, v, og, gi, gf = _mlstm_proj(x, mod[1], cv, gb, ml_weights, tm)
    y = _mlstm_core(q, k, v, og, gi, gf, ml_hn_w[0].reshape(1, nv), 256)
    x = _post_ffn(x, y, mod[1], final_w2, bf(ml_w_out[0]), bf(ffn_w_gu[1][:, :d_ff]),
                  bf(ffn_w_gu[1][:, d_ff:]), bf(ffn_w_down[1]), tm, True)
    return x
```

```python
import functools

import jax
import jax.numpy as jnp
from jax import lax
from jax.experimental import pallas as pl
from jax.experimental.pallas import tpu as pltpu

F32 = jnp.float32
BF16 = jnp.bfloat16

EPS = 1e-6
GN_EPS = 64e-5
GATE_CAP = 15.0
DECAY_SCALE = 0.6065306597126334
CHUNK = 64
LANES = 128
RW_HEAD = 64
LORA_W = 64
LORA_A = 64
LORA_G = 160
ML_HEADS = 8
ML_QK = 64
ML_V = 128
CONV_W = 4
COL_BLOCK = 256
VMEM_LIMIT = 56 * 1024 * 1024
TM_DENSE = 512
TT_RWKV = 256
PAIRS_RWKV = 8
TT_MLSTM = 256

NT_DIMS = (((1,), (1,)), ((), ()))
TN_DIMS = (((0,), (0,)), ((), ()))


def _dot(a, b):
    return jnp.dot(a, b, preferred_element_type=F32)


def _dot_nt(a, b):
    return lax.dot_general(a, b, NT_DIMS, preferred_element_type=F32)


def _dot_tn(a, b):
    return lax.dot_general(a, b, TN_DIMS, preferred_element_type=F32)


def _sigmoid(z):
    return 0.5 * jnp.tanh(0.5 * z) + 0.5


def _softplus(z):
    return jnp.maximum(z, 0.0) + jnp.log(1.0 + jnp.exp(-jnp.abs(z)))


def _rms_mod(xv, sc, sh):
    inv = lax.rsqrt(jnp.mean(xv * xv, axis=-1, keepdims=True) + EPS)
    return xv * inv * (1.0 + sc) + sh


def _split_bf16(z):
    hi = z.astype(BF16)
    lo = (z - hi.astype(F32)).astype(BF16)
    return hi, lo


def _params(sem, flags=None):
    return pltpu.CompilerParams(dimension_semantics=sem, vmem_limit_bytes=VMEM_LIMIT, flags=flags)


def _const_spec(shape):
    nd = len(shape)
    return pl.BlockSpec(shape, lambda *_: (0,) * nd, pipeline_mode=pl.Buffered(1))


def _mod_kernel(c_ref, w_ref, b_ref, o_ref):
    c = c_ref[...]
    ca = (c * _sigmoid(c)).astype(BF16)
    o_ref[0] = _dot(ca, w_ref[0].astype(BF16)) + b_ref[0]


def _adaln_mod(c, ada_w, ada_b):
    depth, d, d6 = ada_w.shape
    b = c.shape[0]
    bp = 8
    c_pad = jnp.zeros((bp, d), F32).at[:b].set(c)
    nblk = d6 // d
    out = pl.pallas_call(
        _mod_kernel,
        grid=(depth, nblk),
        in_specs=[
            pl.BlockSpec((bp, d), lambda i, j: (0, 0)),
            pl.BlockSpec((1, d, d), lambda i, j: (i, 0, j)),
            pl.BlockSpec((1, 1, d), lambda i, j: (i, 0, j)),
        ],
        out_specs=pl.BlockSpec((1, bp, d), lambda i, j: (i, 0, j)),
        out_shape=jax.ShapeDtypeStruct((depth, bp, d6), F32),
        compiler_params=_params(("arbitrary", "arbitrary")),
        name="adaln_mod",
    )(c_pad, ada_w, ada_b.reshape(depth, 1, d6))
    mod = out[:, :b].reshape(depth, b, nblk, d)
    return jnp.pad(mod, ((0, 0), (0, 0), (0, 8 - nblk), (0, 0)))


def _rwkv_proj_kernel(x_ref, xp_ref, mod_ref, mix_ref, vec_ref, wr_ref, wk_ref, wv_ref,
                      wwl_ref, wal_ref, wgl_ref, w2_ref, a2_ref, g2_ref,
                      r_out, k_out, v_out, kk_out, a_out, lw_out, g_out, hbuf, xbuf):
    tm = x_ref.shape[1]
    t = pl.program_id(1)
    sh = mod_ref[0, 0:1, :]
    sc = mod_ref[0, 1:2, :]
    hp = _rms_mod(xp_ref[0], sc, sh)
    hbuf[0:8, :] = jnp.where(t > 0, hp, 0.0)
    w0 = vec_ref[0:1, :]
    a0 = vec_ref[1:2, :]
    k_k = vec_ref[2:3, :]
    k_a = vec_ref[3:4, :]

    h = _rms_mod(x_ref[0], sc, sh)
    hbuf[8:, :] = h
    xbuf[...] = hbuf[pl.ds(7, tm), :] - h

    def mixed(i):
        return (hbuf[8:, :] + xbuf[...] * mix_ref[i:i + 1, :]).astype(BF16)

    def put(out, val, cb):
        for q in range(COL_BLOCK // LANES):
            out[0, cb * (COL_BLOCK // LANES) + q] = val[:, q * LANES:(q + 1) * LANES].astype(out.dtype)

    ncb = r_out.shape[1] * LANES // COL_BLOCK
    col = lambda cb: slice(cb * COL_BLOCK, (cb + 1) * COL_BLOCK)
    lhs = mixed(0)
    for cb in range(ncb):
        put(r_out, _dot(lhs, wr_ref[:, col(cb)]), cb)
    lhs = mixed(3)
    for cb in range(ncb):
        put(v_out, _dot(lhs, wv_ref[:, col(cb)]), cb)
    twl = jnp.tanh(_dot(mixed(1), wwl_ref[...])).astype(BF16)
    alb = _dot(mixed(4), wal_ref[...]).astype(BF16)
    sgl = _sigmoid(_dot(mixed(5), wgl_ref[...])).astype(BF16)
    a_blocks = []
    for cb in range(ncb):
        c = col(cb)
        put(lw_out, -DECAY_SCALE * _sigmoid(w0[:, c] + _dot(twl, w2_ref[:, c])), cb)
        a = _sigmoid(a0[:, c] + _dot(alb, a2_ref[:, c]))
        put(a_out, a, cb)
        a_blocks.append(a)
        put(g_out, _dot(sgl, g2_ref[:, c]), cb)
    lhs = mixed(2)
    for cb in range(ncb):
        c = col(cb)
        k = _dot(lhs, wk_ref[:, c])
        put(kk_out, k * k_k[:, c], cb)
        put(k_out, k * (1.0 + (a_blocks[cb] - 1.0) * k_a[:, c]), cb)


def _rwkv_proj(x, mod, mix, vecs, weights, tm):
    b, s, d = x.shape
    npair = d // LANES
    nt = s // tm
    blk8 = tm // 8
    pair_shape = lambda dt: jax.ShapeDtypeStruct((b, npair, s, LANES), dt)
    pair_spec = pl.BlockSpec((1, npair, tm, LANES), lambda i, j: (i, 0, j, 0))
    return pl.pallas_call(
        _rwkv_proj_kernel,
        grid=(b, nt),
        in_specs=[
            pl.BlockSpec((1, tm, d), lambda i, j: (i, j, 0)),
            pl.BlockSpec((1, 8, d), lambda i, j: (i, jnp.maximum(j * blk8 - 1, 0), 0)),
            pl.BlockSpec((1, 8, d), lambda i, j: (i, 0, 0)),
            _const_spec(mix.shape),
            _const_spec(vecs.shape),
        ] + [_const_spec(w.shape) for w in weights],
        out_specs=[pair_spec] * 7,
        out_shape=[pair_shape(BF16)] * 5 + [pair_shape(F32), pair_shape(BF16)],
        scratch_shapes=[pltpu.VMEM((tm + 8, d), F32), pltpu.VMEM((tm, d), F32)],
        compiler_params=_params(("parallel", "arbitrary")),
        name="rwkv_proj",
    )(x, x, mod, mix, vecs, *weights)


def _rwkv_core_kernel(r_ref, k_ref, v_ref, kk_ref, a_ref, lw_ref, g_ref, vec_ref, vec_prev_ref,
                      o_ref, s_ref, rbar_s, y0_s, mlrt_s, s0_s, wl_s, g_s, bg_s, *, tiles_per_seq):
    npg = r_ref.shape[1]
    tt = r_ref.shape[2]
    L = CHUNK
    nch = tt // L
    nunit = npg * nch
    step = pl.program_id(0)

    @pl.when(step == 0)
    def _():
        for ref in (s_ref, rbar_s, y0_s, mlrt_s, s0_s, wl_s, g_s, bg_s):
            ref[...] = jnp.zeros_like(ref)

    fresh = (jnp.maximum(step - 1, 0) % tiles_per_seq) == 0

    lane = lax.broadcasted_iota(jnp.int32, (L, LANES), 1)
    m0 = lane < RW_HEAD
    rowl = lax.broadcasted_iota(jnp.int32, (L, LANES), 0)
    strict = rowl > (lane % L)
    incl = rowl >= (lane % L)
    eye_cat = (rowl == (lane % L)).astype(F32)
    row2 = lax.broadcasted_iota(jnp.int32, (LANES, LANES), 0)
    col2 = lax.broadcasted_iota(jnp.int32, (LANES, LANES), 1)
    blockdiag = (row2 // RW_HEAD) == (col2 // RW_HEAD)
    rowt = lax.broadcasted_iota(jnp.int32, (tt, tt), 0)
    colt = lax.broadcasted_iota(jnp.int32, (tt, tt), 1)
    tri = (((rowt // L) == (colt // L)) & (rowt >= colt)).astype(BF16)
    zeros_l = jnp.zeros((L, LANES), BF16)

    def head_sum(z):
        lo = jnp.sum(jnp.where(m0, z, 0.0), axis=-1, keepdims=True)
        hi = jnp.sum(jnp.where(m0, 0.0, z), axis=-1, keepdims=True)
        return jnp.where(m0, lo, hi)

    def split_heads(z):
        zb = z.astype(BF16)
        zero = jnp.zeros_like(zb)
        return jnp.concatenate([jnp.where(m0, zb, zero), jnp.where(m0, zero, zb)], axis=0)

    units = [(p, c) for p in range(npg) for c in range(nch)]

    cums = []
    for p in range(npg):
        hi, lo = _split_bf16(lw_ref[0, p])
        both = _dot(tri, jnp.concatenate([hi, lo], axis=1))
        cums.append(both[:, :LANES] + both[:, LANES:])

    pu = []
    for (p, c) in units:
        rows = slice(c * L, (c + 1) * L)
        r = r_ref[0, p, rows, :].astype(F32)
        k = k_ref[0, p, rows, :].astype(F32)
        v = v_ref[0, p, rows, :]
        kk = kk_ref[0, p, rows, :].astype(F32)
        a = a_ref[0, p, rows, :].astype(F32)
        g = g_ref[0, p, rows, :]
        lw = lw_ref[0, p, rows, :]
        cum = cums[p][rows, :]
        kkn = kk * jnp.minimum(lax.rsqrt(head_sum(kk * kk)), 1e12)
        bvec = kkn * a
        c_last = cum[L - 1:L, :]
        at = -kkn * jnp.exp(cum - lw)
        rt = r * jnp.exp(cum)
        inv_w = jnp.exp(-cum)
        d_last = jnp.exp(c_last - cum)
        lanes = slice(p * LANES, (p + 1) * LANES)
        bonus = head_sum(r * k * vec_ref[0:1, lanes]) * v.astype(F32)
        pu.append(dict(
            rt=rt, v=v, w_last=jnp.exp(c_last), g=g, bg=bonus * g.astype(F32),
            lhs=jnp.concatenate([at, rt], axis=0).astype(BF16),
            rhs=jnp.concatenate([split_heads(bvec * inv_w), split_heads(k * inv_w)], axis=0),
            dk=jnp.concatenate([bvec * d_last, k * d_last], axis=0).astype(BF16),
            at_s=split_heads(at), v_s=split_heads(v),
        ))

    states = [jnp.where(fresh, 0.0, s_ref[p]) for p in range(npg)]
    ys = {}

    def scan_level(c):
        for p in range(npg):
            j = p * nch + c
            s_old = states[p]
            sb = s_old.astype(BF16)
            ys[j] = _dot_nt(rbar_s[j], sb) + y0_s[j]
            states[p] = s_old * wl_s[j] + _dot_nt(sb, mlrt_s[j]) + s0_s[j]

    per_level = -(-nunit // nch)
    for j, u in enumerate(pu):
        if j % per_level == 0:
            scan_level(j // per_level)
        gram = _dot_nt(u["lhs"], u["rhs"])
        u["pw"] = jnp.where(strict, gram[:L, :LANES], 0.0)
        u["r_b"] = jnp.where(incl, gram[L:, :LANES], 0.0).astype(BF16)
        u["ak_rk"] = jnp.concatenate([jnp.where(strict, gram[:L, LANES:], 0.0),
                                      jnp.where(incl, gram[L:, LANES:], 0.0)], axis=0).astype(BF16)
        u["tinv"] = eye_cat + u["pw"]
        u["pw_s"] = split_heads(u["pw"])
    for p in range(npg):
        s_ref[p] = states[p]

    for j in range(nunit):
        p, c = divmod(j, nch)
        lanes = slice(p * LANES, (p + 1) * LANES)
        y = ys[j]
        mu = head_sum(y) * (1.0 / RW_HEAD)
        yc = y - mu
        var = head_sum(yc * yc) * (1.0 / RW_HEAD)
        yn = yc * lax.rsqrt(var + GN_EPS) * vec_prev_ref[1:2, lanes] + vec_prev_ref[2:3, lanes]
        o_ref[0, c * L:(c + 1) * L, lanes] = (yn * g_s[j].astype(F32) + bg_s[j]).astype(o_ref.dtype)

    for u in pu:
        both = _dot(u["ak_rk"], u["v_s"])
        u["akv"] = both[:L]
        u["rkv"] = both[L:]

    for u in pu:
        u["pw"] = _dot(u["pw"].astype(BF16), u["pw_s"])
        u["pw_s"] = split_heads(u["pw"])
    n_levels = L.bit_length() - 1
    for level in range(1, n_levels):
        last = level == n_levels - 1
        for u in pu:
            if last:
                u["tinv"] = u["tinv"] + _dot(u["tinv"].astype(BF16), u["pw_s"])
            else:
                lhs = jnp.concatenate([u["tinv"], u["pw"]], axis=0).astype(BF16)
                both = _dot(lhs, u["pw_s"])
                u["tinv"] = u["tinv"] + both[:L]
                u["pw"] = both[L:]
                u["pw_s"] = split_heads(u["pw"])

    for u in pu:
        xin = jnp.concatenate([u["at_s"], split_heads(u["akv"])], axis=1)
        u["z"] = _dot(u["tinv"].astype(BF16), xin)
    for u in pu:
        z = u["z"]
        rhs = jnp.concatenate([split_heads(z[:, :LANES]), split_heads(z[:, LANES:])], axis=1)
        u["w"] = _dot(u["r_b"], rhs)
    for u in pu:
        bot = jnp.concatenate([zeros_l, u["v"]], axis=1)
        u["ms"] = _dot_tn(u["dk"], jnp.concatenate([u["z"].astype(BF16), bot], axis=0))

    for j, u in enumerate(pu):
        rbar_s[j] = (u["rt"] + u["w"][:, :LANES]).astype(BF16)
        y0_s[j] = u["w"][:, LANES:] + u["rkv"]
        mlrt_s[j] = jnp.where(blockdiag, u["ms"][:, :LANES], 0.0).astype(BF16)
        s0_s[j] = jnp.where(blockdiag, u["ms"][:, LANES:].T, 0.0)
        wl_s[j] = u["w_last"]
        g_s[j] = u["g"]
        bg_s[j] = u["bg"]


def _rwkv_core(proj, vecs, b, s, d, tt, npg):
    npair = d // LANES
    ngrp = npair // npg
    nt = s // tt
    ntiles = b * ngrp * nt
    nunit = npg * (tt // CHUNK)

    def tile_coords(tile):
        return tile // (ngrp * nt), (tile // nt) % ngrp, tile % nt

    def in_map(i):
        bi, gi, ti = tile_coords(jnp.minimum(i, ntiles - 1))
        return (bi, gi, ti, 0)

    def out_map(i):
        bi, gi, ti = tile_coords(jnp.maximum(i - 1, 0))
        return (bi, ti, gi)

    in_spec = pl.BlockSpec((1, npg, tt, LANES), in_map)
    vec_spec = pl.BlockSpec((8, npg * LANES), lambda i: (0, tile_coords(jnp.minimum(i, ntiles - 1))[1]))
    vec_prev_spec = pl.BlockSpec((8, npg * LANES), lambda i: (0, tile_coords(jnp.maximum(i - 1, 0))[1]))
    L = CHUNK
    return pl.pallas_call(
        functools.partial(_rwkv_core_kernel, tiles_per_seq=nt),
        grid=(ntiles + 1,),
        in_specs=[in_spec] * 7 + [vec_spec, vec_prev_spec],
        out_specs=pl.BlockSpec((1, tt, npg * LANES), out_map),
        out_shape=jax.ShapeDtypeStruct((b, s, d), BF16),
        scratch_shapes=[
            pltpu.VMEM((npg, LANES, LANES), F32),
            pltpu.VMEM((nunit, L, LANES), BF16),
            pltpu.VMEM((nunit, L, LANES), F32),
            pltpu.VMEM((nunit, LANES, LANES), BF16),
            pltpu.VMEM((nunit, LANES, LANES), F32),
            pltpu.VMEM((nunit, 1, LANES), F32),
            pltpu.VMEM((nunit, L, LANES), BF16),
            pltpu.VMEM((nunit, L, LANES), F32),
        ],
        compiler_params=_params(("arbitrary",)),
        name="rwkv_core",
    )(*proj, vecs, vecs)


def _post_ffn_kernel(x_ref, y_ref, mod_ref, fw_ref, wo_ref, wgu_ref, wd_ref, o_ref, act_ref,
                     *, final):
    g1 = mod_ref[0, 2:3, :]
    sh2 = mod_ref[0, 3:4, :]
    sc2 = mod_ref[0, 4:5, :]
    g2 = mod_ref[0, 5:6, :]
    x1 = x_ref[0] + g1 * _dot(y_ref[0], wo_ref[...])
    h = _rms_mod(x1, sc2, sh2).astype(BF16)
    d_ff = act_ref.shape[1]
    for c0 in range(0, d_ff, COL_BLOCK):
        cols = slice(c0, c0 + COL_BLOCK)
        gate = _dot(h, wgu_ref[0, :, cols])
        up = _dot(h, wgu_ref[0, :, d_ff + c0:d_ff + c0 + COL_BLOCK])
        act_ref[:, cols] = (gate * _sigmoid(gate) * up).astype(BF16)
    x2 = x1 + g2 * _dot(act_ref[...], wd_ref[0])
    if final:
        inv = lax.rsqrt(jnp.mean(x2 * x2, axis=-1, keepdims=True) + EPS)
        x2 = x2 * inv * fw_ref[0:1, :]
    o_ref[0] = x2


def _post_ffn(x, y, mod, final_w, w_out, w_gu, w_d, layer, tm, final):
    b, s, d = x.shape
    d_ff = w_d.shape[1]
    assert d_ff % COL_BLOCK == 0
    kern = functools.partial(_post_ffn_kernel, final=final)
    tok = lambda i, j: (i, j, 0)
    layer_spec = lambda w: pl.BlockSpec((1,) + w.shape[1:], lambda i, j: (layer, 0, 0),
                                        pipeline_mode=pl.Buffered(1))
    return pl.pallas_call(
        kern,
        grid=(b, s // tm),
        in_specs=[
            pl.BlockSpec((1, tm, d), tok),
            pl.BlockSpec((1, tm, d), tok),
            pl.BlockSpec((1, 8, d), lambda i, j: (i, 0, 0)),
            _const_spec(final_w.shape),
            _const_spec(w_out.shape),
            layer_spec(w_gu),
            layer_spec(w_d),
        ],
        out_specs=pl.BlockSpec((1, tm, d), tok),
        out_shape=jax.ShapeDtypeStruct((b, s, d), F32),
        scratch_shapes=[pltpu.VMEM((tm, d_ff), BF16)],
        compiler_params=_params(("parallel", "parallel")),
        name="post_ffn_final" if final else "post_ffn",
    )(x, y, mod, final_w, w_out, w_gu, w_d)


def _softcap(z):
    return GATE_CAP * jnp.tanh(z * (1.0 / GATE_CAP))


def _mlstm_proj_kernel(x_ref, xp_ref, mod_ref, cv_ref, gb_ref, wqk_ref, wv_ref,
                       wo_ref, wgi_ref, wgf_ref,
                       q_out, k_out, v_out, og_out, gi_out, gf_out, pbuf):
    tm = x_ref.shape[1]
    t = pl.program_id(1)
    sh = mod_ref[0, 0:1, :]
    sc = mod_ref[0, 1:2, :]
    h = _rms_mod(x_ref[0], sc, sh).astype(BF16)
    hp = _rms_mod(xp_ref[0], sc, sh).astype(BF16)

    half = wqk_ref.shape[1] // 2
    for c0 in range(0, wqk_ref.shape[1], COL_BLOCK):
        c = slice(c0, c0 + COL_BLOCK)
        pqk = _dot(h, wqk_ref[:, c])
        pbuf[0:8, c] = jnp.where(t > 0, _dot(hp, wqk_ref[:, c]), 0.0)
        pbuf[8:, c] = pqk
        conv = cv_ref[CONV_W:CONV_W + 1, c] + pqk * cv_ref[CONV_W - 1:CONV_W, c]
        for j in range(1, CONV_W):
            conv = conv + pbuf[pl.ds(8 - j, tm), c] * cv_ref[CONV_W - 1 - j:CONV_W - j, c]
        qk = conv * _sigmoid(conv)
        if c0 < half:
            q_out[0, :, c] = (qk * (ML_QK ** -0.5)).astype(q_out.dtype)
        else:
            k_out[0, :, c0 - half:c0 - half + COL_BLOCK] = qk.astype(k_out.dtype)
    for c0 in range(0, wv_ref.shape[1], COL_BLOCK):
        c = slice(c0, c0 + COL_BLOCK)
        v_out[0, :, c] = _dot(h, wv_ref[:, c]).astype(v_out.dtype)
    for c0 in range(0, wo_ref.shape[1], COL_BLOCK):
        c = slice(c0, c0 + COL_BLOCK)
        og_out[0, :, c] = _sigmoid(_dot(h, wo_ref[:, c])).astype(og_out.dtype)

    lane = lax.broadcasted_iota(jnp.int32, gi_out.shape[1:], 1)
    zi = _dot(h, wgi_ref[...]) + gb_ref[0:1, :]
    zf = _dot(h, wgf_ref[...]) + gb_ref[1:2, :]
    gi_out[0] = jnp.where(lane < ML_HEADS, _softcap(zi), 0.0)
    gf_out[0] = jnp.where(lane < ML_HEADS, -_softplus(-_softcap(zf)), 0.0)


def _mlstm_proj(x, mod, cv, gb, weights, tm):
    b, s, d = x.shape
    w_qk, w_v, w_o, w_gi, w_gf = weights
    nqk = w_qk.shape[1] // 2
    ng = w_gi.shape[1]
    blk8 = tm // 8
    tok = lambda i, j: (i, j, 0)
    return pl.pallas_call(
        _mlstm_proj_kernel,
        grid=(b, s // tm),
        in_specs=[
            pl.BlockSpec((1, tm, d), tok),
            pl.BlockSpec((1, 8, d), lambda i, j: (i, jnp.maximum(j * blk8 - 1, 0), 0)),
            pl.BlockSpec((1, 8, d), lambda i, j: (i, 0, 0)),
            _const_spec(cv.shape),
            _const_spec(gb.shape),
        ] + [_const_spec(w.shape) for w in weights],
        out_specs=[
            pl.BlockSpec((1, tm, nqk), tok),
            pl.BlockSpec((1, tm, nqk), tok),
            pl.BlockSpec((1, tm, w_v.shape[1]), tok),
            pl.BlockSpec((1, tm, w_o.shape[1]), tok),
            pl.BlockSpec((1, tm, ng), tok),
            pl.BlockSpec((1, tm, ng), tok),
        ],
        out_shape=[
            jax.ShapeDtypeStruct((b, s, nqk), BF16),
            jax.ShapeDtypeStruct((b, s, nqk), BF16),
            jax.ShapeDtypeStruct((b, s, w_v.shape[1]), BF16),
            jax.ShapeDtypeStruct((b, s, w_o.shape[1]), BF16),
            jax.ShapeDtypeStruct((b, s, ng), F32),
            jax.ShapeDtypeStruct((b, s, ng), F32),
        ],
        scratch_shapes=[pltpu.VMEM((tm + 8, w_qk.shape[1]), F32)],
        compiler_params=_params(("parallel", "arbitrary")),
        name="mlstm_proj",
    )(x, x, mod, cv, gb, *weights)


def _split3_bf16(z):
    hi = z.astype(BF16)
    r1 = z - hi.astype(F32)
    mid = r1.astype(BF16)
    lo = (r1 - mid.astype(F32)).astype(BF16)
    return jnp.concatenate([hi, mid, lo], axis=1)


def _mlstm_core_kernel(q_ref, k_ref, v_ref, og_ref, gi_ref, gf_ref, hn_ref, e64_ref, e128_ref,
                       o_ref, c_ref, m_ref):
    tt = q_ref.shape[1]
    L = CHUNK
    nch = tt // L
    npair = ML_HEADS // 2
    t = pl.program_id(1)

    @pl.when(t == 0)
    def _():
        c_ref[...] = jnp.zeros_like(c_ref)
        m_ref[...] = jnp.zeros_like(m_ref)

    lane = lax.broadcasted_iota(jnp.int32, (L, LANES), 1)
    rowl = lax.broadcasted_iota(jnp.int32, (L, LANES), 0)
    m0 = lane < ML_QK
    src = lane % L
    rowt = lax.broadcasted_iota(jnp.int32, (tt, tt), 0)
    colt = lax.broadcasted_iota(jnp.int32, (tt, tt), 1)
    tri = (((rowt // L) == (colt // L)) & (rowt >= colt)).astype(BF16)
    row_in_chunk = lax.broadcasted_iota(jnp.int32, (tt, LANES), 0) % L
    srow = lax.broadcasted_iota(jnp.int32, (LANES, 4 * LANES), 0)
    scol = lax.broadcasted_iota(jnp.int32, (LANES, 4 * LANES), 1)
    state_block = (srow // ML_QK) == (scol // (2 * LANES))
    row_first = lax.broadcasted_iota(jnp.int32, (LANES, 1), 0) < ML_QK
    ones = jnp.ones((L, LANES), BF16)
    zer = jnp.zeros((L, LANES), BF16)

    gi = gi_ref[0]
    b3 = _dot(tri, _split3_bf16(gf_ref[0]))
    bcum = b3[:, :LANES] + b3[:, LANES:2 * LANES] + b3[:, 2 * LANES:]
    z = gi - bcum
    cmax = z
    shift = 1
    while shift < L:
        cmax = jnp.where(row_in_chunk >= shift, jnp.maximum(cmax, pltpu.roll(cmax, shift, 0)), cmax)
        shift *= 2
    m_intra = bcum + cmax

    m_prev = m_ref[0:1, :]
    mp_rows, mn_rows, gt_rows, s_a = [], [], [], []
    for c in range(nch):
        last = c * L + L - 1
        g_tot = bcum[last:last + 1, :]
        m_loc = g_tot + cmax[last:last + 1, :]
        m_new = jnp.maximum(g_tot + m_prev, m_loc)
        mp_rows.append(jnp.broadcast_to(m_prev, (L, LANES)))
        mn_rows.append(jnp.broadcast_to(m_new, (L, LANES)))
        gt_rows.append(jnp.broadcast_to(g_tot, (L, LANES)))
        s_a.append(jnp.exp(g_tot + m_prev - m_new))
        m_prev = m_new
    m_ref[0:1, :] = m_prev
    inter = bcum + jnp.concatenate(mp_rows, axis=0)
    m_t = jnp.maximum(inter, m_intra)
    w_inter = jnp.exp(inter - m_t)
    kws = jnp.exp(jnp.concatenate(gt_rows, axis=0) + z - jnp.concatenate(mn_rows, axis=0))

    a_exp = _dot(_split3_bf16(bcum - m_t), e64_ref[...])
    z_exp = _dot(_split3_bf16(z), e64_ref[...])
    wi_exp = _dot(_split3_bf16(w_inter), e64_ref[...])
    kw_exp = _dot(_split3_bf16(kws), e64_ref[...])
    em_exp = _dot(_split3_bf16(jnp.exp(-m_t)), e128_ref[...])

    un = []
    for c in range(nch):
        rows = slice(c * L, (c + 1) * L)
        for p in range(npair):
            lanes = slice(p * LANES, (p + 1) * LANES)
            qp = q_ref[0, rows, lanes]
            kp = k_ref[0, rows, lanes]
            v0 = v_ref[0, rows, 2 * p * ML_V:(2 * p + 1) * ML_V]
            v1 = v_ref[0, rows, (2 * p + 1) * ML_V:(2 * p + 2) * ML_V]
            z_src = jnp.sum(jnp.where(rowl == src, z_exp[rows, lanes], 0.0), axis=0, keepdims=True)
            un.append(dict(
                c=c, p=p, rows=rows, qp=qp,
                k2=jnp.concatenate([jnp.where(m0, kp, 0), jnp.where(m0, 0, kp)], axis=0),
                expd=jnp.where(rowl >= src, jnp.exp(a_exp[rows, lanes] + z_src), 0.0),
                qs=(qp * wi_exp[rows, lanes]).astype(BF16),
                kw=(kp * kw_exp[rows, lanes]).astype(BF16),
                vaug=jnp.concatenate([v0, ones, v1, ones], axis=1),
                rhs=jnp.concatenate([jnp.concatenate([v0, ones, zer, zer], axis=1),
                                     jnp.concatenate([zer, zer, v1, ones], axis=1)], axis=0),
            ))

    for u in un:
        scores = _dot_nt(u["qp"], u["k2"])
        u["w_intra"] = (u["expd"] * scores).astype(BF16)
    for u in un:
        u["c_loc"] = jnp.where(state_block, _dot_tn(u["kw"], u["vaug"]), 0.0)

    states = [c_ref[p] for p in range(npair)]
    for u in un:
        c, p = u["c"], u["p"]
        u["c_old"] = states[p].astype(BF16)
        sa = jnp.where(row_first, s_a[c][:, 2 * p:2 * p + 1], s_a[c][:, 2 * p + 1:2 * p + 2])
        states[p] = sa * states[p] + u["c_loc"]
    for p in range(npair):
        c_ref[p] = states[p]

    for u in un:
        u["nd"] = _dot(jnp.concatenate([u["qs"], u["w_intra"]], axis=1),
                       jnp.concatenate([u["c_old"], u["rhs"]], axis=0))

    for u in un:
        rows = u["rows"]
        for hh in range(2):
            head = 2 * u["p"] + hh
            cols = slice(head * ML_V, (head + 1) * ML_V)
            num = u["nd"][:, 2 * hh * LANES:(2 * hh + 1) * LANES]
            den = u["nd"][:, (2 * hh + 1) * LANES:(2 * hh + 2) * LANES]
            den = jnp.maximum(jnp.abs(den), em_exp[rows, cols])
            hv = num / den
            hv = hv * lax.rsqrt(jnp.mean(hv * hv, axis=-1, keepdims=True) + EPS)
            hv = hv * hn_ref[0:1, cols]
            o_ref[0, rows, cols] = (hv * og_ref[0, rows, cols]).astype(o_ref.dtype)


def _expand_matrix(width):
    rows = jnp.arange(LANES)[:, None]
    cols = jnp.arange(ML_HEADS * width)[None, :]
    e = (rows == cols // width).astype(BF16)
    return jnp.concatenate([e, e, e], axis=0)


def _mlstm_core(q, k, v, og, gi, gf, hn_w, tt):
    b, s, dv = v.shape
    nqk = q.shape[2]
    ng = gi.shape[2]
    e64 = _expand_matrix(ML_QK)
    e128 = _expand_matrix(ML_V)
    tok = lambda i, j: (i, j, 0)
    return pl.pallas_call(
        _mlstm_core_kernel,
        grid=(b, s // tt),
        in_specs=[
            pl.BlockSpec((1, tt, nqk), tok),
            pl.BlockSpec((1, tt, nqk), tok),
            pl.BlockSpec((1, tt, dv), tok),
            pl.BlockSpec((1, tt, dv), tok),
            pl.BlockSpec((1, tt, ng), tok),
            pl.BlockSpec((1, tt, ng), tok),
            _const_spec(hn_w.shape),
            _const_spec(e64.shape),
            _const_spec(e128.shape),
        ],
        out_specs=pl.BlockSpec((1, tt, dv), tok),
        out_shape=jax.ShapeDtypeStruct((b, s, dv), BF16),
        scratch_shapes=[
            pltpu.VMEM((ML_HEADS // 2, LANES, 4 * LANES), F32),
            pltpu.VMEM((8, LANES), F32),
        ],
        compiler_params=_params(("parallel", "arbitrary")),
        name="mlstm_core",
    )(q, k, v, og, gi, gf, hn_w, e64, e128)


def _pad_rows(a, rows=8):
    return jnp.pad(a, ((0, rows - a.shape[0]), (0, 0)))


def kernel(x, c, ada_w, ada_b, rw_mix, rw_w_in, rw_w0, rw_w2, rw_a0, rw_a2, rw_g2, rw_k_k, rw_k_a, rw_r_k, rw_gn_w, rw_gn_b, rw_w_out, ml_w_in, ml_conv_w, ml_conv_b, ml_b_i, ml_b_f, ml_hn_w, ml_w_out, ffn_w_gu, ffn_w_down, final_w):
    b, s, d = x.shape
    bf = lambda w: w.astype(BF16)
    final_w2 = final_w.reshape(1, d)

    mod = _adaln_mod(c, ada_w, ada_b)

    w_in = rw_w_in[0]
    o1 = d
    o2 = o1 + LORA_W
    o3 = o2 + d
    o4 = o3 + d
    o5 = o4 + LORA_A
    rw_weights = [bf(w_in[:, :o1]), bf(w_in[:, o2:o3]), bf(w_in[:, o3:o4]),
                  bf(w_in[:, o1:o2]), bf(w_in[:, o4:o5]), bf(w_in[:, o5:]),
                  bf(rw_w2[0]), bf(rw_a2[0]), bf(rw_g2[0])]
    mix = _pad_rows(rw_mix[0])
    vecs = _pad_rows(jnp.stack([rw_w0[0], rw_a0[0], rw_k_k[0], rw_k_a[0]]))
    proj = _rwkv_proj(x, mod[0], mix, vecs, rw_weights, TM_DENSE)
    core_vecs = _pad_rows(jnp.stack([rw_r_k[0].reshape(d), rw_gn_w[0], rw_gn_b[0]]))
    y = _rwkv_core(proj, core_vecs, b, s, d, TT_RWKV, PAIRS_RWKV)
    w_gu = bf(ffn_w_gu)
    w_down = bf(ffn_w_down)
    x = _post_ffn(x, y, mod[0], final_w2, bf(rw_w_out[0]), w_gu, w_down, 0, TM_DENSE, False)

    w_in = ml_w_in[0]
    nqk = 2 * ML_HEADS * ML_QK
    nv = ML_HEADS * ML_V
    pad_lanes = lambda w: jnp.pad(w, ((0, 0), (0, LANES - w.shape[1])))
    o_g = nqk + 2 * nv
    ml_weights = [bf(w_in[:, :nqk]), bf(w_in[:, nqk:nqk + nv]), bf(w_in[:, nqk + nv:o_g]),
                  bf(pad_lanes(w_in[:, o_g:o_g + ML_HEADS])), bf(pad_lanes(w_in[:, o_g + ML_HEADS:]))]
    cv = _pad_rows(jnp.concatenate([ml_conv_w[0], ml_conv_b[0][None, :]], axis=0))
    gb = _pad_rows(pad_lanes(jnp.stack([ml_b_i[0], ml_b_f[0]])))
    q, k, v, og, gi, gf = _mlstm_proj(x, mod[1], cv, gb, ml_weights, TM_DENSE)
    y = _mlstm_core(q, k, v, og, gi, gf, ml_hn_w[0].reshape(1, nv), TT_MLSTM)
    x = _post_ffn(x, y, mod[1], final_w2, bf(ml_w_out[0]), w_gu, w_down, 1, TM_DENSE, True)
    return x
```

```python
import functools

import jax
import jax.numpy as jnp
from jax import lax
from jax.experimental import pallas as pl
from jax.experimental.pallas import tpu as pltpu

F32 = jnp.float32
BF16 = jnp.bfloat16

EPS = 1e-6
GN_EPS = 64e-5
GATE_CAP = 15.0
DECAY_SCALE = 0.6065306597126334
CHUNK = 64
LANES = 128
RW_HEAD = 64
LORA_W = 64
LORA_A = 64
LORA_G = 160
ML_HEADS = 8
ML_QK = 64
ML_V = 128
CONV_W = 4
COL_BLOCK = 256
VMEM_LIMIT = 56 * 1024 * 1024
TM_DENSE = 512
TT_RWKV = 256
PAIRS_RWKV = 8
TT_MLSTM = 256

NT_DIMS = (((1,), (1,)), ((), ()))
TN_DIMS = (((0,), (0,)), ((), ()))


def _dot(a, b):
    return jnp.dot(a, b, preferred_element_type=F32)


def _dot_nt(a, b):
    return lax.dot_general(a, b, NT_DIMS, preferred_element_type=F32)


def _dot_tn(a, b):
    return lax.dot_general(a, b, TN_DIMS, preferred_element_type=F32)


def _sigmoid(z):
    return 0.5 * jnp.tanh(0.5 * z) + 0.5


def _softplus(z):
    return jnp.maximum(z, 0.0) + jnp.log(1.0 + jnp.exp(-jnp.abs(z)))


def _rms_mod(xv, sc, sh):
    inv = lax.rsqrt(jnp.mean(xv * xv, axis=-1, keepdims=True) + EPS)
    return xv * inv * (1.0 + sc) + sh


def _split_bf16(z):
    hi = z.astype(BF16)
    lo = (z - hi.astype(F32)).astype(BF16)
    return hi, lo


def _params(sem, flags=None):
    return pltpu.CompilerParams(dimension_semantics=sem, vmem_limit_bytes=VMEM_LIMIT, flags=flags)


def _const_spec(shape):
    nd = len(shape)
    return pl.BlockSpec(shape, lambda *_: (0,) * nd, pipeline_mode=pl.Buffered(1))


def _mod_kernel(c_ref, w_ref, b_ref, o_ref):
    c = c_ref[...]
    ca = (c * _sigmoid(c)).astype(BF16)
    o_ref[0] = _dot(ca, w_ref[0].astype(BF16)) + b_ref[0]


def _adaln_mod(c, ada_w, ada_b):
    depth, d, d6 = ada_w.shape
    b = c.shape[0]
    bp = 8
    c_pad = jnp.zeros((bp, d), F32).at[:b].set(c)
    nblk = d6 // d
    out = pl.pallas_call(
        _mod_kernel,
        grid=(depth, nblk),
        in_specs=[
            pl.BlockSpec((bp, d), lambda i, j: (0, 0)),
            pl.BlockSpec((1, d, d), lambda i, j: (i, 0, j)),
            pl.BlockSpec((1, 1, d), lambda i, j: (i, 0, j)),
        ],
        out_specs=pl.BlockSpec((1, bp, d), lambda i, j: (i, 0, j)),
        out_shape=jax.ShapeDtypeStruct((depth, bp, d6), F32),
        compiler_params=_params(("arbitrary", "arbitrary")),
        name="adaln_mod",
    )(c_pad, ada_w, ada_b.reshape(depth, 1, d6))
    mod = out[:, :b].reshape(depth, b, nblk, d)
    return jnp.pad(mod, ((0, 0), (0, 0), (0, 8 - nblk), (0, 0)))


def _rwkv_proj_kernel(x_ref, xp_ref, mod_ref, mix_ref, vec_ref, wr_ref, wk_ref, wv_ref,
                      wwl_ref, wal_ref, wgl_ref, w2_ref, a2_ref, g2_ref,
                      r_out, k_out, v_out, kk_out, a_out, lw_out, g_out, hbuf, xbuf):
    tm = x_ref.shape[1]
    t = pl.program_id(1)
    sh = mod_ref[0, 0:1, :]
    sc = mod_ref[0, 1:2, :]
    hp = _rms_mod(xp_ref[0], sc, sh)
    hbuf[0:8, :] = jnp.where(t > 0, hp, 0.0)
    w0 = vec_ref[0:1, :]
    a0 = vec_ref[1:2, :]
    k_k = vec_ref[2:3, :]
    k_a = vec_ref[3:4, :]

    h = _rms_mod(x_ref[0], sc, sh)
    hbuf[8:, :] = h
    xbuf[...] = hbuf[pl.ds(7, tm), :] - h

    def mixed(i):
        return (hbuf[8:, :] + xbuf[...] * mix_ref[i:i + 1, :]).astype(BF16)

    def put(out, val, cb):
        for q in range(COL_BLOCK // LANES):
            out[0, cb * (COL_BLOCK // LANES) + q] = val[:, q * LANES:(q + 1) * LANES].astype(out.dtype)

    ncb = r_out.shape[1] * LANES // COL_BLOCK
    col = lambda cb: slice(cb * COL_BLOCK, (cb + 1) * COL_BLOCK)
    lhs = mixed(0)
    for cb in range(ncb):
        put(r_out, _dot(lhs, wr_ref[:, col(cb)]), cb)
    lhs = mixed(3)
    for cb in range(ncb):
        put(v_out, _dot(lhs, wv_ref[:, col(cb)]), cb)
    twl = jnp.tanh(_dot(mixed(1), wwl_ref[...])).astype(BF16)
    alb = _dot(mixed(4), wal_ref[...]).astype(BF16)
    sgl = _sigmoid(_dot(mixed(5), wgl_ref[...])).astype(BF16)
    a_blocks = []
    for cb in range(ncb):
        c = col(cb)
        put(lw_out, -DECAY_SCALE * _sigmoid(w0[:, c] + _dot(twl, w2_ref[:, c])), cb)
        a = _sigmoid(a0[:, c] + _dot(alb, a2_ref[:, c]))
        put(a_out, a, cb)
        a_blocks.append(a)
        put(g_out, _dot(sgl, g2_ref[:, c]), cb)
    lhs = mixed(2)
    for cb in range(ncb):
        c = col(cb)
        k = _dot(lhs, wk_ref[:, c])
        put(kk_out, k * k_k[:, c], cb)
        put(k_out, k * (1.0 + (a_blocks[cb] - 1.0) * k_a[:, c]), cb)


def _rwkv_proj(x, mod, mix, vecs, weights, tm):
    b, s, d = x.shape
    npair = d // LANES
    nt = s // tm
    blk8 = tm // 8
    pair_shape = lambda dt: jax.ShapeDtypeStruct((b, npair, s, LANES), dt)
    pair_spec = pl.BlockSpec((1, npair, tm, LANES), lambda i, j: (i, 0, j, 0))
    return pl.pallas_call(
        _rwkv_proj_kernel,
        grid=(b, nt),
        in_specs=[
            pl.BlockSpec((1, tm, d), lambda i, j: (i, j, 0)),
            pl.BlockSpec((1, 8, d), lambda i, j: (i, jnp.maximum(j * blk8 - 1, 0), 0)),
            pl.BlockSpec((1, 8, d), lambda i, j: (i, 0, 0)),
            _const_spec(mix.shape),
            _const_spec(vecs.shape),
        ] + [_const_spec(w.shape) for w in weights],
        out_specs=[pair_spec] * 7,
        out_shape=[pair_shape(BF16)] * 5 + [pair_shape(F32), pair_shape(BF16)],
        scratch_shapes=[pltpu.VMEM((tm + 8, d), F32), pltpu.VMEM((tm, d), F32)],
        compiler_params=_params(("parallel", "arbitrary")),
        name="rwkv_proj",
    )(x, x, mod, mix, vecs, *weights)


def _rwkv_core_kernel(r_ref, k_ref, v_ref, kk_ref, a_ref, lw_ref, g_ref, vec_ref, vec_prev_ref,
                      o_ref, s_ref, rbar_s, y0_s, mlrt_s, s0_s, wl_s, g_s, bg_s, *, tiles_per_seq):
    npg = r_ref.shape[1]
    tt = r_ref.shape[2]
    L = CHUNK
    nch = tt // L
    nunit = npg * nch
    step = pl.program_id(0)

    @pl.when(step == 0)
    def _():
        for ref in (s_ref, rbar_s, y0_s, mlrt_s, s0_s, wl_s, g_s, bg_s):
            ref[...] = jnp.zeros_like(ref)

    fresh = (jnp.maximum(step - 1, 0) % tiles_per_seq) == 0

    lane = lax.broadcasted_iota(jnp.int32, (L, LANES), 1)
    m0 = lane < RW_HEAD
    rowl = lax.broadcasted_iota(jnp.int32, (L, LANES), 0)
    strict = rowl > (lane % L)
    incl = rowl >= (lane % L)
    eye_cat = (rowl == (lane % L)).astype(F32)
    row2 = lax.broadcasted_iota(jnp.int32, (LANES, LANES), 0)
    col2 = lax.broadcasted_iota(jnp.int32, (LANES, LANES), 1)
    blockdiag = (row2 // RW_HEAD) == (col2 // RW_HEAD)
    rowt = lax.broadcasted_iota(jnp.int32, (tt, tt), 0)
    colt = lax.broadcasted_iota(jnp.int32, (tt, tt), 1)
    tri = (((rowt // L) == (colt // L)) & (rowt >= colt)).astype(BF16)
    zeros_l = jnp.zeros((L, LANES), BF16)

    def head_sum(z):
        lo = jnp.sum(jnp.where(m0, z, 0.0), axis=-1, keepdims=True)
        hi = jnp.sum(jnp.where(m0, 0.0, z), axis=-1, keepdims=True)
        return jnp.where(m0, lo, hi)

    def split_heads(z):
        zb = z.astype(BF16)
        zero = jnp.zeros_like(zb)
        return jnp.concatenate([jnp.where(m0, zb, zero), jnp.where(m0, zero, zb)], axis=0)

    units = [(p, c) for p in range(npg) for c in range(nch)]

    cums = []
    for p in range(npg):
        hi, lo = _split_bf16(lw_ref[0, p])
        both = _dot(tri, jnp.concatenate([hi, lo], axis=1))
        cums.append(both[:, :LANES] + both[:, LANES:])

    pu = []
    for (p, c) in units:
        rows = slice(c * L, (c + 1) * L)
        r = r_ref[0, p, rows, :].astype(F32)
        k = k_ref[0, p, rows, :].astype(F32)
        v = v_ref[0, p, rows, :]
        kk = kk_ref[0, p, rows, :].astype(F32)
        a = a_ref[0, p, rows, :].astype(F32)
        g = g_ref[0, p, rows, :]
        lw = lw_ref[0, p, rows, :]
        cum = cums[p][rows, :]
        kkn = kk * jnp.minimum(lax.rsqrt(head_sum(kk * kk)), 1e12)
        bvec = kkn * a
        c_last = cum[L - 1:L, :]
        at = -kkn * jnp.exp(cum - lw)
        rt = r * jnp.exp(cum)
        inv_w = jnp.exp(-cum)
        d_last = jnp.exp(c_last - cum)
        lanes = slice(p * LANES, (p + 1) * LANES)
        bonus = head_sum(r * k * vec_ref[0:1, lanes]) * v.astype(F32)
        pu.append(dict(
            rt=rt, v=v, w_last=jnp.exp(c_last), g=g, bg=bonus * g.astype(F32),
            lhs=jnp.concatenate([at, rt], axis=0).astype(BF16),
            rhs=jnp.concatenate([split_heads(bvec * inv_w), split_heads(k * inv_w)], axis=0),
            dk=jnp.concatenate([bvec * d_last, k * d_last], axis=0).astype(BF16),
            at_s=split_heads(at), v_s=split_heads(v),
        ))

    states = [jnp.where(fresh, 0.0, s_ref[p]) for p in range(npg)]
    ys = {}

    def scan_level(c):
        for p in range(npg):
            j = p * nch + c
            s_old = states[p]
            sb = s_old.astype(BF16)
            ys[j] = _dot_nt(rbar_s[j], sb) + y0_s[j]
            states[p] = s_old * wl_s[j] + _dot_nt(sb, mlrt_s[j]) + s0_s[j]

    per_level = -(-nunit // nch)
    for j, u in enumerate(pu):
        if j % per_level == 0:
            scan_level(j // per_level)
        gram = _dot_nt(u["lhs"], u["rhs"])
        u["pw"] = jnp.where(strict, gram[:L, :LANES], 0.0)
        u["r_b"] = jnp.where(incl, gram[L:, :LANES], 0.0).astype(BF16)
        u["ak_rk"] = jnp.concatenate([jnp.where(strict, gram[:L, LANES:], 0.0),
                                      jnp.where(incl, gram[L:, LANES:], 0.0)], axis=0).astype(BF16)
        u["tinv"] = eye_cat + u["pw"]
        u["pw_s"] = split_heads(u["pw"])
    for p in range(npg):
        s_ref[p] = states[p]

    for j in range(nunit):
        p, c = divmod(j, nch)
        lanes = slice(p * LANES, (p + 1) * LANES)
        y = ys[j]
        mu = head_sum(y) * (1.0 / RW_HEAD)
        yc = y - mu
        var = head_sum(yc * yc) * (1.0 / RW_HEAD)
        yn = yc * lax.rsqrt(var + GN_EPS) * vec_prev_ref[1:2, lanes] + vec_prev_ref[2:3, lanes]
        o_ref[0, c * L:(c + 1) * L, lanes] = (yn * g_s[j].astype(F32) + bg_s[j]).astype(o_ref.dtype)

    for u in pu:
        both = _dot(u["ak_rk"], u["v_s"])
        u["akv"] = both[:L]
        u["rkv"] = both[L:]

    for u in pu:
        u["pw"] = _dot(u["pw"].astype(BF16), u["pw_s"])
        u["pw_s"] = split_heads(u["pw"])
    n_levels = L.bit_length() - 1
    for level in range(1, n_levels):
        last = level == n_levels - 1
        for u in pu:
            if last:
                u["tinv"] = u["tinv"] + _dot(u["tinv"].astype(BF16), u["pw_s"])
            else:
                lhs = jnp.concatenate([u["tinv"], u["pw"]], axis=0).astype(BF16)
                both = _dot(lhs, u["pw_s"])
                u["tinv"] = u["tinv"] + both[:L]
                u["pw"] = both[L:]
                u["pw_s"] = split_heads(u["pw"])

    for u in pu:
        xin = jnp.concatenate([u["at_s"], split_heads(u["akv"])], axis=1)
        u["z"] = _dot(u["tinv"].astype(BF16), xin)
    for u in pu:
        z = u["z"]
        rhs = jnp.concatenate([split_heads(z[:, :LANES]), split_heads(z[:, LANES:])], axis=1)
        u["w"] = _dot(u["r_b"], rhs)
    for u in pu:
        bot = jnp.concatenate([zeros_l, u["v"]], axis=1)
        u["ms"] = _dot_tn(u["dk"], jnp.concatenate([u["z"].astype(BF16), bot], axis=0))

    for j, u in enumerate(pu):
        rbar_s[j] = (u["rt"] + u["w"][:, :LANES]).astype(BF16)
        y0_s[j] = u["w"][:, LANES:] + u["rkv"]
        mlrt_s[j] = jnp.where(blockdiag, u["ms"][:, :LANES], 0.0).astype(BF16)
        s0_s[j] = jnp.where(blockdiag, u["ms"][:, LANES:].T, 0.0)
        wl_s[j] = u["w_last"]
        g_s[j] = u["g"]
        bg_s[j] = u["bg"]


def _rwkv_core(proj, vecs, b, s, d, tt, npg):
    npair = d // LANES
    ngrp = npair // npg
    nt = s // tt
    ntiles = b * ngrp * nt
    nunit = npg * (tt // CHUNK)

    def tile_coords(tile):
        return tile // (ngrp * nt), (tile // nt) % ngrp, tile % nt

    def in_map(i):
        bi, gi, ti = tile_coords(jnp.minimum(i, ntiles - 1))
        return (bi, gi, ti, 0)

    def out_map(i):
        bi, gi, ti = tile_coords(jnp.maximum(i - 1, 0))
        return (bi, ti, gi)

    in_spec = pl.BlockSpec((1, npg, tt, LANES), in_map)
    vec_spec = pl.BlockSpec((8, npg * LANES), lambda i: (0, tile_coords(jnp.minimum(i, ntiles - 1))[1]))
    vec_prev_spec = pl.BlockSpec((8, npg * LANES), lambda i: (0, tile_coords(jnp.maximum(i - 1, 0))[1]))
    L = CHUNK
    return pl.pallas_call(
        functools.partial(_rwkv_core_kernel, tiles_per_seq=nt),
        grid=(ntiles + 1,),
        in_specs=[in_spec] * 7 + [vec_spec, vec_prev_spec],
        out_specs=pl.BlockSpec((1, tt, npg * LANES), out_map),
        out_shape=jax.ShapeDtypeStruct((b, s, d), BF16),
        scratch_shapes=[
            pltpu.VMEM((npg, LANES, LANES), F32),
            pltpu.VMEM((nunit, L, LANES), BF16),
            pltpu.VMEM((nunit, L, LANES), F32),
            pltpu.VMEM((nunit, LANES, LANES), BF16),
            pltpu.VMEM((nunit, LANES, LANES), F32),
            pltpu.VMEM((nunit, 1, LANES), F32),
            pltpu.VMEM((nunit, L, LANES), BF16),
            pltpu.VMEM((nunit, L, LANES), F32),
        ],
        compiler_params=_params(("arbitrary",)),
        name="rwkv_core",
    )(*proj, vecs, vecs)


def _post_ffn_kernel(x_ref, y_ref, mod_ref, fw_ref, wo_ref, wgu_ref, wd_ref, o_ref, act_ref,
                     *, final):
    g1 = mod_ref[0, 2:3, :]
    sh2 = mod_ref[0, 3:4, :]
    sc2 = mod_ref[0, 4:5, :]
    g2 = mod_ref[0, 5:6, :]
    x1 = x_ref[0] + g1 * _dot(y_ref[0], wo_ref[...])
    h = _rms_mod(x1, sc2, sh2).astype(BF16)
    d_ff = act_ref.shape[1]
    for c0 in range(0, d_ff, COL_BLOCK):
        cols = slice(c0, c0 + COL_BLOCK)
        gate = _dot(h, wgu_ref[0, :, cols])
        up = _dot(h, wgu_ref[0, :, d_ff + c0:d_ff + c0 + COL_BLOCK])
        act_ref[:, cols] = (gate * _sigmoid(gate) * up).astype(BF16)
    x2 = x1 + g2 * _dot(act_ref[...], wd_ref[0])
    if final:
        inv = lax.rsqrt(jnp.mean(x2 * x2, axis=-1, keepdims=True) + EPS)
        x2 = x2 * inv * fw_ref[0:1, :]
    o_ref[0] = x2


def _post_ffn(x, y, mod, final_w, w_out, w_gu, w_d, layer, tm, final):
    b, s, d = x.shape
    d_ff = w_d.shape[1]
    assert d_ff % COL_BLOCK == 0
    kern = functools.partial(_post_ffn_kernel, final=final)
    tok = lambda i, j: (i, j, 0)
    layer_spec = lambda w: pl.BlockSpec((1,) + w.shape[1:], lambda i, j: (layer, 0, 0),
                                        pipeline_mode=pl.Buffered(1))
    return pl.pallas_call(
        kern,
        grid=(b, s // tm),
        in_specs=[
            pl.BlockSpec((1, tm, d), tok),
            pl.BlockSpec((1, tm, d), tok),
            pl.BlockSpec((1, 8, d), lambda i, j: (i, 0, 0)),
            _const_spec(final_w.shape),
            _const_spec(w_out.shape),
            layer_spec(w_gu),
            layer_spec(w_d),
        ],
        out_specs=pl.BlockSpec((1, tm, d), tok),
        out_shape=jax.ShapeDtypeStruct((b, s, d), F32),
        scratch_shapes=[pltpu.VMEM((tm, d_ff), BF16)],
        compiler_params=_params(("parallel", "parallel")),
        name="post_ffn_final" if final else "post_ffn",
    )(x, y, mod, final_w, w_out, w_gu, w_d)


def _softcap(z):
    return GATE_CAP * jnp.tanh(z * (1.0 / GATE_CAP))


def _mlstm_proj_kernel(x_ref, xp_ref, mod_ref, cv_ref, gb_ref, wqk_ref, wv_ref,
                       wo_ref, wgi_ref, wgf_ref,
                       q_out, k_out, v_out, og_out, gi_out, gf_out, pbuf):
    tm = x_ref.shape[1]
    t = pl.program_id(1)
    sh = mod_ref[0, 0:1, :]
    sc = mod_ref[0, 1:2, :]
    h = _rms_mod(x_ref[0], sc, sh).astype(BF16)
    hp = _rms_mod(xp_ref[0], sc, sh).astype(BF16)

    half = wqk_ref.shape[1] // 2
    for c0 in range(0, wqk_ref.shape[1], COL_BLOCK):
        c = slice(c0, c0 + COL_BLOCK)
        pqk = _dot(h, wqk_ref[:, c])
        pbuf[0:8, c] = jnp.where(t > 0, _dot(hp, wqk_ref[:, c]), 0.0)
        pbuf[8:, c] = pqk
        conv = cv_ref[CONV_W:CONV_W + 1, c] + pqk * cv_ref[CONV_W - 1:CONV_W, c]
        for j in range(1, CONV_W):
            conv = conv + pbuf[pl.ds(8 - j, tm), c] * cv_ref[CONV_W - 1 - j:CONV_W - j, c]
        qk = conv * _sigmoid(conv)
        if c0 < half:
            q_out[0, :, c] = (qk * (ML_QK ** -0.5)).astype(q_out.dtype)
        else:
            k_out[0, :, c0 - half:c0 - half + COL_BLOCK] = qk.astype(k_out.dtype)
    for c0 in range(0, wv_ref.shape[1], COL_BLOCK):
        c = slice(c0, c0 + COL_BLOCK)
        v_out[0, :, c] = _dot(h, wv_ref[:, c]).astype(v_out.dtype)
    for c0 in range(0, wo_ref.shape[1], COL_BLOCK):
        c = slice(c0, c0 + COL_BLOCK)
        og_out[0, :, c] = _sigmoid(_dot(h, wo_ref[:, c])).astype(og_out.dtype)

    lane = lax.broadcasted_iota(jnp.int32, gi_out.shape[1:], 1)
    zi = _dot(h, wgi_ref[...]) + gb_ref[0:1, :]
    zf = _dot(h, wgf_ref[...]) + gb_ref[1:2, :]
    gi_out[0] = jnp.where(lane < ML_HEADS, _softcap(zi), 0.0)
    gf_out[0] = jnp.where(lane < ML_HEADS, -_softplus(-_softcap(zf)), 0.0)


def _mlstm_proj(x, mod, cv, gb, weights, tm):
    b, s, d = x.shape
    w_qk, w_v, w_o, w_gi, w_gf = weights
    nqk = w_qk.shape[1] // 2
    ng = w_gi.shape[1]
    blk8 = tm // 8
    tok = lambda i, j: (i, j, 0)
    return pl.pallas_call(
        _mlstm_proj_kernel,
        grid=(b, s // tm),
        in_specs=[
            pl.BlockSpec((1, tm, d), tok),
            pl.BlockSpec((1, 8, d), lambda i, j: (i, jnp.maximum(j * blk8 - 1, 0), 0)),
            pl.BlockSpec((1, 8, d), lambda i, j: (i, 0, 0)),
            _const_spec(cv.shape),
            _const_spec(gb.shape),
        ] + [_const_spec(w.shape) for w in weights],
        out_specs=[
            pl.BlockSpec((1, tm, nqk), tok),
            pl.BlockSpec((1, tm, nqk), tok),
            pl.BlockSpec((1, tm, w_v.shape[1]), tok),
            pl.BlockSpec((1, tm, w_o.shape[1]), tok),
            pl.BlockSpec((1, tm, ng), tok),
            pl.BlockSpec((1, tm, ng), tok),
        ],
        out_shape=[
            jax.ShapeDtypeStruct((b, s, nqk), BF16),
            jax.ShapeDtypeStruct((b, s, nqk), BF16),
            jax.ShapeDtypeStruct((b, s, w_v.shape[1]), BF16),
            jax.ShapeDtypeStruct((b, s, w_o.shape[1]), BF16),
            jax.ShapeDtypeStruct((b, s, ng), F32),
            jax.ShapeDtypeStruct((b, s, ng), F32),
        ],
        scratch_shapes=[pltpu.VMEM((tm + 8, w_qk.shape[1]), F32)],
        compiler_params=_params(("parallel", "arbitrary")),
        name="mlstm_proj",
    )(x, x, mod, cv, gb, *weights)


def _split3_bf16(z):
    hi = z.astype(BF16)
    r1 = z - hi.astype(F32)
    mid = r1.astype(BF16)
    lo = (r1 - mid.astype(F32)).astype(BF16)
    return jnp.concatenate([hi, mid, lo], axis=1)


def _mlstm_core_kernel(q_ref, k_ref, v_ref, og_ref, gi_ref, gf_ref, hn_ref, e64_ref, e128_ref,
                       o_ref, c_ref, m_ref):
    tt = q_ref.shape[1]
    L = CHUNK
    nch = tt // L
    npair = ML_HEADS // 2
    t = pl.program_id(1)

    @pl.when(t == 0)
    def _():
        c_ref[...] = jnp.zeros_like(c_ref)
        m_ref[...] = jnp.zeros_like(m_ref)

    lane = lax.broadcasted_iota(jnp.int32, (L, LANES), 1)
    rowl = lax.broadcasted_iota(jnp.int32, (L, LANES), 0)
    m0 = lane < ML_QK
    src = lane % L
    rowt = lax.broadcasted_iota(jnp.int32, (tt, tt), 0)
    colt = lax.broadcasted_iota(jnp.int32, (tt, tt), 1)
    tri = (((rowt // L) == (colt // L)) & (rowt >= colt)).astype(BF16)
    row_in_chunk = lax.broadcasted_iota(jnp.int32, (tt, LANES), 0) % L
    row_first = lax.broadcasted_iota(jnp.int32, (LANES, 1), 0) < ML_QK
    ones = jnp.ones((L, LANES), BF16)

    def split_heads(zv):
        zb = zv.astype(BF16)
        zero = jnp.zeros_like(zb)
        return jnp.concatenate([jnp.where(m0, zb, zero), jnp.where(m0, zero, zb)], axis=0)

    gi = gi_ref[0]
    b3 = _dot(tri, _split3_bf16(gf_ref[0]))
    bcum = b3[:, :LANES] + b3[:, LANES:2 * LANES] + b3[:, 2 * LANES:]
    z = gi - bcum
    cmax = z
    shift = 1
    while shift < L:
        cmax = jnp.where(row_in_chunk >= shift, jnp.maximum(cmax, pltpu.roll(cmax, shift, 0)), cmax)
        shift *= 2
    m_intra = bcum + cmax

    m_prev = m_ref[0:1, :]
    mp_rows, mn_rows, gt_rows, s_a = [], [], [], []
    for c in range(nch):
        last = c * L + L - 1
        g_tot = bcum[last:last + 1, :]
        m_loc = g_tot + cmax[last:last + 1, :]
        m_new = jnp.maximum(g_tot + m_prev, m_loc)
        mp_rows.append(jnp.broadcast_to(m_prev, (L, LANES)))
        mn_rows.append(jnp.broadcast_to(m_new, (L, LANES)))
        gt_rows.append(jnp.broadcast_to(g_tot, (L, LANES)))
        s_a.append(jnp.exp(g_tot + m_prev - m_new))
        m_prev = m_new
    m_ref[0:1, :] = m_prev
    inter = bcum + jnp.concatenate(mp_rows, axis=0)
    m_t = jnp.maximum(inter, m_intra)
    w_inter = jnp.exp(inter - m_t)
    kws = jnp.exp(jnp.concatenate(gt_rows, axis=0) + z - jnp.concatenate(mn_rows, axis=0))

    head_lane = lax.broadcasted_iota(jnp.int32, (tt, LANES), 1) < ML_HEADS

    def pack3(val):
        val = jnp.where(head_lane, val, 0.0)
        hi = val.astype(BF16).astype(F32)
        r1 = val - hi
        mid = r1.astype(BF16).astype(F32)
        lo = (r1 - mid).astype(BF16).astype(F32)
        packed = hi + pltpu.roll(mid, ML_HEADS, 1) + pltpu.roll(lo, 2 * ML_HEADS, 1)
        return packed.astype(BF16)

    spread = _dot(jnp.concatenate([pack3(bcum - m_t), pack3(z), pack3(w_inter), pack3(kws)], axis=0),
                  e64_ref[...])
    a_exp, z_exp, wi_exp, kw_exp = (spread[i * tt:(i + 1) * tt] for i in range(4))
    em_exp = _dot(pack3(jnp.exp(-m_t)), e128_ref[...])

    un = []
    for c in range(nch):
        rows = slice(c * L, (c + 1) * L)
        for p in range(npair):
            lanes = slice(p * LANES, (p + 1) * LANES)
            qp = q_ref[0, rows, lanes]
            kp = k_ref[0, rows, lanes]
            v0 = v_ref[0, rows, 2 * p * ML_V:(2 * p + 1) * ML_V]
            v1 = v_ref[0, rows, (2 * p + 1) * ML_V:(2 * p + 2) * ML_V]
            z_src = jnp.sum(jnp.where(rowl == src, z_exp[rows, lanes], 0.0), axis=0, keepdims=True)
            un.append(dict(
                c=c, p=p, rows=rows, qp=qp,
                k2=split_heads(kp),
                expd=jnp.where(rowl >= src, jnp.exp(a_exp[rows, lanes] + z_src), 0.0),
                qs=split_heads(qp * wi_exp[rows, lanes]),
                kw=split_heads(kp * kw_exp[rows, lanes]),
                vst=jnp.concatenate([jnp.concatenate([v0, ones], axis=1),
                                     jnp.concatenate([v1, ones], axis=1)], axis=0),
            ))

    for u in un:
        scores = _dot_nt(u["qp"], u["k2"])
        u["w_intra"] = split_heads(u["expd"] * scores)
    for u in un:
        u["c_loc"] = _dot_tn(u["kw"], u["vst"])

    states = [c_ref[p] for p in range(npair)]
    for u in un:
        c, p = u["c"], u["p"]
        u["c_old"] = states[p].astype(BF16)
        sa = jnp.where(row_first, s_a[c][:, 2 * p:2 * p + 1], s_a[c][:, 2 * p + 1:2 * p + 2])
        states[p] = sa * states[p] + u["c_loc"]
    for p in range(npair):
        c_ref[p] = states[p]

    for u in un:
        u["nd"] = _dot(jnp.concatenate([u["qs"], u["w_intra"]], axis=1),
                       jnp.concatenate([u["c_old"], u["vst"]], axis=0))

    for u in un:
        rows = u["rows"]
        for hh in range(2):
            head = 2 * u["p"] + hh
            cols = slice(head * ML_V, (head + 1) * ML_V)
            num = u["nd"][hh * L:(hh + 1) * L, :LANES]
            den = u["nd"][hh * L:(hh + 1) * L, LANES:]
            den = jnp.maximum(jnp.abs(den), em_exp[rows, cols])
            hv = num / den
            hv = hv * lax.rsqrt(jnp.mean(hv * hv, axis=-1, keepdims=True) + EPS)
            hv = hv * hn_ref[0:1, cols]
            o_ref[0, rows, cols] = (hv * og_ref[0, rows, cols]).astype(o_ref.dtype)


def _expand_matrix(width):
    rows = jnp.arange(LANES)[:, None]
    cols = jnp.arange(ML_HEADS * width)[None, :]
    return ((rows < 3 * ML_HEADS) & (rows % ML_HEADS == cols // width)).astype(BF16)


def _mlstm_core(q, k, v, og, gi, gf, hn_w, tt):
    b, s, dv = v.shape
    nqk = q.shape[2]
    ng = gi.shape[2]
    e64 = _expand_matrix(ML_QK)
    e128 = _expand_matrix(ML_V)
    tok = lambda i, j: (i, j, 0)
    return pl.pallas_call(
        _mlstm_core_kernel,
        grid=(b, s // tt),
        in_specs=[
            pl.BlockSpec((1, tt, nqk), tok),
            pl.BlockSpec((1, tt, nqk), tok),
            pl.BlockSpec((1, tt, dv), tok),
            pl.BlockSpec((1, tt, dv), tok),
            pl.BlockSpec((1, tt, ng), tok),
            pl.BlockSpec((1, tt, ng), tok),
            _const_spec(hn_w.shape),
            _const_spec(e64.shape),
            _const_spec(e128.shape),
        ],
        out_specs=pl.BlockSpec((1, tt, dv), tok),
        out_shape=jax.ShapeDtypeStruct((b, s, dv), BF16),
        scratch_shapes=[
            pltpu.VMEM((ML_HEADS // 2, LANES, 2 * LANES), F32),
            pltpu.VMEM((8, LANES), F32),
        ],
        compiler_params=_params(("parallel", "arbitrary")),
        name="mlstm_core",
    )(q, k, v, og, gi, gf, hn_w, e64, e128)


def _pad_rows(a, rows=8):
    return jnp.pad(a, ((0, rows - a.shape[0]), (0, 0)))


def kernel(x, c, ada_w, ada_b, rw_mix, rw_w_in, rw_w0, rw_w2, rw_a0, rw_a2, rw_g2, rw_k_k, rw_k_a, rw_r_k, rw_gn_w, rw_gn_b, rw_w_out, ml_w_in, ml_conv_w, ml_conv_b, ml_b_i, ml_b_f, ml_hn_w, ml_w_out, ffn_w_gu, ffn_w_down, final_w):
    b, s, d = x.shape
    bf = lambda w: w.astype(BF16)
    final_w2 = final_w.reshape(1, d)

    mod = _adaln_mod(c, ada_w, ada_b)

    w_in = rw_w_in[0]
    o1 = d
    o2 = o1 + LORA_W
    o3 = o2 + d
    o4 = o3 + d
    o5 = o4 + LORA_A
    rw_weights = [bf(w_in[:, :o1]), bf(w_in[:, o2:o3]), bf(w_in[:, o3:o4]),
                  bf(w_in[:, o1:o2]), bf(w_in[:, o4:o5]), bf(w_in[:, o5:]),
                  bf(rw_w2[0]), bf(rw_a2[0]), bf(rw_g2[0])]
    mix = _pad_rows(rw_mix[0])
    vecs = _pad_rows(jnp.stack([rw_w0[0], rw_a0[0], rw_k_k[0], rw_k_a[0]]))
    proj = _rwkv_proj(x, mod[0], mix, vecs, rw_weights, TM_DENSE)
    core_vecs = _pad_rows(jnp.stack([rw_r_k[0].reshape(d), rw_gn_w[0], rw_gn_b[0]]))
    y = _rwkv_core(proj, core_vecs, b, s, d, TT_RWKV, PAIRS_RWKV)
    w_gu = bf(ffn_w_gu)
    w_down = bf(ffn_w_down)
    x = _post_ffn(x, y, mod[0], final_w2, bf(rw_w_out[0]), w_gu, w_down, 0, TM_DENSE, False)

    w_in = ml_w_in[0]
    nqk = 2 * ML_HEADS * ML_QK
    nv = ML_HEADS * ML_V
    pad_lanes = lambda w: jnp.pad(w, ((0, 0), (0, LANES - w.shape[1])))
    o_g = nqk + 2 * nv
    ml_weights = [bf(w_in[:, :nqk]), bf(w_in[:, nqk:nqk + nv]), bf(w_in[:, nqk + nv:o_g]),
                  bf(pad_lanes(w_in[:, o_g:o_g + ML_HEADS])), bf(pad_lanes(w_in[:, o_g + ML_HEADS:]))]
    cv = _pad_rows(jnp.concatenate([ml_conv_w[0], ml_conv_b[0][None, :]], axis=0))
    gb = _pad_rows(pad_lanes(jnp.stack([ml_b_i[0], ml_b_f[0]])))
    q, k, v, og, gi, gf = _mlstm_proj(x, mod[1], cv, gb, ml_weights, TM_DENSE)
    y = _mlstm_core(q, k, v, og, gi, gf, ml_hn_w[0].reshape(1, nv), TT_MLSTM)
    x = _post_ffn(x, y, mod[1], final_w2, bf(ml_w_out[0]), w_gu, w_down, 1, TM_DENSE, True)
    return x
```

```python
import functools

import jax
import jax.numpy as jnp
from jax import lax
from jax.experimental import pallas as pl
from jax.experimental.pallas import tpu as pltpu

F32 = jnp.float32
BF16 = jnp.bfloat16

EPS = 1e-6
GN_EPS = 64e-5
GATE_CAP = 15.0
DECAY_SCALE = 0.6065306597126334
CHUNK = 64
LANES = 128
RW_HEAD = 64
LORA_W = 64
LORA_A = 64
LORA_G = 160
ML_HEADS = 8
ML_QK = 64
ML_V = 128
CONV_W = 4
COL_BLOCK = 256
VMEM_LIMIT = 56 * 1024 * 1024
TM_DENSE = 512
TT_RWKV = 256
PAIRS_RWKV = 8
TT_MLSTM = 256

NT_DIMS = (((1,), (1,)), ((), ()))
TN_DIMS = (((0,), (0,)), ((), ()))


def _dot(a, b):
    return jnp.dot(a, b, preferred_element_type=F32)


def _dot_nt(a, b):
    return lax.dot_general(a, b, NT_DIMS, preferred_element_type=F32)


def _dot_tn(a, b):
    return lax.dot_general(a, b, TN_DIMS, preferred_element_type=F32)


def _sigmoid(z):
    return 0.5 * jnp.tanh(0.5 * z) + 0.5


def _softplus(z):
    return jnp.maximum(z, 0.0) + jnp.log(1.0 + jnp.exp(-jnp.abs(z)))


def _rms_mod(xv, sc, sh):
    inv = lax.rsqrt(jnp.mean(xv * xv, axis=-1, keepdims=True) + EPS)
    return xv * inv * (1.0 + sc) + sh


def _split_bf16(z):
    hi = z.astype(BF16)
    lo = (z - hi.astype(F32)).astype(BF16)
    return hi, lo


def _params(sem, flags=None):
    return pltpu.CompilerParams(dimension_semantics=sem, vmem_limit_bytes=VMEM_LIMIT, flags=flags)


def _const_spec(shape):
    nd = len(shape)
    return pl.BlockSpec(shape, lambda *_: (0,) * nd, pipeline_mode=pl.Buffered(1))


def _mod_kernel(c_ref, w_ref, b_ref, o_ref):
    c = c_ref[...]
    ca = (c * _sigmoid(c)).astype(BF16)
    o_ref[0] = _dot(ca, w_ref[0].astype(BF16)) + b_ref[0]


def _adaln_mod(c, ada_w, ada_b):
    depth, d, d6 = ada_w.shape
    b = c.shape[0]
    bp = 8
    c_pad = jnp.zeros((bp, d), F32).at[:b].set(c)
    nblk = d6 // d
    out = pl.pallas_call(
        _mod_kernel,
        grid=(depth, nblk),
        in_specs=[
            pl.BlockSpec((bp, d), lambda i, j: (0, 0)),
            pl.BlockSpec((1, d, d), lambda i, j: (i, 0, j)),
            pl.BlockSpec((1, 1, d), lambda i, j: (i, 0, j)),
        ],
        out_specs=pl.BlockSpec((1, bp, d), lambda i, j: (i, 0, j)),
        out_shape=jax.ShapeDtypeStruct((depth, bp, d6), F32),
        compiler_params=_params(("arbitrary", "arbitrary")),
        name="adaln_mod",
    )(c_pad, ada_w, ada_b.reshape(depth, 1, d6))
    mod = out[:, :b].reshape(depth, b, nblk, d)
    return jnp.pad(mod, ((0, 0), (0, 0), (0, 8 - nblk), (0, 0)))


def _rwkv_proj_kernel(x_ref, xp_ref, mod_ref, mix_ref, vec_ref, wr_ref, wk_ref, wv_ref,
                      wwl_ref, wal_ref, wgl_ref, w2_ref, a2_ref, g2_ref,
                      r_out, k_out, v_out, kk_out, a_out, lw_out, g_out, hbuf, xbuf):
    tm = x_ref.shape[1]
    t = pl.program_id(1)
    sh = mod_ref[0, 0:1, :]
    sc = mod_ref[0, 1:2, :]
    hp = _rms_mod(xp_ref[0], sc, sh)
    hbuf[0:8, :] = jnp.where(t > 0, hp, 0.0)
    w0 = vec_ref[0:1, :]
    a0 = vec_ref[1:2, :]
    k_k = vec_ref[2:3, :]
    k_a = vec_ref[3:4, :]

    h = _rms_mod(x_ref[0], sc, sh)
    hbuf[8:, :] = h
    xbuf[...] = hbuf[pl.ds(7, tm), :] - h

    def mixed(i):
        return (hbuf[8:, :] + xbuf[...] * mix_ref[i:i + 1, :]).astype(BF16)

    def put(out, val, cb):
        for q in range(COL_BLOCK // LANES):
            out[0, cb * (COL_BLOCK // LANES) + q] = val[:, q * LANES:(q + 1) * LANES].astype(out.dtype)

    ncb = r_out.shape[1] * LANES // COL_BLOCK
    col = lambda cb: slice(cb * COL_BLOCK, (cb + 1) * COL_BLOCK)
    lhs = mixed(0)
    for cb in range(ncb):
        put(r_out, _dot(lhs, wr_ref[:, col(cb)]), cb)
    lhs = mixed(3)
    for cb in range(ncb):
        put(v_out, _dot(lhs, wv_ref[:, col(cb)]), cb)
    twl = jnp.tanh(_dot(mixed(1), wwl_ref[...])).astype(BF16)
    alb = _dot(mixed(4), wal_ref[...]).astype(BF16)
    sgl = _sigmoid(_dot(mixed(5), wgl_ref[...])).astype(BF16)
    a_blocks = []
    for cb in range(ncb):
        c = col(cb)
        put(lw_out, -DECAY_SCALE * _sigmoid(w0[:, c] + _dot(twl, w2_ref[:, c])), cb)
        a = _sigmoid(a0[:, c] + _dot(alb, a2_ref[:, c]))
        put(a_out, a, cb)
        a_blocks.append(a)
        put(g_out, _dot(sgl, g2_ref[:, c]), cb)
    lhs = mixed(2)
    for cb in range(ncb):
        c = col(cb)
        k = _dot(lhs, wk_ref[:, c])
        put(kk_out, k * k_k[:, c], cb)
        put(k_out, k * (1.0 + (a_blocks[cb] - 1.0) * k_a[:, c]), cb)


def _rwkv_proj(x, mod, mix, vecs, weights, tm):
    b, s, d = x.shape
    assert s % tm == 0 and d % COL_BLOCK == 0
    npair = d // LANES
    nt = s // tm
    blk8 = tm // 8
    pair_shape = lambda dt: jax.ShapeDtypeStruct((b, npair, s, LANES), dt)
    pair_spec = pl.BlockSpec((1, npair, tm, LANES), lambda i, j: (i, 0, j, 0))
    return pl.pallas_call(
        _rwkv_proj_kernel,
        grid=(b, nt),
        in_specs=[
            pl.BlockSpec((1, tm, d), lambda i, j: (i, j, 0)),
            pl.BlockSpec((1, 8, d), lambda i, j: (i, jnp.maximum(j * blk8 - 1, 0), 0)),
            pl.BlockSpec((1, 8, d), lambda i, j: (i, 0, 0)),
            _const_spec(mix.shape),
            _const_spec(vecs.shape),
        ] + [_const_spec(w.shape) for w in weights],
        out_specs=[pair_spec] * 7,
        out_shape=[pair_shape(BF16)] * 5 + [pair_shape(F32), pair_shape(BF16)],
        scratch_shapes=[pltpu.VMEM((tm + 8, d), F32), pltpu.VMEM((tm, d), F32)],
        compiler_params=_params(("parallel", "arbitrary")),
        name="rwkv_proj",
    )(x, x, mod, mix, vecs, *weights)


def _rwkv_core_kernel(r_ref, k_ref, v_ref, kk_ref, a_ref, lw_ref, g_ref, vec_ref, vec_prev_ref,
                      o_ref, s_ref, rbar_s, y0_s, mlrt_s, s0_s, wl_s, g_s, bg_s, *, tiles_per_seq):
    npg = r_ref.shape[1]
    tt = r_ref.shape[2]
    L = CHUNK
    nch = tt // L
    nunit = npg * nch
    step = pl.program_id(0)

    @pl.when(step == 0)
    def _():
        for ref in (s_ref, rbar_s, y0_s, mlrt_s, s0_s, wl_s, g_s, bg_s):
            ref[...] = jnp.zeros_like(ref)

    fresh = (jnp.maximum(step - 1, 0) % tiles_per_seq) == 0

    lane = lax.broadcasted_iota(jnp.int32, (L, LANES), 1)
    m0 = lane < RW_HEAD
    rowl = lax.broadcasted_iota(jnp.int32, (L, LANES), 0)
    strict = rowl > (lane % L)
    incl = rowl >= (lane % L)
    eye_cat = (rowl == (lane % L)).astype(F32)
    row2 = lax.broadcasted_iota(jnp.int32, (LANES, LANES), 0)
    col2 = lax.broadcasted_iota(jnp.int32, (LANES, LANES), 1)
    blockdiag = (row2 // RW_HEAD) == (col2 // RW_HEAD)
    rowt = lax.broadcasted_iota(jnp.int32, (tt, tt), 0)
    colt = lax.broadcasted_iota(jnp.int32, (tt, tt), 1)
    tri = (((rowt // L) == (colt // L)) & (rowt >= colt)).astype(BF16)
    zeros_l = jnp.zeros((L, LANES), BF16)

    def head_sum(z):
        lo = jnp.sum(jnp.where(m0, z, 0.0), axis=-1, keepdims=True)
        hi = jnp.sum(jnp.where(m0, 0.0, z), axis=-1, keepdims=True)
        return jnp.where(m0, lo, hi)

    def split_heads(z):
        zb = z.astype(BF16)
        zero = jnp.zeros_like(zb)
        return jnp.concatenate([jnp.where(m0, zb, zero), jnp.where(m0, zero, zb)], axis=0)

    units = [(p, c) for p in range(npg) for c in range(nch)]

    cums = []
    for p in range(npg):
        hi, lo = _split_bf16(lw_ref[0, p])
        both = _dot(tri, jnp.concatenate([hi, lo], axis=1))
        cums.append(both[:, :LANES] + both[:, LANES:])

    pu = []
    for (p, c) in units:
        rows = slice(c * L, (c + 1) * L)
        r = r_ref[0, p, rows, :].astype(F32)
        k = k_ref[0, p, rows, :].astype(F32)
        v = v_ref[0, p, rows, :]
        kk = kk_ref[0, p, rows, :].astype(F32)
        a = a_ref[0, p, rows, :].astype(F32)
        g = g_ref[0, p, rows, :]
        lw = lw_ref[0, p, rows, :]
        cum = cums[p][rows, :]
        kkn = kk * jnp.minimum(lax.rsqrt(head_sum(kk * kk)), 1e12)
        bvec = kkn * a
        c_last = cum[L - 1:L, :]
        at = -kkn * jnp.exp(cum - lw)
        rt = r * jnp.exp(cum)
        inv_w = jnp.exp(-cum)
        d_last = jnp.exp(c_last - cum)
        lanes = slice(p * LANES, (p + 1) * LANES)
        bonus = head_sum(r * k * vec_ref[0:1, lanes]) * v.astype(F32)
        pu.append(dict(
            rt=rt, v=v, w_last=jnp.exp(c_last), g=g, bg=bonus * g.astype(F32),
            lhs=jnp.concatenate([at, rt], axis=0).astype(BF16),
            rhs=jnp.concatenate([split_heads(bvec * inv_w), split_heads(k * inv_w)], axis=0),
            dk=jnp.concatenate([bvec * d_last, k * d_last], axis=0).astype(BF16),
            at_s=split_heads(at), v_s=split_heads(v),
        ))

    states = [jnp.where(fresh, 0.0, s_ref[p]) for p in range(npg)]
    ys = {}

    def scan_level(c):
        for p in range(npg):
            j = p * nch + c
            s_old = states[p]
            sb = s_old.astype(BF16)
            ys[j] = _dot_nt(rbar_s[j], sb) + y0_s[j]
            states[p] = s_old * wl_s[j] + _dot_nt(sb, mlrt_s[j]) + s0_s[j]

    per_level = -(-nunit // nch)
    for j, u in enumerate(pu):
        if j % per_level == 0:
            scan_level(j // per_level)
        gram = _dot_nt(u["lhs"], u["rhs"])
        u["pw"] = jnp.where(strict, gram[:L, :LANES], 0.0)
        u["r_b"] = jnp.where(incl, gram[L:, :LANES], 0.0).astype(BF16)
        u["ak_rk"] = jnp.concatenate([jnp.where(strict, gram[:L, LANES:], 0.0),
                                      jnp.where(incl, gram[L:, LANES:], 0.0)], axis=0).astype(BF16)
        u["tinv"] = eye_cat + u["pw"]
        u["pw_s"] = split_heads(u["pw"])
    for p in range(npg):
        s_ref[p] = states[p]

    for j in range(nunit):
        p, c = divmod(j, nch)
        lanes = slice(p * LANES, (p + 1) * LANES)
        y = ys[j]
        mu = head_sum(y) * (1.0 / RW_HEAD)
        yc = y - mu
        var = head_sum(yc * yc) * (1.0 / RW_HEAD)
        yn = yc * lax.rsqrt(var + GN_EPS) * vec_prev_ref[1:2, lanes] + vec_prev_ref[2:3, lanes]
        o_ref[0, c * L:(c + 1) * L, lanes] = (yn * g_s[j].astype(F32) + bg_s[j]).astype(o_ref.dtype)

    for u in pu:
        both = _dot(u["ak_rk"], u["v_s"])
        u["akv"] = both[:L]
        u["rkv"] = both[L:]

    for u in pu:
        u["pw"] = _dot(u["pw"].astype(BF16), u["pw_s"])
        u["pw_s"] = split_heads(u["pw"])
    n_levels = L.bit_length() - 1
    for level in range(1, n_levels):
        last = level == n_levels - 1
        for u in pu:
            if last:
                u["tinv"] = u["tinv"] + _dot(u["tinv"].astype(BF16), u["pw_s"])
            else:
                lhs = jnp.concatenate([u["tinv"], u["pw"]], axis=0).astype(BF16)
                both = _dot(lhs, u["pw_s"])
                u["tinv"] = u["tinv"] + both[:L]
                u["pw"] = both[L:]
                u["pw_s"] = split_heads(u["pw"])

    for u in pu:
        xin = jnp.concatenate([u["at_s"], split_heads(u["akv"])], axis=1)
        u["z"] = _dot(u["tinv"].astype(BF16), xin)
    for u in pu:
        z = u["z"]
        rhs = jnp.concatenate([split_heads(z[:, :LANES]), split_heads(z[:, LANES:])], axis=1)
        u["w"] = _dot(u["r_b"], rhs)
    for u in pu:
        bot = jnp.concatenate([zeros_l, u["v"]], axis=1)
        u["ms"] = _dot_tn(u["dk"], jnp.concatenate([u["z"].astype(BF16), bot], axis=0))

    for j, u in enumerate(pu):
        rbar_s[j] = (u["rt"] + u["w"][:, :LANES]).astype(BF16)
        y0_s[j] = u["w"][:, LANES:] + u["rkv"]
        mlrt_s[j] = jnp.where(blockdiag, u["ms"][:, :LANES], 0.0).astype(BF16)
        s0_s[j] = jnp.where(blockdiag, u["ms"][:, LANES:].T, 0.0)
        wl_s[j] = u["w_last"]
        g_s[j] = u["g"]
        bg_s[j] = u["bg"]


def _rwkv_core(proj, vecs, b, s, d, tt, npg):
    assert d % (2 * RW_HEAD) == 0 and (d // LANES) % npg == 0 and s % tt == 0 and tt % CHUNK == 0
    npair = d // LANES
    ngrp = npair // npg
    nt = s // tt
    ntiles = b * ngrp * nt
    nunit = npg * (tt // CHUNK)

    def tile_coords(tile):
        return tile // (ngrp * nt), (tile // nt) % ngrp, tile % nt

    def in_map(i):
        bi, gi, ti = tile_coords(jnp.minimum(i, ntiles - 1))
        return (bi, gi, ti, 0)

    def out_map(i):
        bi, gi, ti = tile_coords(jnp.maximum(i - 1, 0))
        return (bi, ti, gi)

    in_spec = pl.BlockSpec((1, npg, tt, LANES), in_map)
    vec_spec = pl.BlockSpec((8, npg * LANES), lambda i: (0, tile_coords(jnp.minimum(i, ntiles - 1))[1]))
    vec_prev_spec = pl.BlockSpec((8, npg * LANES), lambda i: (0, tile_coords(jnp.maximum(i - 1, 0))[1]))
    L = CHUNK
    return pl.pallas_call(
        functools.partial(_rwkv_core_kernel, tiles_per_seq=nt),
        grid=(ntiles + 1,),
        in_specs=[in_spec] * 7 + [vec_spec, vec_prev_spec],
        out_specs=pl.BlockSpec((1, tt, npg * LANES), out_map),
        out_shape=jax.ShapeDtypeStruct((b, s, d), BF16),
        scratch_shapes=[
            pltpu.VMEM((npg, LANES, LANES), F32),
            pltpu.VMEM((nunit, L, LANES), BF16),
            pltpu.VMEM((nunit, L, LANES), F32),
            pltpu.VMEM((nunit, LANES, LANES), BF16),
            pltpu.VMEM((nunit, LANES, LANES), F32),
            pltpu.VMEM((nunit, 1, LANES), F32),
            pltpu.VMEM((nunit, L, LANES), BF16),
            pltpu.VMEM((nunit, L, LANES), F32),
        ],
        compiler_params=_params(("arbitrary",)),
        name="rwkv_core",
    )(*proj, vecs, vecs)


def _post_ffn_kernel(x_ref, y_ref, mod_ref, fw_ref, wo_ref, wgu_ref, wd_ref, o_ref, act_ref,
                     *, final):
    g1 = mod_ref[0, 2:3, :]
    sh2 = mod_ref[0, 3:4, :]
    sc2 = mod_ref[0, 4:5, :]
    g2 = mod_ref[0, 5:6, :]
    x1 = x_ref[0] + g1 * _dot(y_ref[0], wo_ref[...])
    h = _rms_mod(x1, sc2, sh2).astype(BF16)
    d_ff = act_ref.shape[1]
    for c0 in range(0, d_ff, COL_BLOCK):
        cols = slice(c0, c0 + COL_BLOCK)
        gate = _dot(h, wgu_ref[0, :, cols])
        up = _dot(h, wgu_ref[0, :, d_ff + c0:d_ff + c0 + COL_BLOCK])
        act_ref[:, cols] = (gate * _sigmoid(gate) * up).astype(BF16)
    x2 = x1 + g2 * _dot(act_ref[...], wd_ref[0])
    if final:
        inv = lax.rsqrt(jnp.mean(x2 * x2, axis=-1, keepdims=True) + EPS)
        x2 = x2 * inv * fw_ref[0:1, :]
    o_ref[0] = x2


def _post_ffn(x, y, mod, final_w, w_out, w_gu, w_d, layer, tm, final):
    b, s, d = x.shape
    d_ff = w_d.shape[1]
    assert d_ff % COL_BLOCK == 0 and s % tm == 0
    kern = functools.partial(_post_ffn_kernel, final=final)
    tok = lambda i, j: (i, j, 0)
    layer_spec = lambda w: pl.BlockSpec((1,) + w.shape[1:], lambda i, j: (layer, 0, 0),
                                        pipeline_mode=pl.Buffered(1))
    return pl.pallas_call(
        kern,
        grid=(b, s // tm),
        in_specs=[
            pl.BlockSpec((1, tm, d), tok),
            pl.BlockSpec((1, tm, d), tok),
            pl.BlockSpec((1, 8, d), lambda i, j: (i, 0, 0)),
            _const_spec(final_w.shape),
            _const_spec(w_out.shape),
            layer_spec(w_gu),
            layer_spec(w_d),
        ],
        out_specs=pl.BlockSpec((1, tm, d), tok),
        out_shape=jax.ShapeDtypeStruct((b, s, d), F32),
        scratch_shapes=[pltpu.VMEM((tm, d_ff), BF16)],
        compiler_params=_params(("parallel", "parallel")),
        name="post_ffn_final" if final else "post_ffn",
    )(x, y, mod, final_w, w_out, w_gu, w_d)


def _softcap(z):
    return GATE_CAP * jnp.tanh(z * (1.0 / GATE_CAP))


def _mlstm_proj_kernel(x_ref, xp_ref, mod_ref, cv_ref, gb_ref, wqk32_ref, wv32_ref,
                       wo32_ref, wgi_ref, wgf_ref,
                       q_out, k_out, v_out, og_out, gi_out, gf_out, pbuf, wqk_ref, wv_ref, wo_ref):
    tm = x_ref.shape[1]
    t = pl.program_id(1)

    @pl.when((pl.program_id(0) == 0) & (t == 0))
    def _():
        wqk_ref[...] = wqk32_ref[0].astype(BF16)
        wv_ref[...] = wv32_ref[0].astype(BF16)
        wo_ref[...] = wo32_ref[0].astype(BF16)

    sh = mod_ref[0, 0:1, :]
    sc = mod_ref[0, 1:2, :]
    h = _rms_mod(x_ref[0], sc, sh).astype(BF16)
    hp = _rms_mod(xp_ref[0], sc, sh).astype(BF16)

    half = wqk_ref.shape[1] // 2
    for c0 in range(0, wqk_ref.shape[1], COL_BLOCK):
        c = slice(c0, c0 + COL_BLOCK)
        pqk = _dot(h, wqk_ref[:, c])
        pbuf[0:8, c] = jnp.where(t > 0, _dot(hp, wqk_ref[:, c]), 0.0)
        pbuf[8:, c] = pqk
        conv = cv_ref[CONV_W:CONV_W + 1, c] + pqk * cv_ref[CONV_W - 1:CONV_W, c]
        for j in range(1, CONV_W):
            conv = conv + pbuf[pl.ds(8 - j, tm), c] * cv_ref[CONV_W - 1 - j:CONV_W - j, c]
        qk = conv * _sigmoid(conv)
        if c0 < half:
            q_out[0, :, c] = (qk * (ML_QK ** -0.5)).astype(q_out.dtype)
        else:
            k_out[0, :, c0 - half:c0 - half + COL_BLOCK] = qk.astype(k_out.dtype)
    for c0 in range(0, wv_ref.shape[1], COL_BLOCK):
        c = slice(c0, c0 + COL_BLOCK)
        v_out[0, :, c] = _dot(h, wv_ref[:, c]).astype(v_out.dtype)
    for c0 in range(0, wo_ref.shape[1], COL_BLOCK):
        c = slice(c0, c0 + COL_BLOCK)
        og_out[0, :, c] = _sigmoid(_dot(h, wo_ref[:, c])).astype(og_out.dtype)

    lane = lax.broadcasted_iota(jnp.int32, gi_out.shape[1:], 1)
    zi = _dot(h, wgi_ref[...]) + gb_ref[0:1, :]
    zf = _dot(h, wgf_ref[...]) + gb_ref[1:2, :]
    gi_out[0] = jnp.where(lane < ML_HEADS, _softcap(zi), 0.0)
    gf_out[0] = jnp.where(lane < ML_HEADS, -_softplus(-_softcap(zf)), 0.0)


def _mlstm_proj(x, mod, cv, gb, w_in, w_gi, w_gf, tm):
    b, s, d = x.shape
    nqk = ML_HEADS * ML_QK
    nv = ML_HEADS * ML_V
    assert 2 * nqk == nv == d
    assert s % tm == 0
    ng = w_gi.shape[1]
    blk8 = tm // 8
    tok = lambda i, j: (i, j, 0)
    w_block = lambda cb: pl.BlockSpec((1, d, d), lambda i, j: (0, 0, cb), pipeline_mode=pl.Buffered(1))
    return pl.pallas_call(
        _mlstm_proj_kernel,
        grid=(b, s // tm),
        in_specs=[
            pl.BlockSpec((1, tm, d), tok),
            pl.BlockSpec((1, 8, d), lambda i, j: (i, jnp.maximum(j * blk8 - 1, 0), 0)),
            pl.BlockSpec((1, 8, d), lambda i, j: (i, 0, 0)),
            _const_spec(cv.shape),
            _const_spec(gb.shape),
            w_block(0), w_block(1), w_block(2),
            _const_spec(w_gi.shape),
            _const_spec(w_gf.shape),
        ],
        out_specs=[
            pl.BlockSpec((1, tm, nqk), tok),
            pl.BlockSpec((1, tm, nqk), tok),
            pl.BlockSpec((1, tm, nv), tok),
            pl.BlockSpec((1, tm, nv), tok),
            pl.BlockSpec((1, tm, ng), tok),
            pl.BlockSpec((1, tm, ng), tok),
        ],
        out_shape=[
            jax.ShapeDtypeStruct((b, s, nqk), BF16),
            jax.ShapeDtypeStruct((b, s, nqk), BF16),
            jax.ShapeDtypeStruct((b, s, nv), BF16),
            jax.ShapeDtypeStruct((b, s, nv), BF16),
            jax.ShapeDtypeStruct((b, s, ng), F32),
            jax.ShapeDtypeStruct((b, s, ng), F32),
        ],
        scratch_shapes=[pltpu.VMEM((tm + 8, d), F32),
                        pltpu.VMEM((d, d), BF16), pltpu.VMEM((d, d), BF16), pltpu.VMEM((d, d), BF16)],
        compiler_params=_params(("arbitrary", "arbitrary")),
        name="mlstm_proj",
    )(x, x, mod, cv, gb, w_in, w_in, w_in, w_gi, w_gf)


def _split3_bf16(z):
    hi = z.astype(BF16)
    r1 = z - hi.astype(F32)
    mid = r1.astype(BF16)
    lo = (r1 - mid.astype(F32)).astype(BF16)
    return jnp.concatenate([hi, mid, lo], axis=1)


def _mlstm_core_kernel(q_ref, k_ref, v_ref, og_ref, gi_ref, gf_ref, hn_ref, e64_ref, e128_ref,
                       o_ref, c_ref, m_ref):
    tt = q_ref.shape[1]
    L = CHUNK
    nch = tt // L
    npair = ML_HEADS // 2
    t = pl.program_id(1)

    @pl.when(t == 0)
    def _():
        c_ref[...] = jnp.zeros_like(c_ref)
        m_ref[...] = jnp.zeros_like(m_ref)

    lane = lax.broadcasted_iota(jnp.int32, (L, LANES), 1)
    rowl = lax.broadcasted_iota(jnp.int32, (L, LANES), 0)
    m0 = lane < ML_QK
    src = lane % L
    rowt = lax.broadcasted_iota(jnp.int32, (tt, tt), 0)
    colt = lax.broadcasted_iota(jnp.int32, (tt, tt), 1)
    tri = (((rowt // L) == (colt // L)) & (rowt >= colt)).astype(BF16)
    row_in_chunk = lax.broadcasted_iota(jnp.int32, (tt, LANES), 0) % L
    row_first = lax.broadcasted_iota(jnp.int32, (LANES, 1), 0) < ML_QK
    ones = jnp.ones((L, LANES), BF16)

    def split_heads(zv):
        zb = zv.astype(BF16)
        zero = jnp.zeros_like(zb)
        return jnp.concatenate([jnp.where(m0, zb, zero), jnp.where(m0, zero, zb)], axis=0)

    gi = gi_ref[0]
    b3 = _dot(tri, _split3_bf16(gf_ref[0]))
    bcum = b3[:, :LANES] + b3[:, LANES:2 * LANES] + b3[:, 2 * LANES:]
    z = gi - bcum
    cmax = z
    shift = 1
    while shift < L:
        cmax = jnp.where(row_in_chunk >= shift, jnp.maximum(cmax, pltpu.roll(cmax, shift, 0)), cmax)
        shift *= 2
    m_intra = bcum + cmax

    m_prev = m_ref[0:1, :]
    mp_rows, mn_rows, gt_rows, s_a = [], [], [], []
    for c in range(nch):
        last = c * L + L - 1
        g_tot = bcum[last:last + 1, :]
        m_loc = g_tot + cmax[last:last + 1, :]
        m_new = jnp.maximum(g_tot + m_prev, m_loc)
        mp_rows.append(jnp.broadcast_to(m_prev, (L, LANES)))
        mn_rows.append(jnp.broadcast_to(m_new, (L, LANES)))
        gt_rows.append(jnp.broadcast_to(g_tot, (L, LANES)))
        s_a.append(jnp.exp(g_tot + m_prev - m_new))
        m_prev = m_new
    m_ref[0:1, :] = m_prev
    inter = bcum + jnp.concatenate(mp_rows, axis=0)
    m_t = jnp.maximum(inter, m_intra)
    w_inter = jnp.exp(inter - m_t)
    kws = jnp.exp(jnp.concatenate(gt_rows, axis=0) + z - jnp.concatenate(mn_rows, axis=0))

    head_lane = lax.broadcasted_iota(jnp.int32, (tt, LANES), 1) < ML_HEADS

    def pack3(val):
        val = jnp.where(head_lane, val, 0.0)
        hi = val.astype(BF16).astype(F32)
        r1 = val - hi
        mid = r1.astype(BF16).astype(F32)
        lo = (r1 - mid).astype(BF16).astype(F32)
        packed = hi + pltpu.roll(mid, ML_HEADS, 1) + pltpu.roll(lo, 2 * ML_HEADS, 1)
        return packed.astype(BF16)

    spread = _dot(jnp.concatenate([pack3(bcum - m_t), pack3(z), pack3(w_inter), pack3(kws)], axis=0),
                  e64_ref[...])
    a_exp, z_exp, wi_exp, kw_exp = (spread[i * tt:(i + 1) * tt] for i in range(4))
    em_exp = _dot(pack3(jnp.exp(-m_t)), e128_ref[...])

    un = []
    for c in range(nch):
        rows = slice(c * L, (c + 1) * L)
        for p in range(npair):
            lanes = slice(p * LANES, (p + 1) * LANES)
            qp = q_ref[0, rows, lanes]
            kp = k_ref[0, rows, lanes]
            v0 = v_ref[0, rows, 2 * p * ML_V:(2 * p + 1) * ML_V]
            v1 = v_ref[0, rows, (2 * p + 1) * ML_V:(2 * p + 2) * ML_V]
            z_src = jnp.sum(jnp.where(rowl == src, z_exp[rows, lanes], 0.0), axis=0, keepdims=True)
            un.append(dict(
                c=c, p=p, rows=rows, qp=qp,
                k2=split_heads(kp),
                expd=jnp.where(rowl >= src, jnp.exp(a_exp[rows, lanes] + z_src), 0.0),
                qs=split_heads(qp * wi_exp[rows, lanes]),
                kw=split_heads(kp * kw_exp[rows, lanes]),
                vst=jnp.concatenate([jnp.concatenate([v0, ones], axis=1),
                                     jnp.concatenate([v1, ones], axis=1)], axis=0),
            ))

    for u in un:
        scores = _dot_nt(u["qp"], u["k2"])
        u["w_intra"] = split_heads(u["expd"] * scores)
    for u in un:
        u["c_loc"] = _dot_tn(u["kw"], u["vst"])

    states = [c_ref[p] for p in range(npair)]
    for u in un:
        c, p = u["c"], u["p"]
        u["c_old"] = states[p].astype(BF16)
        sa = jnp.where(row_first, s_a[c][:, 2 * p:2 * p + 1], s_a[c][:, 2 * p + 1:2 * p + 2])
        states[p] = sa * states[p] + u["c_loc"]
    for p in range(npair):
        c_ref[p] = states[p]

    for u in un:
        u["nd"] = _dot(jnp.concatenate([u["qs"], u["w_intra"]], axis=1),
                       jnp.concatenate([u["c_old"], u["vst"]], axis=0))

    for u in un:
        rows = u["rows"]
        for hh in range(2):
            head = 2 * u["p"] + hh
            cols = slice(head * ML_V, (head + 1) * ML_V)
            num = u["nd"][hh * L:(hh + 1) * L, :LANES]
            den = u["nd"][hh * L:(hh + 1) * L, LANES:]
            den = jnp.maximum(jnp.abs(den), em_exp[rows, cols])
            hv = num / den
            hv = hv * lax.rsqrt(jnp.mean(hv * hv, axis=-1, keepdims=True) + EPS)
            hv = hv * hn_ref[0:1, cols]
            o_ref[0, rows, cols] = (hv * og_ref[0, rows, cols]).astype(o_ref.dtype)


def _expand_matrix(width):
    rows = jnp.arange(LANES)[:, None]
    cols = jnp.arange(ML_HEADS * width)[None, :]
    return ((rows < 3 * ML_HEADS) & (rows % ML_HEADS == cols // width)).astype(BF16)


def _mlstm_core(q, k, v, og, gi, gf, hn_w, tt):
    b, s, dv = v.shape
    assert s % tt == 0 and tt % CHUNK == 0 and dv == ML_HEADS * ML_V
    nqk = q.shape[2]
    ng = gi.shape[2]
    e64 = _expand_matrix(ML_QK)
    e128 = _expand_matrix(ML_V)
    tok = lambda i, j: (i, j, 0)
    return pl.pallas_call(
        _mlstm_core_kernel,
        grid=(b, s // tt),
        in_specs=[
            pl.BlockSpec((1, tt, nqk), tok),
            pl.BlockSpec((1, tt, nqk), tok),
            pl.BlockSpec((1, tt, dv), tok),
            pl.BlockSpec((1, tt, dv), tok),
            pl.BlockSpec((1, tt, ng), tok),
            pl.BlockSpec((1, tt, ng), tok),
            _const_spec(hn_w.shape),
            _const_spec(e64.shape),
            _const_spec(e128.shape),
        ],
        out_specs=pl.BlockSpec((1, tt, dv), tok),
        out_shape=jax.ShapeDtypeStruct((b, s, dv), BF16),
        scratch_shapes=[
            pltpu.VMEM((ML_HEADS // 2, LANES, 2 * LANES), F32),
            pltpu.VMEM((8, LANES), F32),
        ],
        compiler_params=_params(("parallel", "arbitrary")),
        name="mlstm_core",
    )(q, k, v, og, gi, gf, hn_w, e64, e128)


def _pad_rows(a, rows=8):
    return jnp.pad(a, ((0, rows - a.shape[0]), (0, 0)))


def kernel(x, c, ada_w, ada_b, rw_mix, rw_w_in, rw_w0, rw_w2, rw_a0, rw_a2, rw_g2, rw_k_k, rw_k_a, rw_r_k, rw_gn_w, rw_gn_b, rw_w_out, ml_w_in, ml_conv_w, ml_conv_b, ml_b_i, ml_b_f, ml_hn_w, ml_w_out, ffn_w_gu, ffn_w_down, final_w):
    b, s, d = x.shape
    bf = lambda w: w.astype(BF16)
    final_w2 = final_w.reshape(1, d)

    mod = _adaln_mod(c, ada_w, ada_b)

    w_in = rw_w_in[0]
    o1 = d
    o2 = o1 + LORA_W
    o3 = o2 + d
    o4 = o3 + d
    o5 = o4 + LORA_A
    rw_weights = [bf(w_in[:, :o1]), bf(w_in[:, o2:o3]), bf(w_in[:, o3:o4]),
                  bf(w_in[:, o1:o2]), bf(w_in[:, o4:o5]), bf(w_in[:, o5:]),
                  bf(rw_w2[0]), bf(rw_a2[0]), bf(rw_g2[0])]
    mix = _pad_rows(rw_mix[0])
    vecs = _pad_rows(jnp.stack([rw_w0[0], rw_a0[0], rw_k_k[0], rw_k_a[0]]))
    proj = _rwkv_proj(x, mod[0], mix, vecs, rw_weights, TM_DENSE)
    core_vecs = _pad_rows(jnp.stack([rw_r_k[0].reshape(d), rw_gn_w[0], rw_gn_b[0]]))
    y = _rwkv_core(proj, core_vecs, b, s, d, TT_RWKV, PAIRS_RWKV)
    w_gu = bf(ffn_w_gu)
    w_down = bf(ffn_w_down)
    x = _post_ffn(x, y, mod[0], final_w2, bf(rw_w_out[0]), w_gu, w_down, 0, TM_DENSE, False)

    nv = ML_HEADS * ML_V
    pad_lanes = lambda w: jnp.pad(w, ((0, 0), (0, LANES - w.shape[1])))
    o_g = 2 * ML_HEADS * ML_QK + 2 * nv
    w_gi = bf(pad_lanes(ml_w_in[0][:, o_g:o_g + ML_HEADS]))
    w_gf = bf(pad_lanes(ml_w_in[0][:, o_g + ML_HEADS:]))
    cv = _pad_rows(jnp.concatenate([ml_conv_w[0], ml_conv_b[0][None, :]], axis=0))
    gb = _pad_rows(pad_lanes(jnp.stack([ml_b_i[0], ml_b_f[0]])))
    q, k, v, og, gi, gf = _mlstm_proj(x, mod[1], cv, gb, ml_w_in, w_gi, w_gf, TM_DENSE)
    y = _mlstm_core(q, k, v, og, gi, gf, ml_hn_w[0].reshape(1, nv), TT_MLSTM)
    x = _post_ffn(x, y, mod[1], final_w2, bf(ml_w_out[0]), w_gu, w_down, 1, TM_DENSE, True)
    return x
```

```python
import functools

import jax
import jax.numpy as jnp
from jax import lax
from jax.experimental import pallas as pl
from jax.experimental.pallas import tpu as pltpu

F32 = jnp.float32
BF16 = jnp.bfloat16

EPS = 1e-6
GN_EPS = 64e-5
GATE_CAP = 15.0
DECAY_SCALE = 0.6065306597126334
CHUNK = 64
LANES = 128
RW_HEAD = 64
LORA_W = 64
LORA_A = 64
LORA_G = 160
ML_HEADS = 8
ML_QK = 64
ML_V = 128
CONV_W = 4
COL_BLOCK = 256
VMEM_LIMIT = 56 * 1024 * 1024
TM_DENSE = 512
TM_FFN = 1024
TT_RWKV = 256
PAIRS_RWKV = 8
TT_MLSTM = 512

NT_DIMS = (((1,), (1,)), ((), ()))
TN_DIMS = (((0,), (0,)), ((), ()))


def _dot(a, b):
    return jnp.dot(a, b, preferred_element_type=F32)


def _dot_nt(a, b):
    return lax.dot_general(a, b, NT_DIMS, preferred_element_type=F32)


def _dot_tn(a, b):
    return lax.dot_general(a, b, TN_DIMS, preferred_element_type=F32)


def _sigmoid(z):
    return 0.5 * jnp.tanh(0.5 * z) + 0.5


def _softplus(z):
    return jnp.maximum(z, 0.0) + jnp.log(1.0 + jnp.exp(-jnp.abs(z)))


def _rms_mod(xv, sc, sh):
    inv = lax.rsqrt(jnp.mean(xv * xv, axis=-1, keepdims=True) + EPS)
    return xv * inv * (1.0 + sc) + sh


def _split_bf16(z):
    hi = z.astype(BF16)
    lo = (z - hi.astype(F32)).astype(BF16)
    return hi, lo


def _params(sem, flags=None):
    return pltpu.CompilerParams(dimension_semantics=sem, vmem_limit_bytes=VMEM_LIMIT, flags=flags)


def _const_spec(shape):
    nd = len(shape)
    return pl.BlockSpec(shape, lambda *_: (0,) * nd, pipeline_mode=pl.Buffered(1))


def _mod_kernel(c_ref, w_ref, b_ref, o_ref):
    c = c_ref[...]
    ca = (c * _sigmoid(c)).astype(BF16)
    o_ref[0] = _dot(ca, w_ref[0].astype(BF16)) + b_ref[0]


def _adaln_mod(c, ada_w, ada_b):
    depth, d, d6 = ada_w.shape
    b = c.shape[0]
    bp = 8
    c_pad = jnp.zeros((bp, d), F32).at[:b].set(c)
    nblk = d6 // d
    out = pl.pallas_call(
        _mod_kernel,
        grid=(depth, nblk),
        in_specs=[
            pl.BlockSpec((bp, d), lambda i, j: (0, 0)),
            pl.BlockSpec((1, d, d), lambda i, j: (i, 0, j)),
            pl.BlockSpec((1, 1, d), lambda i, j: (i, 0, j)),
        ],
        out_specs=pl.BlockSpec((1, bp, d), lambda i, j: (i, 0, j)),
        out_shape=jax.ShapeDtypeStruct((depth, bp, d6), F32),
        compiler_params=_params(("arbitrary", "arbitrary")),
        name="adaln_mod",
    )(c_pad, ada_w, ada_b.reshape(depth, 1, d6))
    mod = out[:, :b].reshape(depth, b, nblk, d)
    return jnp.pad(mod, ((0, 0), (0, 0), (0, 8 - nblk), (0, 0)))


def _rwkv_proj_kernel(x_ref, xp_ref, mod_ref, mix_ref, vec_ref, wr_ref, wk_ref, wv_ref,
                      wwl_ref, wal_ref, wgl_ref, w2_ref, a2_ref, g2_ref,
                      r_out, k_out, v_out, kk_out, a_out, lw_out, g_out, hbuf, xbuf):
    tm = x_ref.shape[1]
    t = pl.program_id(1)
    sh = mod_ref[0, 0:1, :]
    sc = mod_ref[0, 1:2, :]
    hp = _rms_mod(xp_ref[0], sc, sh)
    hbuf[0:8, :] = jnp.where(t > 0, hp, 0.0)
    w0 = vec_ref[0:1, :]
    a0 = vec_ref[1:2, :]
    k_k = vec_ref[2:3, :]
    k_a = vec_ref[3:4, :]

    h = _rms_mod(x_ref[0], sc, sh)
    hbuf[8:, :] = h
    xbuf[...] = hbuf[pl.ds(7, tm), :] - h

    def mixed(i):
        return (hbuf[8:, :] + xbuf[...] * mix_ref[i:i + 1, :]).astype(BF16)

    def put(out, val, cb):
        for q in range(COL_BLOCK // LANES):
            out[0, cb * (COL_BLOCK // LANES) + q] = val[:, q * LANES:(q + 1) * LANES].astype(out.dtype)

    ncb = r_out.shape[1] * LANES // COL_BLOCK
    col = lambda cb: slice(cb * COL_BLOCK, (cb + 1) * COL_BLOCK)
    lhs = mixed(0)
    for cb in range(ncb):
        put(r_out, _dot(lhs, wr_ref[:, col(cb)]), cb)
    lhs = mixed(3)
    for cb in range(ncb):
        put(v_out, _dot(lhs, wv_ref[:, col(cb)]), cb)
    twl = jnp.tanh(_dot(mixed(1), wwl_ref[...])).astype(BF16)
    alb = _dot(mixed(4), wal_ref[...]).astype(BF16)
    sgl = _sigmoid(_dot(mixed(5), wgl_ref[...])).astype(BF16)
    a_blocks = []
    for cb in range(ncb):
        c = col(cb)
        put(lw_out, -DECAY_SCALE * _sigmoid(w0[:, c] + _dot(twl, w2_ref[:, c])), cb)
        a = _sigmoid(a0[:, c] + _dot(alb, a2_ref[:, c]))
        put(a_out, a, cb)
        a_blocks.append(a)
        put(g_out, _dot(sgl, g2_ref[:, c]), cb)
    lhs = mixed(2)
    for cb in range(ncb):
        c = col(cb)
        k = _dot(lhs, wk_ref[:, c])
        put(kk_out, k * k_k[:, c], cb)
        put(k_out, k * (1.0 + (a_blocks[cb] - 1.0) * k_a[:, c]), cb)


def _rwkv_proj(x, mod, mix, vecs, weights, tm):
    b, s, d = x.shape
    assert s % tm == 0 and d % COL_BLOCK == 0
    npair = d // LANES
    nt = s // tm
    blk8 = tm // 8
    pair_shape = lambda dt: jax.ShapeDtypeStruct((b, npair, s, LANES), dt)
    pair_spec = pl.BlockSpec((1, npair, tm, LANES), lambda i, j: (i, 0, j, 0))
    return pl.pallas_call(
        _rwkv_proj_kernel,
        grid=(b, nt),
        in_specs=[
            pl.BlockSpec((1, tm, d), lambda i, j: (i, j, 0)),
            pl.BlockSpec((1, 8, d), lambda i, j: (i, jnp.maximum(j * blk8 - 1, 0), 0)),
            pl.BlockSpec((1, 8, d), lambda i, j: (i, 0, 0)),
            _const_spec(mix.shape),
            _const_spec(vecs.shape),
        ] + [_const_spec(w.shape) for w in weights],
        out_specs=[pair_spec] * 7,
        out_shape=[pair_shape(BF16)] * 5 + [pair_shape(F32), pair_shape(BF16)],
        scratch_shapes=[pltpu.VMEM((tm + 8, d), F32), pltpu.VMEM((tm, d), F32)],
        compiler_params=_params(("parallel", "arbitrary")),
        name="rwkv_proj",
    )(x, x, mod, mix, vecs, *weights)


def _rwkv_core_kernel(r_ref, k_ref, v_ref, kk_ref, a_ref, lw_ref, g_ref, vec_ref, vec_prev_ref,
                      o_ref, s_ref, rbar_s, y0_s, mlrt_s, s0_s, wl_s, g_s, bg_s, *, tiles_per_seq):
    npg = r_ref.shape[1]
    tt = r_ref.shape[2]
    L = CHUNK
    nch = tt // L
    nunit = npg * nch
    step = pl.program_id(0)

    @pl.when(step == 0)
    def _():
        for ref in (s_ref, rbar_s, y0_s, mlrt_s, s0_s, wl_s, g_s, bg_s):
            ref[...] = jnp.zeros_like(ref)

    fresh = (jnp.maximum(step - 1, 0) % tiles_per_seq) == 0

    lane = lax.broadcasted_iota(jnp.int32, (L, LANES), 1)
    m0 = lane < RW_HEAD
    rowl = lax.broadcasted_iota(jnp.int32, (L, LANES), 0)
    strict = rowl > (lane % L)
    incl = rowl >= (lane % L)
    eye_cat = (rowl == (lane % L)).astype(F32)
    row2 = lax.broadcasted_iota(jnp.int32, (LANES, LANES), 0)
    col2 = lax.broadcasted_iota(jnp.int32, (LANES, LANES), 1)
    blockdiag = (row2 // RW_HEAD) == (col2 // RW_HEAD)
    rowt = lax.broadcasted_iota(jnp.int32, (tt, tt), 0)
    colt = lax.broadcasted_iota(jnp.int32, (tt, tt), 1)
    tri = (((rowt // L) == (colt // L)) & (rowt >= colt)).astype(BF16)
    zeros_l = jnp.zeros((L, LANES), BF16)

    def head_sum(z):
        lo = jnp.sum(jnp.where(m0, z, 0.0), axis=-1, keepdims=True)
        hi = jnp.sum(jnp.where(m0, 0.0, z), axis=-1, keepdims=True)
        return jnp.where(m0, lo, hi)

    def split_heads(z):
        zb = z.astype(BF16)
        zero = jnp.zeros_like(zb)
        return jnp.concatenate([jnp.where(m0, zb, zero), jnp.where(m0, zero, zb)], axis=0)

    units = [(p, c) for p in range(npg) for c in range(nch)]

    cums = []
    for p in range(npg):
        hi, lo = _split_bf16(lw_ref[0, p])
        both = _dot(tri, jnp.concatenate([hi, lo], axis=1))
        cums.append(both[:, :LANES] + both[:, LANES:])

    pu = []
    for (p, c) in units:
        rows = slice(c * L, (c + 1) * L)
        r = r_ref[0, p, rows, :].astype(F32)
        k = k_ref[0, p, rows, :].astype(F32)
        v = v_ref[0, p, rows, :]
        kk = kk_ref[0, p, rows, :].astype(F32)
        a = a_ref[0, p, rows, :].astype(F32)
        g = g_ref[0, p, rows, :]
        lw = lw_ref[0, p, rows, :]
        cum = cums[p][rows, :]
        kkn = kk * jnp.minimum(lax.rsqrt(head_sum(kk * kk)), 1e12)
        bvec = kkn * a
        c_last = cum[L - 1:L, :]
        at = -kkn * jnp.exp(cum - lw)
        rt = r * jnp.exp(cum)
        inv_w = jnp.exp(-cum)
        d_last = jnp.exp(c_last - cum)
        lanes = slice(p * LANES, (p + 1) * LANES)
        bonus = head_sum(r * k * vec_ref[0:1, lanes]) * v.astype(F32)
        pu.append(dict(
            rt=rt, v=v, w_last=jnp.exp(c_last), g=g, bg=bonus * g.astype(F32),
            lhs=jnp.concatenate([at, rt], axis=0).astype(BF16),
            rhs=jnp.concatenate([split_heads(bvec * inv_w), split_heads(k * inv_w)], axis=0),
            dk=jnp.concatenate([bvec * d_last, k * d_last], axis=0).astype(BF16),
            at_s=split_heads(at), v_s=split_heads(v),
        ))

    states = [jnp.where(fresh, 0.0, s_ref[p]) for p in range(npg)]
    ys = {}

    def scan_level(c):
        for p in range(npg):
            j = p * nch + c
            s_old = states[p]
            sb = s_old.astype(BF16)
            ys[j] = _dot_nt(rbar_s[j], sb) + y0_s[j]
            states[p] = s_old * wl_s[j] + _dot_nt(sb, mlrt_s[j]) + s0_s[j]

    per_level = -(-nunit // nch)
    for j, u in enumerate(pu):
        if j % per_level == 0:
            scan_level(j // per_level)
        gram = _dot_nt(u["lhs"], u["rhs"])
        u["pw"] = jnp.where(strict, gram[:L, :LANES], 0.0)
        u["r_b"] = jnp.where(incl, gram[L:, :LANES], 0.0).astype(BF16)
        u["ak_rk"] = jnp.concatenate([jnp.where(strict, gram[:L, LANES:], 0.0),
                                      jnp.where(incl, gram[L:, LANES:], 0.0)], axis=0).astype(BF16)
        u["tinv"] = eye_cat + u["pw"]
        u["pw_s"] = split_heads(u["pw"])
    for p in range(npg):
        s_ref[p] = states[p]

    for j in range(nunit):
        p, c = divmod(j, nch)
        lanes = slice(p * LANES, (p + 1) * LANES)
        y = ys[j]
        mu = head_sum(y) * (1.0 / RW_HEAD)
        yc = y - mu
        var = head_sum(yc * yc) * (1.0 / RW_HEAD)
        yn = yc * lax.rsqrt(var + GN_EPS) * vec_prev_ref[1:2, lanes] + vec_prev_ref[2:3, lanes]
        o_ref[0, c * L:(c + 1) * L, lanes] = (yn * g_s[j].astype(F32) + bg_s[j]).astype(o_ref.dtype)

    for u in pu:
        both = _dot(u["ak_rk"], u["v_s"])
        u["akv"] = both[:L]
        u["rkv"] = both[L:]

    for u in pu:
        u["pw"] = _dot(u["pw"].astype(BF16), u["pw_s"])
        u["pw_s"] = split_heads(u["pw"])
    n_levels = L.bit_length() - 1
    for level in range(1, n_levels):
        last = level == n_levels - 1
        for u in pu:
            if last:
                u["tinv"] = u["tinv"] + _dot(u["tinv"].astype(BF16), u["pw_s"])
            else:
                lhs = jnp.concatenate([u["tinv"], u["pw"]], axis=0).astype(BF16)
                both = _dot(lhs, u["pw_s"])
                u["tinv"] = u["tinv"] + both[:L]
                u["pw"] = both[L:]
                u["pw_s"] = split_heads(u["pw"])

    for u in pu:
        xin = jnp.concatenate([u["at_s"], split_heads(u["akv"])], axis=1)
        u["z"] = _dot(u["tinv"].astype(BF16), xin)
    for u in pu:
        z = u["z"]
        rhs = jnp.concatenate([split_heads(z[:, :LANES]), split_heads(z[:, LANES:])], axis=1)
        u["w"] = _dot(u["r_b"], rhs)
    for u in pu:
        bot = jnp.concatenate([zeros_l, u["v"]], axis=1)
        u["ms"] = _dot_tn(u["dk"], jnp.concatenate([u["z"].astype(BF16), bot], axis=0))

    for j, u in enumerate(pu):
        rbar_s[j] = (u["rt"] + u["w"][:, :LANES]).astype(BF16)
        y0_s[j] = u["w"][:, LANES:] + u["rkv"]
        mlrt_s[j] = jnp.where(blockdiag, u["ms"][:, :LANES], 0.0).astype(BF16)
        s0_s[j] = jnp.where(blockdiag, u["ms"][:, LANES:].T, 0.0)
        wl_s[j] = u["w_last"]
        g_s[j] = u["g"]
        bg_s[j] = u["bg"]


def _rwkv_core(proj, vecs, b, s, d, tt, npg):
    assert d % (2 * RW_HEAD) == 0 and (d // LANES) % npg == 0 and s % tt == 0 and tt % CHUNK == 0
    npair = d // LANES
    ngrp = npair // npg
    nt = s // tt
    ntiles = b * ngrp * nt
    nunit = npg * (tt // CHUNK)

    def tile_coords(tile):
        return tile // (ngrp * nt), (tile // nt) % ngrp, tile % nt

    def in_map(i):
        bi, gi, ti = tile_coords(jnp.minimum(i, ntiles - 1))
        return (bi, gi, ti, 0)

    def out_map(i):
        bi, gi, ti = tile_coords(jnp.maximum(i - 1, 0))
        return (bi, ti, gi)

    in_spec = pl.BlockSpec((1, npg, tt, LANES), in_map)
    vec_spec = pl.BlockSpec((8, npg * LANES), lambda i: (0, tile_coords(jnp.minimum(i, ntiles - 1))[1]))
    vec_prev_spec = pl.BlockSpec((8, npg * LANES), lambda i: (0, tile_coords(jnp.maximum(i - 1, 0))[1]))
    L = CHUNK
    return pl.pallas_call(
        functools.partial(_rwkv_core_kernel, tiles_per_seq=nt),
        grid=(ntiles + 1,),
        in_specs=[in_spec] * 7 + [vec_spec, vec_prev_spec],
        out_specs=pl.BlockSpec((1, tt, npg * LANES), out_map),
        out_shape=jax.ShapeDtypeStruct((b, s, d), BF16),
        scratch_shapes=[
            pltpu.VMEM((npg, LANES, LANES), F32),
            pltpu.VMEM((nunit, L, LANES), BF16),
            pltpu.VMEM((nunit, L, LANES), F32),
            pltpu.VMEM((nunit, LANES, LANES), BF16),
            pltpu.VMEM((nunit, LANES, LANES), F32),
            pltpu.VMEM((nunit, 1, LANES), F32),
            pltpu.VMEM((nunit, L, LANES), BF16),
            pltpu.VMEM((nunit, L, LANES), F32),
        ],
        compiler_params=_params(("arbitrary",)),
        name="rwkv_core",
    )(*proj, vecs, vecs)


def _post_ffn_kernel(x_ref, y_ref, mod_ref, fw_ref, wo_ref, wgu_ref, wd_ref, o_ref, act_ref,
                     *, final):
    g1 = mod_ref[0, 2:3, :]
    sh2 = mod_ref[0, 3:4, :]
    sc2 = mod_ref[0, 4:5, :]
    g2 = mod_ref[0, 5:6, :]
    x1 = x_ref[0] + g1 * _dot(y_ref[0], wo_ref[...])
    h = _rms_mod(x1, sc2, sh2).astype(BF16)
    d_ff = act_ref.shape[1]
    for c0 in range(0, d_ff, COL_BLOCK):
        cols = slice(c0, c0 + COL_BLOCK)
        gate = _dot(h, wgu_ref[0, :, cols])
        up = _dot(h, wgu_ref[0, :, d_ff + c0:d_ff + c0 + COL_BLOCK])
        act_ref[:, cols] = (gate * _sigmoid(gate) * up).astype(BF16)
    x2 = x1 + g2 * _dot(act_ref[...], wd_ref[0])
    if final:
        inv = lax.rsqrt(jnp.mean(x2 * x2, axis=-1, keepdims=True) + EPS)
        x2 = x2 * inv * fw_ref[0:1, :]
    o_ref[0] = x2


def _post_ffn(x, y, mod, final_w, w_out, w_gu, w_d, layer, tm, final):
    b, s, d = x.shape
    d_ff = w_d.shape[1]
    assert d_ff % COL_BLOCK == 0 and s % tm == 0
    kern = functools.partial(_post_ffn_kernel, final=final)
    tok = lambda i, j: (i, j, 0)
    layer_spec = lambda w: pl.BlockSpec((1,) + w.shape[1:], lambda i, j: (layer, 0, 0),
                                        pipeline_mode=pl.Buffered(1))
    return pl.pallas_call(
        kern,
        grid=(b, s // tm),
        in_specs=[
            pl.BlockSpec((1, tm, d), tok),
            pl.BlockSpec((1, tm, d), tok),
            pl.BlockSpec((1, 8, d), lambda i, j: (i, 0, 0)),
            _const_spec(final_w.shape),
            _const_spec(w_out.shape),
            layer_spec(w_gu),
            layer_spec(w_d),
        ],
        out_specs=pl.BlockSpec((1, tm, d), tok),
        out_shape=jax.ShapeDtypeStruct((b, s, d), F32),
        scratch_shapes=[pltpu.VMEM((tm, d_ff), BF16)],
        compiler_params=_params(("parallel", "parallel")),
        name="post_ffn_final" if final else "post_ffn",
    )(x, y, mod, final_w, w_out, w_gu, w_d)


def _softcap(z):
    return GATE_CAP * jnp.tanh(z * (1.0 / GATE_CAP))


def _mlstm_proj_kernel(x_ref, xp_ref, mod_ref, cv_ref, gb_ref, wqk_ref, wv_ref,
                       wo_ref, wgi_ref, wgf_ref,
                       q_out, k_out, v_out, og_out, gi_out, gf_out):
    tm = x_ref.shape[1]
    t = pl.program_id(1)
    sh = mod_ref[0, 0:1, :]
    sc = mod_ref[0, 1:2, :]
    h = _rms_mod(x_ref[0], sc, sh).astype(BF16)
    hp = _rms_mod(xp_ref[0], sc, sh).astype(BF16)

    half = wqk_ref.shape[1] // 2
    first_row = lax.broadcasted_iota(jnp.int32, (tm, COL_BLOCK), 0) == 0
    for c0 in range(0, wqk_ref.shape[1], COL_BLOCK):
        c = slice(c0, c0 + COL_BLOCK)
        pqk = _dot(h, wqk_ref[:, c])
        ppv = jnp.where(t > 0, _dot(hp, wqk_ref[:, c]), 0.0)
        acc = pqk * cv_ref[0:1, c]
        acc_prev = ppv * cv_ref[0:1, c]
        for j in range(1, CONV_W):
            wj = cv_ref[j:j + 1, c]
            carry = acc_prev[7:8, :]
            acc = pqk * wj + jnp.where(first_row, carry, pltpu.roll(acc, 1, 0))
            acc_prev = ppv * wj + pltpu.roll(acc_prev, 1, 0)
        conv = acc + cv_ref[CONV_W:CONV_W + 1, c]
        qk = conv * _sigmoid(conv)
        if c0 < half:
            q_out[0, :, c] = (qk * (ML_QK ** -0.5)).astype(q_out.dtype)
        else:
            k_out[0, :, c0 - half:c0 - half + COL_BLOCK] = qk.astype(k_out.dtype)
    for c0 in range(0, wv_ref.shape[1], COL_BLOCK):
        c = slice(c0, c0 + COL_BLOCK)
        v_out[0, :, c] = _dot(h, wv_ref[:, c]).astype(v_out.dtype)
    for c0 in range(0, wo_ref.shape[1], COL_BLOCK):
        c = slice(c0, c0 + COL_BLOCK)
        og_out[0, :, c] = _sigmoid(_dot(h, wo_ref[:, c])).astype(og_out.dtype)

    lane = lax.broadcasted_iota(jnp.int32, gi_out.shape[1:], 1)
    zi = _dot(h, wgi_ref[...]) + gb_ref[0:1, :]
    zf = _dot(h, wgf_ref[...]) + gb_ref[1:2, :]
    gi_out[0] = jnp.where(lane < ML_HEADS, _softcap(zi), 0.0)
    gf_out[0] = jnp.where(lane < ML_HEADS, -_softplus(-_softcap(zf)), 0.0)


def _mlstm_proj(x, mod, cv, gb, weights, tm):
    b, s, d = x.shape
    w_qk, w_v, w_o, w_gi, w_gf = weights
    nqk = w_qk.shape[1] // 2
    nv = w_v.shape[1]
    assert s % tm == 0 and w_qk.shape[1] % COL_BLOCK == 0 and nv % COL_BLOCK == 0
    ng = w_gi.shape[1]
    blk8 = tm // 8
    tok = lambda i, j: (i, j, 0)
    return pl.pallas_call(
        _mlstm_proj_kernel,
        grid=(b, s // tm),
        in_specs=[
            pl.BlockSpec((1, tm, d), tok),
            pl.BlockSpec((1, 8, d), lambda i, j: (i, jnp.maximum(j * blk8 - 1, 0), 0)),
            pl.BlockSpec((1, 8, d), lambda i, j: (i, 0, 0)),
            _const_spec(cv.shape),
            _const_spec(gb.shape),
        ] + [_const_spec(w.shape) for w in weights],
        out_specs=[
            pl.BlockSpec((1, tm, nqk), tok),
            pl.BlockSpec((1, tm, nqk), tok),
            pl.BlockSpec((1, tm, nv), tok),
            pl.BlockSpec((1, tm, nv), tok),
            pl.BlockSpec((1, tm, ng), tok),
            pl.BlockSpec((1, tm, ng), tok),
        ],
        out_shape=[
            jax.ShapeDtypeStruct((b, s, nqk), BF16),
            jax.ShapeDtypeStruct((b, s, nqk), BF16),
            jax.ShapeDtypeStruct((b, s, nv), BF16),
            jax.ShapeDtypeStruct((b, s, nv), BF16),
            jax.ShapeDtypeStruct((b, s, ng), F32),
            jax.ShapeDtypeStruct((b, s, ng), F32),
        ],
        compiler_params=_params(("parallel", "arbitrary")),
        name="mlstm_proj",
    )(x, x, mod, cv, gb, *weights)


def _split3_bf16(z):
    hi = z.astype(BF16)
    r1 = z - hi.astype(F32)
    mid = r1.astype(BF16)
    lo = (r1 - mid.astype(F32)).astype(BF16)
    return jnp.concatenate([hi, mid, lo], axis=1)


def _mlstm_core_kernel(q_ref, k_ref, v_ref, og_ref, gi_ref, gf_ref, hn_ref, e64_ref, e128_ref,
                       o_ref, c_ref, m_ref):
    tt = q_ref.shape[1]
    L = CHUNK
    nch = tt // L
    npair = ML_HEADS // 2
    t = pl.program_id(1)

    @pl.when(t == 0)
    def _():
        c_ref[...] = jnp.zeros_like(c_ref)
        m_ref[...] = jnp.zeros_like(m_ref)

    lane = lax.broadcasted_iota(jnp.int32, (L, LANES), 1)
    rowl = lax.broadcasted_iota(jnp.int32, (L, LANES), 0)
    m0 = lane < ML_QK
    src = lane % L
    rowt = lax.broadcasted_iota(jnp.int32, (tt, tt), 0)
    colt = lax.broadcasted_iota(jnp.int32, (tt, tt), 1)
    tri = (((rowt // L) == (colt // L)) & (rowt >= colt)).astype(BF16)
    row_in_chunk = lax.broadcasted_iota(jnp.int32, (tt, LANES), 0) % L
    row_first = lax.broadcasted_iota(jnp.int32, (LANES, 1), 0) < ML_QK
    ones = jnp.ones((L, LANES), BF16)

    def split_heads(zv):
        zb = zv.astype(BF16)
        zero = jnp.zeros_like(zb)
        return jnp.concatenate([jnp.where(m0, zb, zero), jnp.where(m0, zero, zb)], axis=0)

    gi = gi_ref[0]
    b3 = _dot(tri, _split3_bf16(gf_ref[0]))
    bcum = b3[:, :LANES] + b3[:, LANES:2 * LANES] + b3[:, 2 * LANES:]
    z = gi - bcum
    cmax = z
    shift = 1
    while shift < L:
        cmax = jnp.where(row_in_chunk >= shift, jnp.maximum(cmax, pltpu.roll(cmax, shift, 0)), cmax)
        shift *= 2
    m_intra = bcum + cmax

    m_prev = m_ref[0:1, :]
    mp_rows, mn_rows, gt_rows, s_a = [], [], [], []
    for c in range(nch):
        last = c * L + L - 1
        g_tot = bcum[last:last + 1, :]
        m_loc = g_tot + cmax[last:last + 1, :]
        m_new = jnp.maximum(g_tot + m_prev, m_loc)
        mp_rows.append(jnp.broadcast_to(m_prev, (L, LANES)))
        mn_rows.append(jnp.broadcast_to(m_new, (L, LANES)))
        gt_rows.append(jnp.broadcast_to(g_tot, (L, LANES)))
        s_a.append(jnp.exp(g_tot + m_prev - m_new))
        m_prev = m_new
    m_ref[0:1, :] = m_prev
    inter = bcum + jnp.concatenate(mp_rows, axis=0)
    m_t = jnp.maximum(inter, m_intra)
    w_inter = jnp.exp(inter - m_t)
    kws = jnp.exp(jnp.concatenate(gt_rows, axis=0) + z - jnp.concatenate(mn_rows, axis=0))

    head_lane = lax.broadcasted_iota(jnp.int32, (tt, LANES), 1) < ML_HEADS

    def pack3(val):
        val = jnp.where(head_lane, val, 0.0)
        hi = val.astype(BF16).astype(F32)
        r1 = val - hi
        mid = r1.astype(BF16).astype(F32)
        lo = (r1 - mid).astype(BF16).astype(F32)
        packed = hi + pltpu.roll(mid, ML_HEADS, 1) + pltpu.roll(lo, 2 * ML_HEADS, 1)
        return packed.astype(BF16)

    spread = _dot(jnp.concatenate([pack3(bcum - m_t), pack3(z), pack3(w_inter), pack3(kws)], axis=0),
                  e64_ref[...])
    a_exp, z_exp, wi_exp, kw_exp = (spread[i * tt:(i + 1) * tt] for i in range(4))
    em_exp = _dot(pack3(jnp.exp(-m_t)), e128_ref[...])

    un = []
    for c in range(nch):
        rows = slice(c * L, (c + 1) * L)
        for p in range(npair):
            lanes = slice(p * LANES, (p + 1) * LANES)
            qp = q_ref[0, rows, lanes]
            kp = k_ref[0, rows, lanes]
            v0 = v_ref[0, rows, 2 * p * ML_V:(2 * p + 1) * ML_V]
            v1 = v_ref[0, rows, (2 * p + 1) * ML_V:(2 * p + 2) * ML_V]
            z_src = jnp.sum(jnp.where(rowl == src, z_exp[rows, lanes], 0.0), axis=0, keepdims=True)
            un.append(dict(
                c=c, p=p, rows=rows, qp=qp,
                k2=split_heads(kp),
                expd=jnp.where(rowl >= src, jnp.exp(a_exp[rows, lanes] + z_src), 0.0),
                qs=split_heads(qp * wi_exp[rows, lanes]),
                kw=split_heads(kp * kw_exp[rows, lanes]),
                vst=jnp.concatenate([jnp.concatenate([v0, ones], axis=1),
                                     jnp.concatenate([v1, ones], axis=1)], axis=0),
            ))

    for u in un:
        scores = _dot_nt(u["qp"], u["k2"])
        u["w_intra"] = split_heads(u["expd"] * scores)
    for u in un:
        u["c_loc"] = _dot_tn(u["kw"], u["vst"])

    states = [c_ref[p] for p in range(npair)]
    for u in un:
        c, p = u["c"], u["p"]
        u["c_old"] = states[p].astype(BF16)
        sa = jnp.where(row_first, s_a[c][:, 2 * p:2 * p + 1], s_a[c][:, 2 * p + 1:2 * p + 2])
        states[p] = sa * states[p] + u["c_loc"]
    for p in range(npair):
        c_ref[p] = states[p]

    for u in un:
        u["nd"] = _dot(jnp.concatenate([u["qs"], u["w_intra"]], axis=1),
                       jnp.concatenate([u["c_old"], u["vst"]], axis=0))

    for u in un:
        rows = u["rows"]
        for hh in range(2):
            head = 2 * u["p"] + hh
            cols = slice(head * ML_V, (head + 1) * ML_V)
            num = u["nd"][hh * L:(hh + 1) * L, :LANES]
            den = u["nd"][hh * L:(hh + 1) * L, LANES:]
            den = jnp.maximum(jnp.abs(den), em_exp[rows, cols])
            hv = num / den
            hv = hv * lax.rsqrt(jnp.mean(hv * hv, axis=-1, keepdims=True) + EPS)
            hv = hv * hn_ref[0:1, cols]
            o_ref[0, rows, cols] = (hv * og_ref[0, rows, cols]).astype(o_ref.dtype)


def _expand_matrix(width):
    rows = jnp.arange(LANES)[:, None]
    cols = jnp.arange(ML_HEADS * width)[None, :]
    return ((rows < 3 * ML_HEADS) & (rows % ML_HEADS == cols // width)).astype(BF16)


def _mlstm_core(q, k, v, og, gi, gf, hn_w, tt):
    b, s, dv = v.shape
    assert s % tt == 0 and tt % CHUNK == 0 and dv == ML_HEADS * ML_V
    nqk = q.shape[2]
    ng = gi.shape[2]
    e64 = _expand_matrix(ML_QK)
    e128 = _expand_matrix(ML_V)
    tok = lambda i, j: (i, j, 0)
    return pl.pallas_call(
        _mlstm_core_kernel,
        grid=(b, s // tt),
        in_specs=[
            pl.BlockSpec((1, tt, nqk), tok),
            pl.BlockSpec((1, tt, nqk), tok),
            pl.BlockSpec((1, tt, dv), tok),
            pl.BlockSpec((1, tt, dv), tok),
            pl.BlockSpec((1, tt, ng), tok),
            pl.BlockSpec((1, tt, ng), tok),
            _const_spec(hn_w.shape),
            _const_spec(e64.shape),
            _const_spec(e128.shape),
        ],
        out_specs=pl.BlockSpec((1, tt, dv), tok),
        out_shape=jax.ShapeDtypeStruct((b, s, dv), BF16),
        scratch_shapes=[
            pltpu.VMEM((ML_HEADS // 2, LANES, 2 * LANES), F32),
            pltpu.VMEM((8, LANES), F32),
        ],
        compiler_params=_params(("parallel", "arbitrary")),
        name="mlstm_core",
    )(q, k, v, og, gi, gf, hn_w, e64, e128)


def _pad_rows(a, rows=8):
    return jnp.pad(a, ((0, rows - a.shape[0]), (0, 0)))


def kernel(x, c, ada_w, ada_b, rw_mix, rw_w_in, rw_w0, rw_w2, rw_a0, rw_a2, rw_g2, rw_k_k, rw_k_a, rw_r_k, rw_gn_w, rw_gn_b, rw_w_out, ml_w_in, ml_conv_w, ml_conv_b, ml_b_i, ml_b_f, ml_hn_w, ml_w_out, ffn_w_gu, ffn_w_down, final_w):
    b, s, d = x.shape
    bf = lambda w: w.astype(BF16)
    final_w2 = final_w.reshape(1, d)

    mod = _adaln_mod(c, ada_w, ada_b)

    w_in = rw_w_in[0]
    o1 = d
    o2 = o1 + LORA_W
    o3 = o2 + d
    o4 = o3 + d
    o5 = o4 + LORA_A
    rw_weights = [bf(w_in[:, :o1]), bf(w_in[:, o2:o3]), bf(w_in[:, o3:o4]),
                  bf(w_in[:, o1:o2]), bf(w_in[:, o4:o5]), bf(w_in[:, o5:]),
                  bf(rw_w2[0]), bf(rw_a2[0]), bf(rw_g2[0])]
    mix = _pad_rows(rw_mix[0])
    vecs = _pad_rows(jnp.stack([rw_w0[0], rw_a0[0], rw_k_k[0], rw_k_a[0]]))
    proj = _rwkv_proj(x, mod[0], mix, vecs, rw_weights, TM_DENSE)
    core_vecs = _pad_rows(jnp.stack([rw_r_k[0].reshape(d), rw_gn_w[0], rw_gn_b[0]]))
    y = _rwkv_core(proj, core_vecs, b, s, d, TT_RWKV, PAIRS_RWKV)
    w_gu = bf(ffn_w_gu)
    w_down = bf(ffn_w_down)
    x = _post_ffn(x, y, mod[0], final_w2, bf(rw_w_out[0]), w_gu, w_down, 0, TM_FFN, False)

    w_in = ml_w_in[0]
    nqk = 2 * ML_HEADS * ML_QK
    nv = ML_HEADS * ML_V
    pad_lanes = lambda w: jnp.pad(w, ((0, 0), (0, LANES - w.shape[1])))
    o_g = nqk + 2 * nv
    ml_weights = [bf(w_in[:, :nqk]), bf(w_in[:, nqk:nqk + nv]), bf(w_in[:, nqk + nv:o_g]),
                  bf(pad_lanes(w_in[:, o_g:o_g + ML_HEADS])), bf(pad_lanes(w_in[:, o_g + ML_HEADS:]))]
    cv = _pad_rows(jnp.concatenate([ml_conv_w[0], ml_conv_b[0][None, :]], axis=0))
    gb = _pad_rows(pad_lanes(jnp.stack([ml_b_i[0], ml_b_f[0]])))
    q, k, v, og, gi, gf = _mlstm_proj(x, mod[1], cv, gb, ml_weights, TM_DENSE)
    y = _mlstm_core(q, k, v, og, gi, gf, ml_hn_w[0].reshape(1, nv), TT_MLSTM)
    x = _post_ffn(x, y, mod[1], final_w2, bf(ml_w_out[0]), w_gu, w_down, 1, TM_FFN, True)
    return x
```

```python
import functools

import jax
import jax.numpy as jnp
from jax import lax
from jax.experimental import pallas as pl
from jax.experimental.pallas import tpu as pltpu

F32 = jnp.float32
BF16 = jnp.bfloat16

EPS = 1e-6
GN_EPS = 64e-5
GATE_CAP = 15.0
DECAY_SCALE = 0.6065306597126334
CHUNK = 64
LANES = 128
RW_HEAD = 64
LORA_W = 64
LORA_A = 64
LORA_G = 160
ML_HEADS = 8
ML_QK = 64
ML_V = 128
CONV_W = 4
COL_BLOCK = 256
VMEM_LIMIT = 56 * 1024 * 1024
TM_DENSE = 512
TM_FFN = 1024
TT_RWKV = 256
PAIRS_RWKV = 8
TT_MLSTM = 512

NT_DIMS = (((1,), (1,)), ((), ()))
TN_DIMS = (((0,), (0,)), ((), ()))


def _dot(a, b):
    return jnp.dot(a, b, preferred_element_type=F32)


def _dot_nt(a, b):
    return lax.dot_general(a, b, NT_DIMS, preferred_element_type=F32)


def _dot_tn(a, b):
    return lax.dot_general(a, b, TN_DIMS, preferred_element_type=F32)


def _sigmoid(z):
    return 0.5 * jnp.tanh(0.5 * z) + 0.5


def _softplus(z):
    return jnp.maximum(z, 0.0) + jnp.log(1.0 + jnp.exp(-jnp.abs(z)))


def _rms_mod(xv, sc, sh):
    inv = lax.rsqrt(jnp.mean(xv * xv, axis=-1, keepdims=True) + EPS)
    return xv * inv * (1.0 + sc) + sh


def _split_bf16(z):
    hi = z.astype(BF16)
    lo = (z - hi.astype(F32)).astype(BF16)
    return hi, lo


def _params(sem):
    return pltpu.CompilerParams(dimension_semantics=sem, vmem_limit_bytes=VMEM_LIMIT)


def _const_spec(shape):
    nd = len(shape)
    return pl.BlockSpec(shape, lambda *_: (0,) * nd, pipeline_mode=pl.Buffered(1))


def _mod_kernel(c_ref, w_ref, b_ref, o_ref):
    c = c_ref[...]
    ca = (c * _sigmoid(c)).astype(BF16)
    o_ref[0] = _dot(ca, w_ref[0].astype(BF16)) + b_ref[0]


def _adaln_mod(c, ada_w, ada_b):
    depth, d, d6 = ada_w.shape
    b = c.shape[0]
    bp = 8
    assert b <= bp and d6 % d == 0
    c_pad = jnp.zeros((bp, d), F32).at[:b].set(c)
    nblk = d6 // d
    out = pl.pallas_call(
        _mod_kernel,
        grid=(depth, nblk),
        in_specs=[
            pl.BlockSpec((bp, d), lambda i, j: (0, 0)),
            pl.BlockSpec((1, d, d), lambda i, j: (i, 0, j)),
            pl.BlockSpec((1, 1, d), lambda i, j: (i, 0, j)),
        ],
        out_specs=pl.BlockSpec((1, bp, d), lambda i, j: (i, 0, j)),
        out_shape=jax.ShapeDtypeStruct((depth, bp, d6), F32),
        compiler_params=_params(("arbitrary", "arbitrary")),
        name="adaln_mod",
    )(c_pad, ada_w, ada_b.reshape(depth, 1, d6))
    mod = out[:, :b].reshape(depth, b, nblk, d)
    return jnp.pad(mod, ((0, 0), (0, 0), (0, 8 - nblk), (0, 0)))


def _rwkv_proj_kernel(x_ref, xp_ref, mod_ref, mix_ref, vec_ref, wr_ref, wk_ref, wv_ref,
                      wwl_ref, wal_ref, wgl_ref, w2_ref, a2_ref, g2_ref,
                      r_out, k_out, v_out, kk_out, a_out, lw_out, g_out, hbuf, xbuf):
    tm = x_ref.shape[1]
    t = pl.program_id(1)
    sh = mod_ref[0, 0:1, :]
    sc = mod_ref[0, 1:2, :]
    hp = _rms_mod(xp_ref[0], sc, sh)
    hbuf[0:8, :] = jnp.where(t > 0, hp, 0.0)
    w0 = vec_ref[0:1, :]
    a0 = vec_ref[1:2, :]
    k_k = vec_ref[2:3, :]
    k_a = vec_ref[3:4, :]

    h = _rms_mod(x_ref[0], sc, sh)
    hbuf[8:, :] = h
    xbuf[...] = hbuf[pl.ds(7, tm), :] - h

    def mixed(i):
        return (hbuf[8:, :] + xbuf[...] * mix_ref[i:i + 1, :]).astype(BF16)

    def put(out, val, cb):
        for q in range(COL_BLOCK // LANES):
            out[0, cb * (COL_BLOCK // LANES) + q] = val[:, q * LANES:(q + 1) * LANES].astype(out.dtype)

    ncb = r_out.shape[1] * LANES // COL_BLOCK
    col = lambda cb: slice(cb * COL_BLOCK, (cb + 1) * COL_BLOCK)
    lhs = mixed(0)
    for cb in range(ncb):
        put(r_out, _dot(lhs, wr_ref[:, col(cb)]), cb)
    lhs = mixed(3)
    for cb in range(ncb):
        put(v_out, _dot(lhs, wv_ref[:, col(cb)]), cb)
    twl = jnp.tanh(_dot(mixed(1), wwl_ref[...])).astype(BF16)
    alb = _dot(mixed(4), wal_ref[...]).astype(BF16)
    sgl = _sigmoid(_dot(mixed(5), wgl_ref[...])).astype(BF16)
    a_blocks = []
    for cb in range(ncb):
        c = col(cb)
        put(lw_out, -DECAY_SCALE * _sigmoid(w0[:, c] + _dot(twl, w2_ref[:, c])), cb)
        a = _sigmoid(a0[:, c] + _dot(alb, a2_ref[:, c]))
        put(a_out, a, cb)
        a_blocks.append(a)
        put(g_out, _dot(sgl, g2_ref[:, c]), cb)
    lhs = mixed(2)
    for cb in range(ncb):
        c = col(cb)
        k = _dot(lhs, wk_ref[:, c])
        put(kk_out, k * k_k[:, c], cb)
        put(k_out, k * (1.0 + (a_blocks[cb] - 1.0) * k_a[:, c]), cb)


def _rwkv_proj(x, mod, mix, vecs, weights, tm):
    b, s, d = x.shape
    assert s % tm == 0 and d % COL_BLOCK == 0
    npair = d // LANES
    nt = s // tm
    blk8 = tm // 8
    pair_shape = lambda dt: jax.ShapeDtypeStruct((b, npair, s, LANES), dt)
    pair_spec = pl.BlockSpec((1, npair, tm, LANES), lambda i, j: (i, 0, j, 0))
    return pl.pallas_call(
        _rwkv_proj_kernel,
        grid=(b, nt),
        in_specs=[
            pl.BlockSpec((1, tm, d), lambda i, j: (i, j, 0)),
            pl.BlockSpec((1, 8, d), lambda i, j: (i, jnp.maximum(j * blk8 - 1, 0), 0)),
            pl.BlockSpec((1, 8, d), lambda i, j: (i, 0, 0)),
            _const_spec(mix.shape),
            _const_spec(vecs.shape),
        ] + [_const_spec(w.shape) for w in weights],
        out_specs=[pair_spec] * 7,
        out_shape=[pair_shape(BF16)] * 5 + [pair_shape(F32), pair_shape(BF16)],
        scratch_shapes=[pltpu.VMEM((tm + 8, d), F32), pltpu.VMEM((tm, d), F32)],
        compiler_params=_params(("parallel", "arbitrary")),
        name="rwkv_proj",
    )(x, x, mod, mix, vecs, *weights)


def _rwkv_core_kernel(r_ref, k_ref, v_ref, kk_ref, a_ref, lw_ref, g_ref, vec_ref, vec_prev_ref,
                      o_ref, s_ref, rbar_s, y0_s, mlrt_s, s0_s, wl_s, g_s, bg_s, *, tiles_per_seq):
    npg = r_ref.shape[1]
    tt = r_ref.shape[2]
    L = CHUNK
    nch = tt // L
    nunit = npg * nch
    step = pl.program_id(0)

    @pl.when(step == 0)
    def _():
        for ref in (s_ref, rbar_s, y0_s, mlrt_s, s0_s, wl_s, g_s, bg_s):
            ref[...] = jnp.zeros_like(ref)

    fresh = (jnp.maximum(step - 1, 0) % tiles_per_seq) == 0

    lane = lax.broadcasted_iota(jnp.int32, (L, LANES), 1)
    m0 = lane < RW_HEAD
    rowl = lax.broadcasted_iota(jnp.int32, (L, LANES), 0)
    strict = rowl > (lane % L)
    incl = rowl >= (lane % L)
    eye_cat = (rowl == (lane % L)).astype(F32)
    row2 = lax.broadcasted_iota(jnp.int32, (LANES, LANES), 0)
    col2 = lax.broadcasted_iota(jnp.int32, (LANES, LANES), 1)
    blockdiag = (row2 // RW_HEAD) == (col2 // RW_HEAD)
    rowt = lax.broadcasted_iota(jnp.int32, (tt, tt), 0)
    colt = lax.broadcasted_iota(jnp.int32, (tt, tt), 1)
    tri = (((rowt // L) == (colt // L)) & (rowt >= colt)).astype(BF16)
    zeros_l = jnp.zeros((L, LANES), BF16)

    def head_sum(z):
        lo = jnp.sum(jnp.where(m0, z, 0.0), axis=-1, keepdims=True)
        hi = jnp.sum(jnp.where(m0, 0.0, z), axis=-1, keepdims=True)
        return jnp.where(m0, lo, hi)

    def split_heads(z):
        zb = z.astype(BF16)
        zero = jnp.zeros_like(zb)
        return jnp.concatenate([jnp.where(m0, zb, zero), jnp.where(m0, zero, zb)], axis=0)

    units = [(p, c) for p in range(npg) for c in range(nch)]

    cums = []
    for p in range(npg):
        hi, lo = _split_bf16(lw_ref[0, p])
        both = _dot(tri, jnp.concatenate([hi, lo], axis=1))
        cums.append(both[:, :LANES] + both[:, LANES:])

    pu = []
    for (p, c) in units:
        rows = slice(c * L, (c + 1) * L)
        r = r_ref[0, p, rows, :].astype(F32)
        k = k_ref[0, p, rows, :].astype(F32)
        v = v_ref[0, p, rows, :]
        kk = kk_ref[0, p, rows, :].astype(F32)
        a = a_ref[0, p, rows, :].astype(F32)
        g = g_ref[0, p, rows, :]
        lw = lw_ref[0, p, rows, :]
        cum = cums[p][rows, :]
        kkn = kk * jnp.minimum(lax.rsqrt(head_sum(kk * kk)), 1e12)
        bvec = kkn * a
        c_last = cum[L - 1:L, :]
        at = -kkn * jnp.exp(cum - lw)
        rt = r * jnp.exp(cum)
        inv_w = jnp.exp(-cum)
        d_last = jnp.exp(c_last - cum)
        lanes = slice(p * LANES, (p + 1) * LANES)
        bonus = head_sum(r * k * vec_ref[0:1, lanes]) * v.astype(F32)
        pu.append(dict(
            rt=rt, v=v, w_last=jnp.exp(c_last), g=g, bg=bonus * g.astype(F32),
            lhs=jnp.concatenate([at, rt], axis=0).astype(BF16),
            rhs=jnp.concatenate([split_heads(bvec * inv_w), split_heads(k * inv_w)], axis=0),
            dk=jnp.concatenate([bvec * d_last, k * d_last], axis=0).astype(BF16),
            at_s=split_heads(at), v_s=split_heads(v),
        ))

    states = [jnp.where(fresh, 0.0, s_ref[p]) for p in range(npg)]
    ys = {}

    def scan_level(c):
        for p in range(npg):
            j = p * nch + c
            s_old = states[p]
            sb = s_old.astype(BF16)
            ys[j] = _dot_nt(rbar_s[j], sb) + y0_s[j]
            states[p] = s_old * wl_s[j] + _dot_nt(sb, mlrt_s[j]) + s0_s[j]

    per_level = -(-nunit // nch)
    for j, u in enumerate(pu):
        if j % per_level == 0:
            scan_level(j // per_level)
        gram = _dot_nt(u["lhs"], u["rhs"])
        u["pw"] = jnp.where(strict, gram[:L, :LANES], 0.0)
        u["r_b"] = jnp.where(incl, gram[L:, :LANES], 0.0).astype(BF16)
        u["ak_rk"] = jnp.concatenate([jnp.where(strict, gram[:L, LANES:], 0.0),
                                      jnp.where(incl, gram[L:, LANES:], 0.0)], axis=0).astype(BF16)
        u["tinv"] = eye_cat + u["pw"]
        u["pw_s"] = split_heads(u["pw"])
    for p in range(npg):
        s_ref[p] = states[p]

    for j in range(nunit):
        p, c = divmod(j, nch)
        lanes = slice(p * LANES, (p + 1) * LANES)
        y = ys[j]
        mu = head_sum(y) * (1.0 / RW_HEAD)
        yc = y - mu
        var = head_sum(yc * yc) * (1.0 / RW_HEAD)
        yn = yc * lax.rsqrt(var + GN_EPS) * vec_prev_ref[1:2, lanes] + vec_prev_ref[2:3, lanes]
        o_ref[0, c * L:(c + 1) * L, lanes] = (yn * g_s[j].astype(F32) + bg_s[j]).astype(o_ref.dtype)

    for u in pu:
        both = _dot(u["ak_rk"], u["v_s"])
        u["akv"] = both[:L]
        u["rkv"] = both[L:]

    for u in pu:
        u["pw"] = _dot(u["pw"].astype(BF16), u["pw_s"])
        u["pw_s"] = split_heads(u["pw"])
    n_levels = L.bit_length() - 1
    for level in range(1, n_levels):
        last = level == n_levels - 1
        for u in pu:
            if last:
                u["tinv"] = u["tinv"] + _dot(u["tinv"].astype(BF16), u["pw_s"])
            else:
                lhs = jnp.concatenate([u["tinv"], u["pw"]], axis=0).astype(BF16)
                both = _dot(lhs, u["pw_s"])
                u["tinv"] = u["tinv"] + both[:L]
                u["pw"] = both[L:]
                u["pw_s"] = split_heads(u["pw"])

    for u in pu:
        xin = jnp.concatenate([u["at_s"], split_heads(u["akv"])], axis=1)
        u["z"] = _dot(u["tinv"].astype(BF16), xin)
    for u in pu:
        z = u["z"]
        rhs = jnp.concatenate([split_heads(z[:, :LANES]), split_heads(z[:, LANES:])], axis=1)
        u["w"] = _dot(u["r_b"], rhs)
    for u in pu:
        bot = jnp.concatenate([zeros_l, u["v"]], axis=1)
        u["ms"] = _dot_tn(u["dk"], jnp.concatenate([u["z"].astype(BF16), bot], axis=0))

    for j, u in enumerate(pu):
        rbar_s[j] = (u["rt"] + u["w"][:, :LANES]).astype(BF16)
        y0_s[j] = u["w"][:, LANES:] + u["rkv"]
        mlrt_s[j] = jnp.where(blockdiag, u["ms"][:, :LANES], 0.0).astype(BF16)
        s0_s[j] = jnp.where(blockdiag, u["ms"][:, LANES:].T, 0.0)
        wl_s[j] = u["w_last"]
        g_s[j] = u["g"]
        bg_s[j] = u["bg"]


def _rwkv_core(proj, vecs, b, s, d, tt, npg):
    assert d % (2 * RW_HEAD) == 0 and (d // LANES) % npg == 0 and s % tt == 0 and tt % CHUNK == 0
    npair = d // LANES
    ngrp = npair // npg
    nt = s // tt
    ntiles = b * ngrp * nt
    nunit = npg * (tt // CHUNK)

    def tile_coords(tile):
        return tile // (ngrp * nt), (tile // nt) % ngrp, tile % nt

    def in_map(i):
        bi, gi, ti = tile_coords(jnp.minimum(i, ntiles - 1))
        return (bi, gi, ti, 0)

    def out_map(i):
        bi, gi, ti = tile_coords(jnp.maximum(i - 1, 0))
        return (bi, ti, gi)

    in_spec = pl.BlockSpec((1, npg, tt, LANES), in_map)
    vec_spec = pl.BlockSpec((8, npg * LANES), lambda i: (0, tile_coords(jnp.minimum(i, ntiles - 1))[1]))
    vec_prev_spec = pl.BlockSpec((8, npg * LANES), lambda i: (0, tile_coords(jnp.maximum(i - 1, 0))[1]))
    L = CHUNK
    return pl.pallas_call(
        functools.partial(_rwkv_core_kernel, tiles_per_seq=nt),
        grid=(ntiles + 1,),
        in_specs=[in_spec] * 7 + [vec_spec, vec_prev_spec],
        out_specs=pl.BlockSpec((1, tt, npg * LANES), out_map),
        out_shape=jax.ShapeDtypeStruct((b, s, d), BF16),
        scratch_shapes=[
            pltpu.VMEM((npg, LANES, LANES), F32),
            pltpu.VMEM((nunit, L, LANES), BF16),
            pltpu.VMEM((nunit, L, LANES), F32),
            pltpu.VMEM((nunit, LANES, LANES), BF16),
            pltpu.VMEM((nunit, LANES, LANES), F32),
            pltpu.VMEM((nunit, 1, LANES), F32),
            pltpu.VMEM((nunit, L, LANES), BF16),
            pltpu.VMEM((nunit, L, LANES), F32),
        ],
        compiler_params=_params(("arbitrary",)),
        name="rwkv_core",
    )(*proj, vecs, vecs)


def _post_ffn_kernel(x_ref, y_ref, mod_ref, fw_ref, wo_ref, wgu_ref, wd_ref, o_ref, act_ref,
                     *, final):
    g1 = mod_ref[0, 2:3, :]
    sh2 = mod_ref[0, 3:4, :]
    sc2 = mod_ref[0, 4:5, :]
    g2 = mod_ref[0, 5:6, :]
    x1 = x_ref[0] + g1 * _dot(y_ref[0], wo_ref[...])
    h = _rms_mod(x1, sc2, sh2).astype(BF16)
    d_ff = act_ref.shape[1]
    for c0 in range(0, d_ff, COL_BLOCK):
        cols = slice(c0, c0 + COL_BLOCK)
        gate = _dot(h, wgu_ref[0, :, cols])
        up = _dot(h, wgu_ref[0, :, d_ff + c0:d_ff + c0 + COL_BLOCK])
        act_ref[:, cols] = (gate * _sigmoid(gate) * up).astype(BF16)
    x2 = x1 + g2 * _dot(act_ref[...], wd_ref[0])
    if final:
        inv = lax.rsqrt(jnp.mean(x2 * x2, axis=-1, keepdims=True) + EPS)
        x2 = x2 * inv * fw_ref[0:1, :]
    o_ref[0] = x2


def _post_ffn(x, y, mod, final_w, w_out, w_gu, w_d, layer, tm, final):
    b, s, d = x.shape
    d_ff = w_d.shape[1]
    assert d_ff % COL_BLOCK == 0 and s % tm == 0
    kern = functools.partial(_post_ffn_kernel, final=final)
    tok = lambda i, j: (i, j, 0)
    layer_spec = lambda w: pl.BlockSpec((1,) + w.shape[1:], lambda i, j: (layer, 0, 0),
                                        pipeline_mode=pl.Buffered(1))
    return pl.pallas_call(
        kern,
        grid=(b, s // tm),
        in_specs=[
            pl.BlockSpec((1, tm, d), tok),
            pl.BlockSpec((1, tm, d), tok),
            pl.BlockSpec((1, 8, d), lambda i, j: (i, 0, 0)),
            _const_spec(final_w.shape),
            _const_spec(w_out.shape),
            layer_spec(w_gu),
            layer_spec(w_d),
        ],
        out_specs=pl.BlockSpec((1, tm, d), tok),
        out_shape=jax.ShapeDtypeStruct((b, s, d), F32),
        scratch_shapes=[pltpu.VMEM((tm, d_ff), BF16)],
        compiler_params=_params(("parallel", "parallel")),
        name="post_ffn_final" if final else "post_ffn",
    )(x, y, mod, final_w, w_out, w_gu, w_d)


def _softcap(z):
    return GATE_CAP * jnp.tanh(z * (1.0 / GATE_CAP))


def _mlstm_proj_kernel(x_ref, xp_ref, mod_ref, cv_ref, gb_ref, wqk_ref, wv_ref,
                       wo_ref, wgi_ref, wgf_ref,
                       q_out, k_out, v_out, og_out, gi_out, gf_out):
    tm = x_ref.shape[1]
    t = pl.program_id(1)
    sh = mod_ref[0, 0:1, :]
    sc = mod_ref[0, 1:2, :]
    h = _rms_mod(x_ref[0], sc, sh).astype(BF16)
    hp = _rms_mod(xp_ref[0], sc, sh).astype(BF16)

    half = wqk_ref.shape[1] // 2
    first_row = lax.broadcasted_iota(jnp.int32, (tm, COL_BLOCK), 0) == 0
    for c0 in range(0, wqk_ref.shape[1], COL_BLOCK):
        c = slice(c0, c0 + COL_BLOCK)
        pqk = _dot(h, wqk_ref[:, c])
        ppv = jnp.where(t > 0, _dot(hp, wqk_ref[:, c]), 0.0)
        acc = pqk * cv_ref[0:1, c]
        acc_prev = ppv * cv_ref[0:1, c]
        for j in range(1, CONV_W):
            wj = cv_ref[j:j + 1, c]
            carry = acc_prev[7:8, :]
            acc = pqk * wj + jnp.where(first_row, carry, pltpu.roll(acc, 1, 0))
            acc_prev = ppv * wj + pltpu.roll(acc_prev, 1, 0)
        conv = acc + cv_ref[CONV_W:CONV_W + 1, c]
        qk = conv * _sigmoid(conv)
        if c0 < half:
            q_out[0, :, c] = (qk * (ML_QK ** -0.5)).astype(q_out.dtype)
        else:
            k_out[0, :, c0 - half:c0 - half + COL_BLOCK] = qk.astype(k_out.dtype)
    for c0 in range(0, wv_ref.shape[1], COL_BLOCK):
        c = slice(c0, c0 + COL_BLOCK)
        v_out[0, :, c] = _dot(h, wv_ref[:, c]).astype(v_out.dtype)
    for c0 in range(0, wo_ref.shape[1], COL_BLOCK):
        c = slice(c0, c0 + COL_BLOCK)
        og_out[0, :, c] = _sigmoid(_dot(h, wo_ref[:, c])).astype(og_out.dtype)

    lane = lax.broadcasted_iota(jnp.int32, gi_out.shape[1:], 1)
    zi = _dot(h, wgi_ref[...]) + gb_ref[0:1, :]
    zf = _dot(h, wgf_ref[...]) + gb_ref[1:2, :]
    gi_out[0] = jnp.where(lane < ML_HEADS, _softcap(zi), 0.0)
    gf_out[0] = jnp.where(lane < ML_HEADS, -_softplus(-_softcap(zf)), 0.0)


def _mlstm_proj(x, mod, cv, gb, weights, tm):
    b, s, d = x.shape
    w_qk, w_v, w_o, w_gi, w_gf = weights
    nqk = w_qk.shape[1] // 2
    nv = w_v.shape[1]
    assert s % tm == 0 and w_qk.shape[1] % COL_BLOCK == 0 and nv % COL_BLOCK == 0
    ng = w_gi.shape[1]
    blk8 = tm // 8
    tok = lambda i, j: (i, j, 0)
    return pl.pallas_call(
        _mlstm_proj_kernel,
        grid=(b, s // tm),
        in_specs=[
            pl.BlockSpec((1, tm, d), tok),
            pl.BlockSpec((1, 8, d), lambda i, j: (i, jnp.maximum(j * blk8 - 1, 0), 0)),
            pl.BlockSpec((1, 8, d), lambda i, j: (i, 0, 0)),
            _const_spec(cv.shape),
            _const_spec(gb.shape),
        ] + [_const_spec(w.shape) for w in weights],
        out_specs=[
            pl.BlockSpec((1, tm, nqk), tok),
            pl.BlockSpec((1, tm, nqk), tok),
            pl.BlockSpec((1, tm, nv), tok),
            pl.BlockSpec((1, tm, nv), tok),
            pl.BlockSpec((1, tm, ng), tok),
            pl.BlockSpec((1, tm, ng), tok),
        ],
        out_shape=[
            jax.ShapeDtypeStruct((b, s, nqk), BF16),
            jax.ShapeDtypeStruct((b, s, nqk), BF16),
            jax.ShapeDtypeStruct((b, s, nv), BF16),
            jax.ShapeDtypeStruct((b, s, nv), BF16),
            jax.ShapeDtypeStruct((b, s, ng), F32),
            jax.ShapeDtypeStruct((b, s, ng), F32),
        ],
        compiler_params=_params(("parallel", "arbitrary")),
        name="mlstm_proj",
    )(x, x, mod, cv, gb, *weights)


def _split3_bf16(z):
    hi = z.astype(BF16)
    r1 = z - hi.astype(F32)
    mid = r1.astype(BF16)
    lo = (r1 - mid.astype(F32)).astype(BF16)
    return jnp.concatenate([hi, mid, lo], axis=1)


def _mlstm_core_kernel(q_ref, k_ref, v_ref, og_ref, gi_ref, gf_ref, hn_ref, e64_ref, e128_ref,
                       o_ref, c_ref, m_ref):
    tt = q_ref.shape[1]
    L = CHUNK
    nch = tt // L
    npair = ML_HEADS // 2
    t = pl.program_id(1)

    @pl.when(t == 0)
    def _():
        c_ref[...] = jnp.zeros_like(c_ref)
        m_ref[...] = jnp.zeros_like(m_ref)

    lane = lax.broadcasted_iota(jnp.int32, (L, LANES), 1)
    rowl = lax.broadcasted_iota(jnp.int32, (L, LANES), 0)
    m0 = lane < ML_QK
    src = lane % L
    rowt = lax.broadcasted_iota(jnp.int32, (tt, tt), 0)
    colt = lax.broadcasted_iota(jnp.int32, (tt, tt), 1)
    tri = (((rowt // L) == (colt // L)) & (rowt >= colt)).astype(BF16)
    row_in_chunk = lax.broadcasted_iota(jnp.int32, (tt, LANES), 0) % L
    row_first = lax.broadcasted_iota(jnp.int32, (LANES, 1), 0) < ML_QK
    ones = jnp.ones((L, LANES), BF16)

    def split_heads(zv):
        zb = zv.astype(BF16)
        zero = jnp.zeros_like(zb)
        return jnp.concatenate([jnp.where(m0, zb, zero), jnp.where(m0, zero, zb)], axis=0)

    gi = gi_ref[0]
    b3 = _dot(tri, _split3_bf16(gf_ref[0]))
    bcum = b3[:, :LANES] + b3[:, LANES:2 * LANES] + b3[:, 2 * LANES:]
    z = gi - bcum
    cmax = z
    shift = 1
    while shift < L:
        cmax = jnp.where(row_in_chunk >= shift, jnp.maximum(cmax, pltpu.roll(cmax, shift, 0)), cmax)
        shift *= 2
    m_intra = bcum + cmax

    m_prev = m_ref[0:1, :]
    mp_rows, mn_rows, gt_rows, s_a = [], [], [], []
    for c in range(nch):
        last = c * L + L - 1
        g_tot = bcum[last:last + 1, :]
        m_loc = g_tot + cmax[last:last + 1, :]
        m_new = jnp.maximum(g_tot + m_prev, m_loc)
        mp_rows.append(jnp.broadcast_to(m_prev, (L, LANES)))
        mn_rows.append(jnp.broadcast_to(m_new, (L, LANES)))
        gt_rows.append(jnp.broadcast_to(g_tot, (L, LANES)))
        s_a.append(jnp.exp(g_tot + m_prev - m_new))
        m_prev = m_new
    m_ref[0:1, :] = m_prev
    inter = bcum + jnp.concatenate(mp_rows, axis=0)
    m_t = jnp.maximum(inter, m_intra)
    w_inter = jnp.exp(inter - m_t)
    kws = jnp.exp(jnp.concatenate(gt_rows, axis=0) + z - jnp.concatenate(mn_rows, axis=0))

    head_lane = lax.broadcasted_iota(jnp.int32, (tt, LANES), 1) < ML_HEADS

    def pack3(val):
        val = jnp.where(head_lane, val, 0.0)
        hi = val.astype(BF16).astype(F32)
        r1 = val - hi
        mid = r1.astype(BF16).astype(F32)
        lo = (r1 - mid).astype(BF16).astype(F32)
        packed = hi + pltpu.roll(mid, ML_HEADS, 1) + pltpu.roll(lo, 2 * ML_HEADS, 1)
        return packed.astype(BF16)

    spread = _dot(jnp.concatenate([pack3(bcum - m_t), pack3(z), pack3(w_inter), pack3(kws)], axis=0),
                  e64_ref[...])
    a_exp, z_exp, wi_exp, kw_exp = (spread[i * tt:(i + 1) * tt] for i in range(4))
    em_exp = _dot(pack3(jnp.exp(-m_t)), e128_ref[...])

    un = []
    for c in range(nch):
        rows = slice(c * L, (c + 1) * L)
        for p in range(npair):
            lanes = slice(p * LANES, (p + 1) * LANES)
            qp = q_ref[0, rows, lanes]
            kp = k_ref[0, rows, lanes]
            v0 = v_ref[0, rows, 2 * p * ML_V:(2 * p + 1) * ML_V]
            v1 = v_ref[0, rows, (2 * p + 1) * ML_V:(2 * p + 2) * ML_V]
            z_src = jnp.sum(jnp.where(rowl == src, z_exp[rows, lanes], 0.0), axis=0, keepdims=True)
            un.append(dict(
                c=c, p=p, rows=rows, qp=qp,
                k2=split_heads(kp),
                expd=jnp.where(rowl >= src, jnp.exp(a_exp[rows, lanes] + z_src), 0.0),
                qs=split_heads(qp * wi_exp[rows, lanes]),
                kw=split_heads(kp * kw_exp[rows, lanes]),
                vst=jnp.concatenate([jnp.concatenate([v0, ones], axis=1),
                                     jnp.concatenate([v1, ones], axis=1)], axis=0),
            ))

    for u in un:
        scores = _dot_nt(u["qp"], u["k2"])
        u["w_intra"] = split_heads(u["expd"] * scores)
    for u in un:
        u["c_loc"] = _dot_tn(u["kw"], u["vst"])

    states = [c_ref[p] for p in range(npair)]
    for u in un:
        c, p = u["c"], u["p"]
        u["c_old"] = states[p].astype(BF16)
        sa = jnp.where(row_first, s_a[c][:, 2 * p:2 * p + 1], s_a[c][:, 2 * p + 1:2 * p + 2])
        states[p] = sa * states[p] + u["c_loc"]
    for p in range(npair):
        c_ref[p] = states[p]

    for u in un:
        u["nd"] = _dot(jnp.concatenate([u["qs"], u["w_intra"]], axis=1),
                       jnp.concatenate([u["c_old"], u["vst"]], axis=0))

    for u in un:
        rows = u["rows"]
        for hh in range(2):
            head = 2 * u["p"] + hh
            cols = slice(head * ML_V, (head + 1) * ML_V)
            num = u["nd"][hh * L:(hh + 1) * L, :LANES]
            den = u["nd"][hh * L:(hh + 1) * L, LANES:]
            den = jnp.maximum(jnp.abs(den), em_exp[rows, cols])
            hv = num / den
            hv = hv * lax.rsqrt(jnp.mean(hv * hv, axis=-1, keepdims=True) + EPS)
            hv = hv * hn_ref[0:1, cols]
            o_ref[0, rows, cols] = (hv * og_ref[0, rows, cols]).astype(o_ref.dtype)


def _expand_matrix(width):
    rows = jnp.arange(LANES)[:, None]
    cols = jnp.arange(ML_HEADS * width)[None, :]
    return ((rows < 3 * ML_HEADS) & (rows % ML_HEADS == cols // width)).astype(BF16)


def _mlstm_core(q, k, v, og, gi, gf, hn_w, tt):
    b, s, dv = v.shape
    assert s % tt == 0 and tt % CHUNK == 0 and dv == ML_HEADS * ML_V
    nqk = q.shape[2]
    ng = gi.shape[2]
    e64 = _expand_matrix(ML_QK)
    e128 = _expand_matrix(ML_V)
    tok = lambda i, j: (i, j, 0)
    return pl.pallas_call(
        _mlstm_core_kernel,
        grid=(b, s // tt),
        in_specs=[
            pl.BlockSpec((1, tt, nqk), tok),
            pl.BlockSpec((1, tt, nqk), tok),
            pl.BlockSpec((1, tt, dv), tok),
            pl.BlockSpec((1, tt, dv), tok),
            pl.BlockSpec((1, tt, ng), tok),
            pl.BlockSpec((1, tt, ng), tok),
            _const_spec(hn_w.shape),
            _const_spec(e64.shape),
            _const_spec(e128.shape),
        ],
        out_specs=pl.BlockSpec((1, tt, dv), tok),
        out_shape=jax.ShapeDtypeStruct((b, s, dv), BF16),
        scratch_shapes=[
            pltpu.VMEM((ML_HEADS // 2, LANES, 2 * LANES), F32),
            pltpu.VMEM((8, LANES), F32),
        ],
        compiler_params=_params(("parallel", "arbitrary")),
        name="mlstm_core",
    )(q, k, v, og, gi, gf, hn_w, e64, e128)


def _pad_rows(a, rows=8):
    return jnp.pad(a, ((0, rows - a.shape[0]), (0, 0)))


def kernel(x, c, ada_w, ada_b, rw_mix, rw_w_in, rw_w0, rw_w2, rw_a0, rw_a2, rw_g2, rw_k_k, rw_k_a, rw_r_k, rw_gn_w, rw_gn_b, rw_w_out, ml_w_in, ml_conv_w, ml_conv_b, ml_b_i, ml_b_f, ml_hn_w, ml_w_out, ffn_w_gu, ffn_w_down, final_w):
    b, s, d = x.shape
    bf = lambda w: w.astype(BF16)
    final_w2 = final_w.reshape(1, d)

    mod = _adaln_mod(c, ada_w, ada_b)

    w_in = rw_w_in[0]
    o1 = d
    o2 = o1 + LORA_W
    o3 = o2 + d
    o4 = o3 + d
    o5 = o4 + LORA_A
    rw_weights = [bf(w_in[:, :o1]), bf(w_in[:, o2:o3]), bf(w_in[:, o3:o4]),
                  bf(w_in[:, o1:o2]), bf(w_in[:, o4:o5]), bf(w_in[:, o5:]),
                  bf(rw_w2[0]), bf(rw_a2[0]), bf(rw_g2[0])]
    mix = _pad_rows(rw_mix[0])
    vecs = _pad_rows(jnp.stack([rw_w0[0], rw_a0[0], rw_k_k[0], rw_k_a[0]]))
    proj = _rwkv_proj(x, mod[0], mix, vecs, rw_weights, TM_DENSE)
    core_vecs = _pad_rows(jnp.stack([rw_r_k[0].reshape(d), rw_gn_w[0], rw_gn_b[0]]))
    y = _rwkv_core(proj, core_vecs, b, s, d, TT_RWKV, PAIRS_RWKV)
    w_gu = bf(ffn_w_gu)
    w_down = bf(ffn_w_down)
    x = _post_ffn(x, y, mod[0], final_w2, bf(rw_w_out[0]), w_gu, w_down, 0, TM_FFN, False)

    w_in = ml_w_in[0]
    nqk = 2 * ML_HEADS * ML_QK
    nv = ML_HEADS * ML_V
    pad_lanes = lambda w: jnp.pad(w, ((0, 0), (0, LANES - w.shape[1])))
    o_g = nqk + 2 * nv
    ml_weights = [bf(w_in[:, :nqk]), bf(w_in[:, nqk:nqk + nv]), bf(w_in[:, nqk + nv:o_g]),
                  bf(pad_lanes(w_in[:, o_g:o_g + ML_HEADS])), bf(pad_lanes(w_in[:, o_g + ML_HEADS:]))]
    cv = _pad_rows(jnp.concatenate([ml_conv_w[0], ml_conv_b[0][None, :]], axis=0))
    gb = _pad_rows(pad_lanes(jnp.stack([ml_b_i[0], ml_b_f[0]])))
    q, k, v, og, gi, gf = _mlstm_proj(x, mod[1], cv, gb, ml_weights, TM_DENSE)
    y = _mlstm_core(q, k, v, og, gi, gf, ml_hn_w[0].reshape(1, nv), TT_MLSTM)
    x = _post_ffn(x, y, mod[1], final_w2, bf(ml_w_out[0]), w_gu, w_down, 1, TM_FFN, True)
    return x
```

```python
import functools

import jax
import jax.numpy as jnp
from jax import lax
from jax.experimental import pallas as pl
from jax.experimental.pallas import tpu as pltpu

F32 = jnp.float32
BF16 = jnp.bfloat16

EPS = 1e-6
GN_EPS = 64e-5
GATE_CAP = 15.0
DECAY_SCALE = 0.6065306597126334
CHUNK = 64
LANES = 128
RW_HEAD = 64
LORA_W = 64
LORA_A = 64
LORA_G = 160
ML_HEADS = 8
ML_QK = 64
ML_V = 128
CONV_W = 4
COL_BLOCK = 256
VMEM_LIMIT = 56 * 1024 * 1024
TM_RWKV_PROJ = 512
TM_DENSE = 1024
TT_RWKV = 256
PAIRS_RWKV = 8
TT_MLSTM = 512

NT_DIMS = (((1,), (1,)), ((), ()))
TN_DIMS = (((0,), (0,)), ((), ()))


def _dot(a, b):
    return jnp.dot(a, b, preferred_element_type=F32)


def _dot_nt(a, b):
    return lax.dot_general(a, b, NT_DIMS, preferred_element_type=F32)


def _dot_tn(a, b):
    return lax.dot_general(a, b, TN_DIMS, preferred_element_type=F32)


def _sigmoid(z):
    return 0.5 * jnp.tanh(0.5 * z) + 0.5


def _softplus(z):
    return jnp.maximum(z, 0.0) + jnp.log(1.0 + jnp.exp(-jnp.abs(z)))


def _rms_mod(xv, sc, sh):
    inv = lax.rsqrt(jnp.mean(xv * xv, axis=-1, keepdims=True) + EPS)
    return xv * inv * (1.0 + sc) + sh


def _split_bf16(z):
    hi = z.astype(BF16)
    lo = (z - hi.astype(F32)).astype(BF16)
    return hi, lo


def _params(sem):
    return pltpu.CompilerParams(dimension_semantics=sem, vmem_limit_bytes=VMEM_LIMIT)


def _const_spec(shape):
    nd = len(shape)
    return pl.BlockSpec(shape, lambda *_: (0,) * nd, pipeline_mode=pl.Buffered(1))


def _mod_kernel(c_ref, w_ref, b_ref, o_ref):
    c = c_ref[...]
    ca = (c * _sigmoid(c)).astype(BF16)
    o_ref[0] = _dot(ca, w_ref[0].astype(BF16)) + b_ref[0]


def _adaln_mod(c, ada_w, ada_b):
    depth, d, d6 = ada_w.shape
    b = c.shape[0]
    bp = 8
    assert b <= bp and d6 % d == 0
    c_pad = jnp.zeros((bp, d), F32).at[:b].set(c)
    nblk = d6 // d
    out = pl.pallas_call(
        _mod_kernel,
        grid=(depth, nblk),
        in_specs=[
            pl.BlockSpec((bp, d), lambda i, j: (0, 0)),
            pl.BlockSpec((1, d, d), lambda i, j: (i, 0, j)),
            pl.BlockSpec((1, 1, d), lambda i, j: (i, 0, j)),
        ],
        out_specs=pl.BlockSpec((1, bp, d), lambda i, j: (i, 0, j)),
        out_shape=jax.ShapeDtypeStruct((depth, bp, d6), F32),
        compiler_params=_params(("arbitrary", "arbitrary")),
        name="adaln_mod",
    )(c_pad, ada_w, ada_b.reshape(depth, 1, d6))
    mod = out[:, :b].reshape(depth, b, nblk, d)
    return jnp.pad(mod, ((0, 0), (0, 0), (0, 8 - nblk), (0, 0)))


def _rwkv_proj_kernel(x_ref, xp_ref, mod_ref, mix_ref, vec_ref, wr_ref, wk_ref, wv_ref,
                      wwl_ref, wal_ref, wgl_ref, w2_ref, a2_ref, g2_ref,
                      r_out, k_out, v_out, kk_out, a_out, lw_out, g_out, hbuf, xbuf):
    tm = x_ref.shape[1]
    t = pl.program_id(1)
    sh = mod_ref[0, 0:1, :]
    sc = mod_ref[0, 1:2, :]
    hp = _rms_mod(xp_ref[0], sc, sh)
    hbuf[0:8, :] = jnp.where(t > 0, hp, 0.0)
    w0 = vec_ref[0:1, :]
    a0 = vec_ref[1:2, :]
    k_k = vec_ref[2:3, :]
    k_a = vec_ref[3:4, :]

    h = _rms_mod(x_ref[0], sc, sh)
    hbuf[8:, :] = h
    xbuf[...] = hbuf[pl.ds(7, tm), :] - h

    def mixed(i):
        return (hbuf[8:, :] + xbuf[...] * mix_ref[i:i + 1, :]).astype(BF16)

    def put(out, val, cb):
        for q in range(COL_BLOCK // LANES):
            out[0, cb * (COL_BLOCK // LANES) + q] = val[:, q * LANES:(q + 1) * LANES].astype(out.dtype)

    ncb = r_out.shape[1] * LANES // COL_BLOCK
    col = lambda cb: slice(cb * COL_BLOCK, (cb + 1) * COL_BLOCK)
    lhs = mixed(0)
    for cb in range(ncb):
        put(r_out, _dot(lhs, wr_ref[:, col(cb)]), cb)
    lhs = mixed(3)
    for cb in range(ncb):
        put(v_out, _dot(lhs, wv_ref[:, col(cb)]), cb)
    twl = jnp.tanh(_dot(mixed(1), wwl_ref[...])).astype(BF16)
    alb = _dot(mixed(4), wal_ref[...]).astype(BF16)
    sgl = _sigmoid(_dot(mixed(5), wgl_ref[...])).astype(BF16)
    a_blocks = []
    for cb in range(ncb):
        c = col(cb)
        put(lw_out, -DECAY_SCALE * _sigmoid(w0[:, c] + _dot(twl, w2_ref[:, c])), cb)
        a = _sigmoid(a0[:, c] + _dot(alb, a2_ref[:, c]))
        put(a_out, a, cb)
        a_blocks.append(a)
        put(g_out, _dot(sgl, g2_ref[:, c]), cb)
    lhs = mixed(2)
    for cb in range(ncb):
        c = col(cb)
        k = _dot(lhs, wk_ref[:, c])
        put(kk_out, k * k_k[:, c], cb)
        put(k_out, k * (1.0 + (a_blocks[cb] - 1.0) * k_a[:, c]), cb)


def _rwkv_proj(x, mod, mix, vecs, weights, tm):
    b, s, d = x.shape
    assert s % tm == 0 and d % COL_BLOCK == 0
    npair = d // LANES
    nt = s // tm
    blk8 = tm // 8
    pair_shape = lambda dt: jax.ShapeDtypeStruct((b, npair, s, LANES), dt)
    pair_spec = pl.BlockSpec((1, npair, tm, LANES), lambda i, j: (i, 0, j, 0))
    return pl.pallas_call(
        _rwkv_proj_kernel,
        grid=(b, nt),
        in_specs=[
            pl.BlockSpec((1, tm, d), lambda i, j: (i, j, 0)),
            pl.BlockSpec((1, 8, d), lambda i, j: (i, jnp.maximum(j * blk8 - 1, 0), 0)),
            pl.BlockSpec((1, 8, d), lambda i, j: (i, 0, 0)),
            _const_spec(mix.shape),
            _const_spec(vecs.shape),
        ] + [_const_spec(w.shape) for w in weights],
        out_specs=[pair_spec] * 7,
        out_shape=[pair_shape(BF16)] * 5 + [pair_shape(F32), pair_shape(BF16)],
        scratch_shapes=[pltpu.VMEM((tm + 8, d), F32), pltpu.VMEM((tm, d), F32)],
        compiler_params=_params(("parallel", "arbitrary")),
        name="rwkv_proj",
    )(x, x, mod, mix, vecs, *weights)


def _rwkv_core_kernel(r_ref, k_ref, v_ref, kk_ref, a_ref, lw_ref, g_ref, vec_ref, vec_prev_ref,
                      o_ref, s_ref, rbar_s, y0_s, mlrt_s, s0_s, wl_s, g_s, bg_s, *, tiles_per_seq):
    npg = r_ref.shape[1]
    tt = r_ref.shape[2]
    L = CHUNK
    nch = tt // L
    nunit = npg * nch
    step = pl.program_id(0)

    @pl.when(step == 0)
    def _():
        for ref in (s_ref, rbar_s, y0_s, mlrt_s, s0_s, wl_s, g_s, bg_s):
            ref[...] = jnp.zeros_like(ref)

    fresh = (jnp.maximum(step - 1, 0) % tiles_per_seq) == 0

    lane = lax.broadcasted_iota(jnp.int32, (L, LANES), 1)
    m0 = lane < RW_HEAD
    rowl = lax.broadcasted_iota(jnp.int32, (L, LANES), 0)
    strict = rowl > (lane % L)
    incl = rowl >= (lane % L)
    eye_cat = (rowl == (lane % L)).astype(F32)
    row2 = lax.broadcasted_iota(jnp.int32, (LANES, LANES), 0)
    col2 = lax.broadcasted_iota(jnp.int32, (LANES, LANES), 1)
    blockdiag = (row2 // RW_HEAD) == (col2 // RW_HEAD)
    rowt = lax.broadcasted_iota(jnp.int32, (tt, tt), 0)
    colt = lax.broadcasted_iota(jnp.int32, (tt, tt), 1)
    tri = (((rowt // L) == (colt // L)) & (rowt >= colt)).astype(BF16)
    zeros_l = jnp.zeros((L, LANES), BF16)

    def head_sum(z):
        lo = jnp.sum(jnp.where(m0, z, 0.0), axis=-1, keepdims=True)
        hi = jnp.sum(jnp.where(m0, 0.0, z), axis=-1, keepdims=True)
        return jnp.where(m0, lo, hi)

    def split_heads(z):
        zb = z.astype(BF16)
        zero = jnp.zeros_like(zb)
        return jnp.concatenate([jnp.where(m0, zb, zero), jnp.where(m0, zero, zb)], axis=0)

    units = [(p, c) for p in range(npg) for c in range(nch)]

    cums = []
    for p in range(npg):
        hi, lo = _split_bf16(lw_ref[0, p])
        both = _dot(tri, jnp.concatenate([hi, lo], axis=1))
        cums.append(both[:, :LANES] + both[:, LANES:])

    pu = []
    for (p, c) in units:
        rows = slice(c * L, (c + 1) * L)
        r = r_ref[0, p, rows, :].astype(F32)
        k = k_ref[0, p, rows, :].astype(F32)
        v = v_ref[0, p, rows, :]
        kk = kk_ref[0, p, rows, :].astype(F32)
        a = a_ref[0, p, rows, :].astype(F32)
        g = g_ref[0, p, rows, :]
        lw = lw_ref[0, p, rows, :]
        cum = cums[p][rows, :]
        kkn = kk * jnp.minimum(lax.rsqrt(head_sum(kk * kk)), 1e12)
        bvec = kkn * a
        c_last = cum[L - 1:L, :]
        at = -kkn * jnp.exp(cum - lw)
        rt = r * jnp.exp(cum)
        inv_w = jnp.exp(-cum)
        d_last = jnp.exp(c_last - cum)
        lanes = slice(p * LANES, (p + 1) * LANES)
        bonus = head_sum(r * k * vec_ref[0:1, lanes]) * v.astype(F32)
        pu.append(dict(
            rt=rt, v=v, w_last=jnp.exp(c_last), g=g, bg=bonus * g.astype(F32),
            lhs=jnp.concatenate([at, rt], axis=0).astype(BF16),
            rhs=jnp.concatenate([split_heads(bvec * inv_w), split_heads(k * inv_w)], axis=0),
            dk=jnp.concatenate([bvec * d_last, k * d_last], axis=0).astype(BF16),
            at_s=split_heads(at), v_s=split_heads(v),
        ))

    states = [jnp.where(fresh, 0.0, s_ref[p]) for p in range(npg)]
    ys = {}

    def scan_level(c):
        for p in range(npg):
            j = p * nch + c
            s_old = states[p]
            sb = s_old.astype(BF16)
            ys[j] = _dot_nt(rbar_s[j], sb) + y0_s[j]
            states[p] = s_old * wl_s[j] + _dot_nt(sb, mlrt_s[j]) + s0_s[j]

    per_level = -(-nunit // nch)
    for j, u in enumerate(pu):
        if j % per_level == 0:
            scan_level(j // per_level)
        gram = _dot_nt(u["lhs"], u["rhs"])
        u["pw"] = jnp.where(strict, gram[:L, :LANES], 0.0)
        u["r_b"] = jnp.where(incl, gram[L:, :LANES], 0.0).astype(BF16)
        u["ak_rk"] = jnp.concatenate([jnp.where(strict, gram[:L, LANES:], 0.0),
                                      jnp.where(incl, gram[L:, LANES:], 0.0)], axis=0).astype(BF16)
        u["tinv"] = eye_cat + u["pw"]
        u["pw_s"] = split_heads(u["pw"])
    for p in range(npg):
        s_ref[p] = states[p]

    for j in range(nunit):
        p, c = divmod(j, nch)
        lanes = slice(p * LANES, (p + 1) * LANES)
        y = ys[j]
        mu = head_sum(y) * (1.0 / RW_HEAD)
        yc = y - mu
        var = head_sum(yc * yc) * (1.0 / RW_HEAD)
        yn = yc * lax.rsqrt(var + GN_EPS) * vec_prev_ref[1:2, lanes] + vec_prev_ref[2:3, lanes]
        o_ref[0, c * L:(c + 1) * L, lanes] = (yn * g_s[j].astype(F32) + bg_s[j]).astype(o_ref.dtype)

    for u in pu:
        both = _dot(u["ak_rk"], u["v_s"])
        u["akv"] = both[:L]
        u["rkv"] = both[L:]

    for u in pu:
        u["pw"] = _dot(u["pw"].astype(BF16), u["pw_s"])
        u["pw_s"] = split_heads(u["pw"])
    n_levels = L.bit_length() - 1
    for level in range(1, n_levels):
        last = level == n_levels - 1
        for u in pu:
            if last:
                u["tinv"] = u["tinv"] + _dot(u["tinv"].astype(BF16), u["pw_s"])
            else:
                lhs = jnp.concatenate([u["tinv"], u["pw"]], axis=0).astype(BF16)
                both = _dot(lhs, u["pw_s"])
                u["tinv"] = u["tinv"] + both[:L]
                u["pw"] = both[L:]
                u["pw_s"] = split_heads(u["pw"])

    for u in pu:
        xin = jnp.concatenate([u["at_s"], split_heads(u["akv"])], axis=1)
        u["z"] = _dot(u["tinv"].astype(BF16), xin)
    for u in pu:
        z = u["z"]
        rhs = jnp.concatenate([split_heads(z[:, :LANES]), split_heads(z[:, LANES:])], axis=1)
        u["w"] = _dot(u["r_b"], rhs)
    for u in pu:
        bot = jnp.concatenate([zeros_l, u["v"]], axis=1)
        u["ms"] = _dot_tn(u["dk"], jnp.concatenate([u["z"].astype(BF16), bot], axis=0))

    for j, u in enumerate(pu):
        rbar_s[j] = (u["rt"] + u["w"][:, :LANES]).astype(BF16)
        y0_s[j] = u["w"][:, LANES:] + u["rkv"]
        mlrt_s[j] = jnp.where(blockdiag, u["ms"][:, :LANES], 0.0).astype(BF16)
        s0_s[j] = jnp.where(blockdiag, u["ms"][:, LANES:].T, 0.0)
        wl_s[j] = u["w_last"]
        g_s[j] = u["g"]
        bg_s[j] = u["bg"]


def _rwkv_core(proj, vecs, b, s, d, tt, npg):
    assert d % (2 * RW_HEAD) == 0 and (d // LANES) % npg == 0 and s % tt == 0 and tt % CHUNK == 0
    npair = d // LANES
    ngrp = npair // npg
    nt = s // tt
    ntiles = b * ngrp * nt
    nunit = npg * (tt // CHUNK)

    def tile_coords(tile):
        return tile // (ngrp * nt), (tile // nt) % ngrp, tile % nt

    def in_map(i):
        bi, gi, ti = tile_coords(jnp.minimum(i, ntiles - 1))
        return (bi, gi, ti, 0)

    def out_map(i):
        bi, gi, ti = tile_coords(jnp.maximum(i - 1, 0))
        return (bi, ti, gi)

    in_spec = pl.BlockSpec((1, npg, tt, LANES), in_map)
    vec_spec = pl.BlockSpec((8, npg * LANES), lambda i: (0, tile_coords(jnp.minimum(i, ntiles - 1))[1]))
    vec_prev_spec = pl.BlockSpec((8, npg * LANES), lambda i: (0, tile_coords(jnp.maximum(i - 1, 0))[1]))
    L = CHUNK
    return pl.pallas_call(
        functools.partial(_rwkv_core_kernel, tiles_per_seq=nt),
        grid=(ntiles + 1,),
        in_specs=[in_spec] * 7 + [vec_spec, vec_prev_spec],
        out_specs=pl.BlockSpec((1, tt, npg * LANES), out_map),
        out_shape=jax.ShapeDtypeStruct((b, s, d), BF16),
        scratch_shapes=[
            pltpu.VMEM((npg, LANES, LANES), F32),
            pltpu.VMEM((nunit, L, LANES), BF16),
            pltpu.VMEM((nunit, L, LANES), F32),
            pltpu.VMEM((nunit, LANES, LANES), BF16),
            pltpu.VMEM((nunit, LANES, LANES), F32),
            pltpu.VMEM((nunit, 1, LANES), F32),
            pltpu.VMEM((nunit, L, LANES), BF16),
            pltpu.VMEM((nunit, L, LANES), F32),
        ],
        compiler_params=_params(("arbitrary",)),
        name="rwkv_core",
    )(*proj, vecs, vecs)


def _post_ffn_kernel(x_ref, y_ref, mod_ref, fw_ref, wo_ref, wgu_ref, wd_ref, o_ref, act_ref,
                     *, final):
    g1 = mod_ref[0, 2:3, :]
    sh2 = mod_ref[0, 3:4, :]
    sc2 = mod_ref[0, 4:5, :]
    g2 = mod_ref[0, 5:6, :]
    x1 = x_ref[0] + g1 * _dot(y_ref[0], wo_ref[...])
    h = _rms_mod(x1, sc2, sh2).astype(BF16)
    d_ff = act_ref.shape[1]
    for c0 in range(0, d_ff, COL_BLOCK):
        cols = slice(c0, c0 + COL_BLOCK)
        gate = _dot(h, wgu_ref[0, :, cols])
        up = _dot(h, wgu_ref[0, :, d_ff + c0:d_ff + c0 + COL_BLOCK])
        act_ref[:, cols] = (gate * _sigmoid(gate) * up).astype(BF16)
    x2 = x1 + g2 * _dot(act_ref[...], wd_ref[0])
    if final:
        inv = lax.rsqrt(jnp.mean(x2 * x2, axis=-1, keepdims=True) + EPS)
        x2 = x2 * inv * fw_ref[0:1, :]
    o_ref[0] = x2


def _post_ffn(x, y, mod, final_w, w_out, w_gu, w_d, layer, tm, final):
    b, s, d = x.shape
    d_ff = w_d.shape[1]
    assert d_ff % COL_BLOCK == 0 and s % tm == 0
    kern = functools.partial(_post_ffn_kernel, final=final)
    tok = lambda i, j: (i, j, 0)
    layer_spec = lambda w: pl.BlockSpec((1,) + w.shape[1:], lambda i, j: (layer, 0, 0),
                                        pipeline_mode=pl.Buffered(1))
    return pl.pallas_call(
        kern,
        grid=(b, s // tm),
        in_specs=[
            pl.BlockSpec((1, tm, d), tok),
            pl.BlockSpec((1, tm, d), tok),
            pl.BlockSpec((1, 8, d), lambda i, j: (i, 0, 0)),
            _const_spec(final_w.shape),
            _const_spec(w_out.shape),
            layer_spec(w_gu),
            layer_spec(w_d),
        ],
        out_specs=pl.BlockSpec((1, tm, d), tok),
        out_shape=jax.ShapeDtypeStruct((b, s, d), F32),
        scratch_shapes=[pltpu.VMEM((tm, d_ff), BF16)],
        compiler_params=_params(("parallel", "parallel")),
        name="post_ffn_final" if final else "post_ffn",
    )(x, y, mod, final_w, w_out, w_gu, w_d)


def _softcap(z):
    return GATE_CAP * jnp.tanh(z * (1.0 / GATE_CAP))


def _mlstm_proj_kernel(x_ref, xp_ref, mod_ref, cv_ref, gb_ref, wqk_ref, wv_ref,
                       wo_ref, wgi_ref, wgf_ref,
                       q_out, k_out, v_out, og_out, gi_out, gf_out):
    tm = x_ref.shape[1]
    t = pl.program_id(1)
    sh = mod_ref[0, 0:1, :]
    sc = mod_ref[0, 1:2, :]
    h = _rms_mod(x_ref[0], sc, sh).astype(BF16)
    hp = _rms_mod(xp_ref[0], sc, sh).astype(BF16)

    half = wqk_ref.shape[1] // 2
    first_row = lax.broadcasted_iota(jnp.int32, (tm, COL_BLOCK), 0) == 0
    for c0 in range(0, wqk_ref.shape[1], COL_BLOCK):
        c = slice(c0, c0 + COL_BLOCK)
        pqk = _dot(h, wqk_ref[:, c])
        ppv = jnp.where(t > 0, _dot(hp, wqk_ref[:, c]), 0.0)
        acc = pqk * cv_ref[0:1, c]
        acc_prev = ppv * cv_ref[0:1, c]
        for j in range(1, CONV_W):
            wj = cv_ref[j:j + 1, c]
            carry = acc_prev[7:8, :]
            acc = pqk * wj + jnp.where(first_row, carry, pltpu.roll(acc, 1, 0))
            acc_prev = ppv * wj + pltpu.roll(acc_prev, 1, 0)
        conv = acc + cv_ref[CONV_W:CONV_W + 1, c]
        qk = conv * _sigmoid(conv)
        if c0 < half:
            q_out[0, :, c] = (qk * (ML_QK ** -0.5)).astype(q_out.dtype)
        else:
            k_out[0, :, c0 - half:c0 - half + COL_BLOCK] = qk.astype(k_out.dtype)
    for c0 in range(0, wv_ref.shape[1], COL_BLOCK):
        c = slice(c0, c0 + COL_BLOCK)
        v_out[0, :, c] = _dot(h, wv_ref[:, c]).astype(v_out.dtype)
    for c0 in range(0, wo_ref.shape[1], COL_BLOCK):
        c = slice(c0, c0 + COL_BLOCK)
        og_out[0, :, c] = _sigmoid(_dot(h, wo_ref[:, c])).astype(og_out.dtype)

    lane = lax.broadcasted_iota(jnp.int32, gi_out.shape[1:], 1)
    zi = _dot(h, wgi_ref[...]) + gb_ref[0:1, :]
    zf = _dot(h, wgf_ref[...]) + gb_ref[1:2, :]
    gi_out[0] = jnp.where(lane < ML_HEADS, _softcap(zi), 0.0)
    gf_out[0] = jnp.where(lane < ML_HEADS, -_softplus(-_softcap(zf)), 0.0)


def _mlstm_proj(x, mod, cv, gb, weights, tm):
    b, s, d = x.shape
    w_qk, w_v, w_o, w_gi, w_gf = weights
    nqk = w_qk.shape[1] // 2
    nv = w_v.shape[1]
    assert s % tm == 0 and w_qk.shape[1] % COL_BLOCK == 0 and nv % COL_BLOCK == 0
    ng = w_gi.shape[1]
    blk8 = tm // 8
    tok = lambda i, j: (i, j, 0)
    return pl.pallas_call(
        _mlstm_proj_kernel,
        grid=(b, s // tm),
        in_specs=[
            pl.BlockSpec((1, tm, d), tok),
            pl.BlockSpec((1, 8, d), lambda i, j: (i, jnp.maximum(j * blk8 - 1, 0), 0)),
            pl.BlockSpec((1, 8, d), lambda i, j: (i, 0, 0)),
            _const_spec(cv.shape),
            _const_spec(gb.shape),
        ] + [_const_spec(w.shape) for w in weights],
        out_specs=[
            pl.BlockSpec((1, tm, nqk), tok),
            pl.BlockSpec((1, tm, nqk), tok),
            pl.BlockSpec((1, tm, nv), tok),
            pl.BlockSpec((1, tm, nv), tok),
            pl.BlockSpec((1, tm, ng), tok),
            pl.BlockSpec((1, tm, ng), tok),
        ],
        out_shape=[
            jax.ShapeDtypeStruct((b, s, nqk), BF16),
            jax.ShapeDtypeStruct((b, s, nqk), BF16),
            jax.ShapeDtypeStruct((b, s, nv), BF16),
            jax.ShapeDtypeStruct((b, s, nv), BF16),
            jax.ShapeDtypeStruct((b, s, ng), F32),
            jax.ShapeDtypeStruct((b, s, ng), F32),
        ],
        compiler_params=_params(("parallel", "arbitrary")),
        name="mlstm_proj",
    )(x, x, mod, cv, gb, *weights)


def _split3_bf16(z):
    hi = z.astype(BF16)
    r1 = z - hi.astype(F32)
    mid = r1.astype(BF16)
    lo = (r1 - mid.astype(F32)).astype(BF16)
    return jnp.concatenate([hi, mid, lo], axis=1)


def _mlstm_core_kernel(q_ref, k_ref, v_ref, og_ref, gi_ref, gf_ref, hn_ref, e64_ref, e128_ref,
                       o_ref, c_ref, m_ref):
    tt = q_ref.shape[1]
    L = CHUNK
    nch = tt // L
    npair = ML_HEADS // 2
    t = pl.program_id(1)

    @pl.when(t == 0)
    def _():
        c_ref[...] = jnp.zeros_like(c_ref)
        m_ref[...] = jnp.zeros_like(m_ref)

    lane = lax.broadcasted_iota(jnp.int32, (L, LANES), 1)
    rowl = lax.broadcasted_iota(jnp.int32, (L, LANES), 0)
    m0 = lane < ML_QK
    src = lane % L
    rowt = lax.broadcasted_iota(jnp.int32, (tt, tt), 0)
    colt = lax.broadcasted_iota(jnp.int32, (tt, tt), 1)
    tri = (((rowt // L) == (colt // L)) & (rowt >= colt)).astype(BF16)
    row_in_chunk = lax.broadcasted_iota(jnp.int32, (tt, LANES), 0) % L
    row_first = lax.broadcasted_iota(jnp.int32, (LANES, 1), 0) < ML_QK
    ones = jnp.ones((L, LANES), BF16)

    def split_heads(zv):
        zb = zv.astype(BF16)
        zero = jnp.zeros_like(zb)
        return jnp.concatenate([jnp.where(m0, zb, zero), jnp.where(m0, zero, zb)], axis=0)

    gi = gi_ref[0]
    b3 = _dot(tri, _split3_bf16(gf_ref[0]))
    bcum = b3[:, :LANES] + b3[:, LANES:2 * LANES] + b3[:, 2 * LANES:]
    z = gi - bcum
    cmax = z
    shift = 1
    while shift < L:
        cmax = jnp.where(row_in_chunk >= shift, jnp.maximum(cmax, pltpu.roll(cmax, shift, 0)), cmax)
        shift *= 2
    m_intra = bcum + cmax

    m_prev = m_ref[0:1, :]
    mp_rows, mn_rows, gt_rows, s_a = [], [], [], []
    for c in range(nch):
        last = c * L + L - 1
        g_tot = bcum[last:last + 1, :]
        m_loc = g_tot + cmax[last:last + 1, :]
        m_new = jnp.maximum(g_tot + m_prev, m_loc)
        mp_rows.append(jnp.broadcast_to(m_prev, (L, LANES)))
        mn_rows.append(jnp.broadcast_to(m_new, (L, LANES)))
        gt_rows.append(jnp.broadcast_to(g_tot, (L, LANES)))
        s_a.append(jnp.exp(g_tot + m_prev - m_new))
        m_prev = m_new
    m_ref[0:1, :] = m_prev
    inter = bcum + jnp.concatenate(mp_rows, axis=0)
    m_t = jnp.maximum(inter, m_intra)
    w_inter = jnp.exp(inter - m_t)
    kws = jnp.exp(jnp.concatenate(gt_rows, axis=0) + z - jnp.concatenate(mn_rows, axis=0))

    head_lane = lax.broadcasted_iota(jnp.int32, (tt, LANES), 1) < ML_HEADS

    def pack3(val):
        val = jnp.where(head_lane, val, 0.0)
        hi = val.astype(BF16).astype(F32)
        r1 = val - hi
        mid = r1.astype(BF16).astype(F32)
        lo = (r1 - mid).astype(BF16).astype(F32)
        packed = hi + pltpu.roll(mid, ML_HEADS, 1) + pltpu.roll(lo, 2 * ML_HEADS, 1)
        return packed.astype(BF16)

    spread = _dot(jnp.concatenate([pack3(bcum - m_t), pack3(z), pack3(w_inter), pack3(kws)], axis=0),
                  e64_ref[...])
    a_exp, z_exp, wi_exp, kw_exp = (spread[i * tt:(i + 1) * tt] for i in range(4))
    em_exp = _dot(pack3(jnp.exp(-m_t)), e128_ref[...])

    un = []
    for c in range(nch):
        rows = slice(c * L, (c + 1) * L)
        for p in range(npair):
            lanes = slice(p * LANES, (p + 1) * LANES)
            qp = q_ref[0, rows, lanes]
            kp = k_ref[0, rows, lanes]
            v0 = v_ref[0, rows, 2 * p * ML_V:(2 * p + 1) * ML_V]
            v1 = v_ref[0, rows, (2 * p + 1) * ML_V:(2 * p + 2) * ML_V]
            z_src = jnp.sum(jnp.where(rowl == src, z_exp[rows, lanes], 0.0), axis=0, keepdims=True)
            un.append(dict(
                c=c, p=p, rows=rows, qp=qp,
                k2=split_heads(kp),
                expd=jnp.where(rowl >= src, jnp.exp(a_exp[rows, lanes] + z_src), 0.0),
                qs=split_heads(qp * wi_exp[rows, lanes]),
                kw=split_heads(kp * kw_exp[rows, lanes]),
                vst=jnp.concatenate([jnp.concatenate([v0, ones], axis=1),
                                     jnp.concatenate([v1, ones], axis=1)], axis=0),
            ))

    for u in un:
        scores = _dot_nt(u["qp"], u["k2"])
        u["w_intra"] = split_heads(u["expd"] * scores)
    for u in un:
        u["c_loc"] = _dot_tn(u["kw"], u["vst"])

    states = [c_ref[p] for p in range(npair)]
    for u in un:
        c, p = u["c"], u["p"]
        u["c_old"] = states[p].astype(BF16)
        sa = jnp.where(row_first, s_a[c][:, 2 * p:2 * p + 1], s_a[c][:, 2 * p + 1:2 * p + 2])
        states[p] = sa * states[p] + u["c_loc"]
    for p in range(npair):
        c_ref[p] = states[p]

    for u in un:
        u["nd"] = _dot(jnp.concatenate([u["qs"], u["w_intra"]], axis=1),
                       jnp.concatenate([u["c_old"], u["vst"]], axis=0))

    for u in un:
        rows = u["rows"]
        for hh in range(2):
            head = 2 * u["p"] + hh
            cols = slice(head * ML_V, (head + 1) * ML_V)
            num = u["nd"][hh * L:(hh + 1) * L, :LANES]
            den = u["nd"][hh * L:(hh + 1) * L, LANES:]
            den = jnp.maximum(jnp.abs(den), em_exp[rows, cols])
            hv = num / den
            hv = hv * lax.rsqrt(jnp.mean(hv * hv, axis=-1, keepdims=True) + EPS)
            hv = hv * hn_ref[0:1, cols]
            o_ref[0, rows, cols] = (hv * og_ref[0, rows, cols]).astype(o_ref.dtype)


def _expand_matrix(width):
    rows = jnp.arange(LANES)[:, None]
    cols = jnp.arange(ML_HEADS * width)[None, :]
    return ((rows < 3 * ML_HEADS) & (rows % ML_HEADS == cols // width)).astype(BF16)


def _mlstm_core(q, k, v, og, gi, gf, hn_w, tt):
    b, s, dv = v.shape
    assert s % tt == 0 and tt % CHUNK == 0 and dv == ML_HEADS * ML_V
    nqk = q.shape[2]
    ng = gi.shape[2]
    e64 = _expand_matrix(ML_QK)
    e128 = _expand_matrix(ML_V)
    tok = lambda i, j: (i, j, 0)
    return pl.pallas_call(
        _mlstm_core_kernel,
        grid=(b, s // tt),
        in_specs=[
            pl.BlockSpec((1, tt, nqk), tok),
            pl.BlockSpec((1, tt, nqk), tok),
            pl.BlockSpec((1, tt, dv), tok),
            pl.BlockSpec((1, tt, dv), tok),
            pl.BlockSpec((1, tt, ng), tok),
            pl.BlockSpec((1, tt, ng), tok),
            _const_spec(hn_w.shape),
            _const_spec(e64.shape),
            _const_spec(e128.shape),
        ],
        out_specs=pl.BlockSpec((1, tt, dv), tok),
        out_shape=jax.ShapeDtypeStruct((b, s, dv), BF16),
        scratch_shapes=[
            pltpu.VMEM((ML_HEADS // 2, LANES, 2 * LANES), F32),
            pltpu.VMEM((8, LANES), F32),
        ],
        compiler_params=_params(("parallel", "arbitrary")),
        name="mlstm_core",
    )(q, k, v, og, gi, gf, hn_w, e64, e128)


def _pad_rows(a, rows=8):
    return jnp.pad(a, ((0, rows - a.shape[0]), (0, 0)))


def kernel(x, c, ada_w, ada_b, rw_mix, rw_w_in, rw_w0, rw_w2, rw_a0, rw_a2, rw_g2, rw_k_k, rw_k_a, rw_r_k, rw_gn_w, rw_gn_b, rw_w_out, ml_w_in, ml_conv_w, ml_conv_b, ml_b_i, ml_b_f, ml_hn_w, ml_w_out, ffn_w_gu, ffn_w_down, final_w):
    b, s, d = x.shape
    bf = lambda w: w.astype(BF16)
    final_w2 = final_w.reshape(1, d)

    mod = _adaln_mod(c, ada_w, ada_b)

    w_in = rw_w_in[0]
    o1 = d
    o2 = o1 + LORA_W
    o3 = o2 + d
    o4 = o3 + d
    o5 = o4 + LORA_A
    rw_weights = [bf(w_in[:, :o1]), bf(w_in[:, o2:o3]), bf(w_in[:, o3:o4]),
                  bf(w_in[:, o1:o2]), bf(w_in[:, o4:o5]), bf(w_in[:, o5:]),
                  bf(rw_w2[0]), bf(rw_a2[0]), bf(rw_g2[0])]
    mix = _pad_rows(rw_mix[0])
    vecs = _pad_rows(jnp.stack([rw_w0[0], rw_a0[0], rw_k_k[0], rw_k_a[0]]))
    proj = _rwkv_proj(x, mod[0], mix, vecs, rw_weights, TM_RWKV_PROJ)
    core_vecs = _pad_rows(jnp.stack([rw_r_k[0].reshape(d), rw_gn_w[0], rw_gn_b[0]]))
    y = _rwkv_core(proj, core_vecs, b, s, d, TT_RWKV, PAIRS_RWKV)
    w_gu = bf(ffn_w_gu)
    w_down = bf(ffn_w_down)
    x = _post_ffn(x, y, mod[0], final_w2, bf(rw_w_out[0]), w_gu, w_down, 0, TM_DENSE, False)

    w_in = ml_w_in[0]
    nqk = 2 * ML_HEADS * ML_QK
    nv = ML_HEADS * ML_V
    pad_lanes = lambda w: jnp.pad(w, ((0, 0), (0, LANES - w.shape[1])))
    o_g = nqk + 2 * nv
    ml_weights = [bf(w_in[:, :nqk]), bf(w_in[:, nqk:nqk + nv]), bf(w_in[:, nqk + nv:o_g]),
                  bf(pad_lanes(w_in[:, o_g:o_g + ML_HEADS])), bf(pad_lanes(w_in[:, o_g + ML_HEADS:]))]
    cv = _pad_rows(jnp.concatenate([ml_conv_w[0], ml_conv_b[0][None, :]], axis=0))
    gb = _pad_rows(pad_lanes(jnp.stack([ml_b_i[0], ml_b_f[0]])))
    q, k, v, og, gi, gf = _mlstm_proj(x, mod[1], cv, gb, ml_weights, TM_DENSE)
    y = _mlstm_core(q, k, v, og, gi, gf, ml_hn_w[0].reshape(1, nv), TT_MLSTM)
    x = _post_ffn(x, y, mod[1], final_w2, bf(ml_w_out[0]), w_gu, w_down, 1, TM_DENSE, True)
    return x
```

```python
import functools

import jax
import jax.numpy as jnp
from jax import lax
from jax.experimental import pallas as pl
from jax.experimental.pallas import tpu as pltpu

F32 = jnp.float32
BF16 = jnp.bfloat16

EPS = 1e-6
GN_EPS = 64e-5
GATE_CAP = 15.0
DECAY_SCALE = 0.6065306597126334
CHUNK = 64
LANES = 128
RW_HEAD = 64
LORA_W = 64
LORA_A = 64
LORA_G = 160
ML_HEADS = 8
ML_QK = 64
ML_V = 128
CONV_W = 4
COL_BLOCK = 256
VMEM_LIMIT = 56 * 1024 * 1024
TM_RWKV_PROJ = 512
TM_DENSE = 1024
TT_RWKV = 256
PAIRS_RWKV = 8
TT_MLSTM = 512

NT_DIMS = (((1,), (1,)), ((), ()))
TN_DIMS = (((0,), (0,)), ((), ()))


def _dot(a, b):
    return jnp.dot(a, b, preferred_element_type=F32)


def _dot_nt(a, b):
    return lax.dot_general(a, b, NT_DIMS, preferred_element_type=F32)


def _dot_tn(a, b):
    return lax.dot_general(a, b, TN_DIMS, preferred_element_type=F32)


def _sigmoid(z):
    return 0.5 * jnp.tanh(0.5 * z) + 0.5


def _softplus(z):
    return jnp.maximum(z, 0.0) + jnp.log(1.0 + jnp.exp(-jnp.abs(z)))


def _rms_mod(xv, sc, sh):
    inv = lax.rsqrt(jnp.mean(xv * xv, axis=-1, keepdims=True) + EPS)
    return xv * inv * (1.0 + sc) + sh


def _split_bf16(z):
    hi = z.astype(BF16)
    lo = (z - hi.astype(F32)).astype(BF16)
    return hi, lo


def _params(sem):
    return pltpu.CompilerParams(dimension_semantics=sem, vmem_limit_bytes=VMEM_LIMIT)


def _const_spec(shape):
    nd = len(shape)
    return pl.BlockSpec(shape, lambda *_: (0,) * nd, pipeline_mode=pl.Buffered(1))


def _mod_kernel(c_ref, w_ref, b_ref, o_ref):
    c = c_ref[...]
    ca = (c * _sigmoid(c)).astype(BF16)
    o_ref[0] = _dot(ca, w_ref[0].astype(BF16)) + b_ref[0]


def _adaln_mod(c, ada_w, ada_b):
    depth, d, d6 = ada_w.shape
    b = c.shape[0]
    bp = 8
    assert b <= bp and d6 % d == 0
    c_pad = jnp.zeros((bp, d), F32).at[:b].set(c)
    nblk = d6 // d
    out = pl.pallas_call(
        _mod_kernel,
        grid=(depth, nblk),
        in_specs=[
            pl.BlockSpec((bp, d), lambda i, j: (0, 0)),
            pl.BlockSpec((1, d, d), lambda i, j: (i, 0, j)),
            pl.BlockSpec((1, 1, d), lambda i, j: (i, 0, j)),
        ],
        out_specs=pl.BlockSpec((1, bp, d), lambda i, j: (i, 0, j)),
        out_shape=jax.ShapeDtypeStruct((depth, bp, d6), F32),
        compiler_params=_params(("arbitrary", "arbitrary")),
        name="adaln_mod",
    )(c_pad, ada_w, ada_b.reshape(depth, 1, d6))
    mod = out[:, :b].reshape(depth, b, nblk, d)
    return jnp.pad(mod, ((0, 0), (0, 0), (0, 8 - nblk), (0, 0)))


def _rwkv_proj_kernel(x_ref, xp_ref, mod_ref, mix_ref, vec_ref, wr_ref, wk_ref, wv_ref,
                      wwl_ref, wal_ref, wgl_ref, w2_ref, a2_ref, g2_ref,
                      r_out, k_out, v_out, kk_out, a_out, lw_out, g_out, hbuf, xbuf):
    tm = x_ref.shape[1]
    t = pl.program_id(1)
    sh = mod_ref[0, 0:1, :]
    sc = mod_ref[0, 1:2, :]
    hp = _rms_mod(xp_ref[0], sc, sh)
    hbuf[0:8, :] = jnp.where(t > 0, hp, 0.0)
    w0 = vec_ref[0:1, :]
    a0 = vec_ref[1:2, :]
    k_k = vec_ref[2:3, :]
    k_a = vec_ref[3:4, :]

    h = _rms_mod(x_ref[0], sc, sh)
    hbuf[8:, :] = h
    xbuf[...] = hbuf[pl.ds(7, tm), :] - h

    def mixed(i):
        return (hbuf[8:, :] + xbuf[...] * mix_ref[i:i + 1, :]).astype(BF16)

    def put(out, val, cb):
        for q in range(COL_BLOCK // LANES):
            out[0, cb * (COL_BLOCK // LANES) + q] = val[:, q * LANES:(q + 1) * LANES].astype(out.dtype)

    ncb = r_out.shape[1] * LANES // COL_BLOCK
    col = lambda cb: slice(cb * COL_BLOCK, (cb + 1) * COL_BLOCK)
    lhs = mixed(0)
    for cb in range(ncb):
        put(r_out, _dot(lhs, wr_ref[:, col(cb)]), cb)
    lhs = mixed(3)
    for cb in range(ncb):
        put(v_out, _dot(lhs, wv_ref[:, col(cb)]), cb)
    twl = jnp.tanh(_dot(mixed(1), wwl_ref[...])).astype(BF16)
    alb = _dot(mixed(4), wal_ref[...]).astype(BF16)
    sgl = _sigmoid(_dot(mixed(5), wgl_ref[...])).astype(BF16)
    a_blocks = []
    for cb in range(ncb):
        c = col(cb)
        put(lw_out, -DECAY_SCALE * _sigmoid(w0[:, c] + _dot(twl, w2_ref[:, c])), cb)
        a = _sigmoid(a0[:, c] + _dot(alb, a2_ref[:, c]))
        put(a_out, a, cb)
        a_blocks.append(a)
        put(g_out, _dot(sgl, g2_ref[:, c]), cb)
    lhs = mixed(2)
    for cb in range(ncb):
        c = col(cb)
        k = _dot(lhs, wk_ref[:, c])
        put(kk_out, k * k_k[:, c], cb)
        put(k_out, k * (1.0 + (a_blocks[cb] - 1.0) * k_a[:, c]), cb)


def _rwkv_proj(x, mod, mix, vecs, weights, tm):
    b, s, d = x.shape
    assert s % tm == 0 and d % COL_BLOCK == 0
    npair = d // LANES
    nt = s // tm
    blk8 = tm // 8
    pair_shape = lambda dt: jax.ShapeDtypeStruct((b, npair, s, LANES), dt)
    pair_spec = pl.BlockSpec((1, npair, tm, LANES), lambda i, j: (i, 0, j, 0))
    return pl.pallas_call(
        _rwkv_proj_kernel,
        grid=(b, nt),
        in_specs=[
            pl.BlockSpec((1, tm, d), lambda i, j: (i, j, 0)),
            pl.BlockSpec((1, 8, d), lambda i, j: (i, jnp.maximum(j * blk8 - 1, 0), 0)),
            pl.BlockSpec((1, 8, d), lambda i, j: (i, 0, 0)),
            _const_spec(mix.shape),
            _const_spec(vecs.shape),
        ] + [_const_spec(w.shape) for w in weights],
        out_specs=[pair_spec] * 7,
        out_shape=[pair_shape(BF16)] * 5 + [pair_shape(F32), pair_shape(BF16)],
        scratch_shapes=[pltpu.VMEM((tm + 8, d), F32), pltpu.VMEM((tm, d), F32)],
        compiler_params=_params(("parallel", "arbitrary")),
        name="rwkv_proj",
    )(x, x, mod, mix, vecs, *weights)


def _rwkv_core_kernel(r_ref, k_ref, v_ref, kk_ref, a_ref, lw_ref, g_ref, vec_ref, vec_prev_ref,
                      o_ref, s_ref, rbar_s, y0_s, mlrt_s, s0_s, wl_s, g_s, bg_s, *, tiles_per_seq):
    npg = r_ref.shape[1]
    tt = r_ref.shape[2]
    L = CHUNK
    nch = tt // L
    nunit = npg * nch
    step = pl.program_id(0)

    @pl.when(step == 0)
    def _():
        for ref in (s_ref, rbar_s, y0_s, mlrt_s, s0_s, wl_s, g_s, bg_s):
            ref[...] = jnp.zeros_like(ref)

    fresh = (jnp.maximum(step - 1, 0) % tiles_per_seq) == 0

    lane = lax.broadcasted_iota(jnp.int32, (L, LANES), 1)
    m0 = lane < RW_HEAD
    rowl = lax.broadcasted_iota(jnp.int32, (L, LANES), 0)
    strict = rowl > (lane % L)
    incl = rowl >= (lane % L)
    SUB = 8
    sub_block = (rowl // SUB) == ((lane % L) // SUB)
    eye_cat = (rowl == (lane % L)).astype(F32)
    row2 = lax.broadcasted_iota(jnp.int32, (LANES, LANES), 0)
    col2 = lax.broadcasted_iota(jnp.int32, (LANES, LANES), 1)
    blockdiag = (row2 // RW_HEAD) == (col2 // RW_HEAD)
    rowt = lax.broadcasted_iota(jnp.int32, (tt, tt), 0)
    colt = lax.broadcasted_iota(jnp.int32, (tt, tt), 1)
    tri = (((rowt // L) == (colt // L)) & (rowt >= colt)).astype(BF16)
    zeros_l = jnp.zeros((L, LANES), BF16)

    def head_sum(z):
        lo = jnp.sum(jnp.where(m0, z, 0.0), axis=-1, keepdims=True)
        hi = jnp.sum(jnp.where(m0, 0.0, z), axis=-1, keepdims=True)
        return jnp.where(m0, lo, hi)

    def split_heads(z):
        zb = z.astype(BF16)
        zero = jnp.zeros_like(zb)
        return jnp.concatenate([jnp.where(m0, zb, zero), jnp.where(m0, zero, zb)], axis=0)

    def cat_dot3(xc, yc):
        xh, xl = _split_bf16(xc)
        yh, yl = _split_bf16(yc)
        lhs = jnp.concatenate([xh, xh, xl], axis=1)
        rhs = jnp.concatenate([split_heads(yh), split_heads(yl), split_heads(yh)], axis=0)
        return _dot(lhs, rhs)

    def neumann(units, src, dst, n_levels):
        for u in units:
            u[dst] = eye_cat + u[src]
        pw = [cat_dot3(u[src], u[src]) for u in units]
        for level in range(1, n_levels):
            for i, u in enumerate(units):
                if level == n_levels - 1:
                    u[dst] = u[dst] + cat_dot3(u[dst], pw[i])
                else:
                    both = cat_dot3(jnp.concatenate([u[dst], pw[i]], axis=0), pw[i])
                    u[dst] = u[dst] + both[:L]
                    pw[i] = both[L:]

    units = [(p, c) for p in range(npg) for c in range(nch)]

    cums = []
    for p in range(npg):
        hi, lo = _split_bf16(lw_ref[0, p])
        both = _dot(tri, jnp.concatenate([hi, lo], axis=1))
        cums.append(both[:, :LANES] + both[:, LANES:])

    pu = []
    for (p, c) in units:
        rows = slice(c * L, (c + 1) * L)
        r = r_ref[0, p, rows, :].astype(F32)
        k = k_ref[0, p, rows, :].astype(F32)
        v = v_ref[0, p, rows, :]
        kk = kk_ref[0, p, rows, :].astype(F32)
        a = a_ref[0, p, rows, :].astype(F32)
        g = g_ref[0, p, rows, :]
        lw = lw_ref[0, p, rows, :]
        cum = cums[p][rows, :]
        kkn = kk * jnp.minimum(lax.rsqrt(head_sum(kk * kk)), 1e12)
        bvec = kkn * a
        c_last = cum[L - 1:L, :]
        at = -kkn * jnp.exp(cum - lw)
        rt = r * jnp.exp(cum)
        inv_w = jnp.exp(-cum)
        d_last = jnp.exp(c_last - cum)
        lanes = slice(p * LANES, (p + 1) * LANES)
        bonus = head_sum(r * k * vec_ref[0:1, lanes]) * v.astype(F32)
        pu.append(dict(
            rt=rt, v=v, w_last=jnp.exp(c_last), g=g, bg=bonus * g.astype(F32),
            lhs=jnp.concatenate([at, rt], axis=0).astype(BF16),
            rhs=jnp.concatenate([split_heads(bvec * inv_w), split_heads(k * inv_w)], axis=0),
            dk=jnp.concatenate([bvec * d_last, k * d_last], axis=0).astype(BF16),
            at_s=split_heads(at), v_s=split_heads(v),
        ))

    states = [jnp.where(fresh, 0.0, s_ref[p]) for p in range(npg)]
    ys = {}

    def scan_level(c):
        for p in range(npg):
            j = p * nch + c
            s_old = states[p]
            sb = s_old.astype(BF16)
            ys[j] = _dot_nt(rbar_s[j], sb) + y0_s[j]
            states[p] = s_old * wl_s[j] + _dot_nt(sb, mlrt_s[j]) + s0_s[j]

    per_level = -(-nunit // nch)
    for j, u in enumerate(pu):
        if j % per_level == 0:
            scan_level(j // per_level)
        gram = _dot_nt(u["lhs"], u["rhs"])
        a_ab = jnp.where(strict, gram[:L, :LANES], 0.0)
        u["a_diag"] = jnp.where(sub_block, a_ab, 0.0)
        u["a_off"] = jnp.where(sub_block, 0.0, a_ab)
        u["r_b"] = jnp.where(incl, gram[L:, :LANES], 0.0).astype(BF16)
        u["ak_rk"] = jnp.concatenate([jnp.where(strict, gram[:L, LANES:], 0.0),
                                      jnp.where(incl, gram[L:, LANES:], 0.0)], axis=0).astype(BF16)
    for p in range(npg):
        s_ref[p] = states[p]

    for j in range(nunit):
        p, c = divmod(j, nch)
        lanes = slice(p * LANES, (p + 1) * LANES)
        y = ys[j]
        mu = head_sum(y) * (1.0 / RW_HEAD)
        yc = y - mu
        var = head_sum(yc * yc) * (1.0 / RW_HEAD)
        yn = yc * lax.rsqrt(var + GN_EPS) * vec_prev_ref[1:2, lanes] + vec_prev_ref[2:3, lanes]
        o_ref[0, c * L:(c + 1) * L, lanes] = (yn * g_s[j].astype(F32) + bg_s[j]).astype(o_ref.dtype)

    for u in pu:
        both = _dot(u["ak_rk"], u["v_s"])
        u["akv"] = both[:L]
        u["rkv"] = both[L:]

    neumann(pu, "a_diag", "t_diag", SUB.bit_length() - 1)
    for u in pu:
        u["n_off"] = cat_dot3(u["t_diag"], u["a_off"])
    neumann(pu, "n_off", "t_off", (L // SUB).bit_length() - 1)
    for u in pu:
        u["tinv"] = _dot(u["t_off"].astype(BF16), split_heads(u["t_diag"]))

    for u in pu:
        xin = jnp.concatenate([u["at_s"], split_heads(u["akv"])], axis=1)
        u["z"] = _dot(u["tinv"].astype(BF16), xin)
    for u in pu:
        z = u["z"]
        rhs = jnp.concatenate([split_heads(z[:, :LANES]), split_heads(z[:, LANES:])], axis=1)
        u["w"] = _dot(u["r_b"], rhs)
    for u in pu:
        bot = jnp.concatenate([zeros_l, u["v"]], axis=1)
        u["ms"] = _dot_tn(u["dk"], jnp.concatenate([u["z"].astype(BF16), bot], axis=0))

    for j, u in enumerate(pu):
        rbar_s[j] = (u["rt"] + u["w"][:, :LANES]).astype(BF16)
        y0_s[j] = u["w"][:, LANES:] + u["rkv"]
        mlrt_s[j] = jnp.where(blockdiag, u["ms"][:, :LANES], 0.0).astype(BF16)
        s0_s[j] = jnp.where(blockdiag, u["ms"][:, LANES:].T, 0.0)
        wl_s[j] = u["w_last"]
        g_s[j] = u["g"]
        bg_s[j] = u["bg"]


def _rwkv_core(proj, vecs, b, s, d, tt, npg):
    assert d % (2 * RW_HEAD) == 0 and (d // LANES) % npg == 0 and s % tt == 0 and tt % CHUNK == 0
    npair = d // LANES
    ngrp = npair // npg
    nt = s // tt
    ntiles = b * ngrp * nt
    nunit = npg * (tt // CHUNK)

    def tile_coords(tile):
        return tile // (ngrp * nt), (tile // nt) % ngrp, tile % nt

    def in_map(i):
        bi, gi, ti = tile_coords(jnp.minimum(i, ntiles - 1))
        return (bi, gi, ti, 0)

    def out_map(i):
        bi, gi, ti = tile_coords(jnp.maximum(i - 1, 0))
        return (bi, ti, gi)

    in_spec = pl.BlockSpec((1, npg, tt, LANES), in_map)
    vec_spec = pl.BlockSpec((8, npg * LANES), lambda i: (0, tile_coords(jnp.minimum(i, ntiles - 1))[1]))
    vec_prev_spec = pl.BlockSpec((8, npg * LANES), lambda i: (0, tile_coords(jnp.maximum(i - 1, 0))[1]))
    L = CHUNK
    return pl.pallas_call(
        functools.partial(_rwkv_core_kernel, tiles_per_seq=nt),
        grid=(ntiles + 1,),
        in_specs=[in_spec] * 7 + [vec_spec, vec_prev_spec],
        out_specs=pl.BlockSpec((1, tt, npg * LANES), out_map),
        out_shape=jax.ShapeDtypeStruct((b, s, d), BF16),
        scratch_shapes=[
            pltpu.VMEM((npg, LANES, LANES), F32),
            pltpu.VMEM((nunit, L, LANES), BF16),
            pltpu.VMEM((nunit, L, LANES), F32),
            pltpu.VMEM((nunit, LANES, LANES), BF16),
            pltpu.VMEM((nunit, LANES, LANES), F32),
            pltpu.VMEM((nunit, 1, LANES), F32),
            pltpu.VMEM((nunit, L, LANES), BF16),
            pltpu.VMEM((nunit, L, LANES), F32),
        ],
        compiler_params=_params(("arbitrary",)),
        name="rwkv_core",
    )(*proj, vecs, vecs)


def _post_ffn_kernel(x_ref, y_ref, mod_ref, fw_ref, wo_ref, wgu_ref, wd_ref, o_ref, act_ref,
                     *, final):
    g1 = mod_ref[0, 2:3, :]
    sh2 = mod_ref[0, 3:4, :]
    sc2 = mod_ref[0, 4:5, :]
    g2 = mod_ref[0, 5:6, :]
    x1 = x_ref[0] + g1 * _dot(y_ref[0], wo_ref[...])
    h = _rms_mod(x1, sc2, sh2).astype(BF16)
    d_ff = act_ref.shape[1]
    for c0 in range(0, d_ff, COL_BLOCK):
        cols = slice(c0, c0 + COL_BLOCK)
        gate = _dot(h, wgu_ref[0, :, cols])
        up = _dot(h, wgu_ref[0, :, d_ff + c0:d_ff + c0 + COL_BLOCK])
        act_ref[:, cols] = (gate * _sigmoid(gate) * up).astype(BF16)
    x2 = x1 + g2 * _dot(act_ref[...], wd_ref[0])
    if final:
        inv = lax.rsqrt(jnp.mean(x2 * x2, axis=-1, keepdims=True) + EPS)
        x2 = x2 * inv * fw_ref[0:1, :]
    o_ref[0] = x2


def _post_ffn(x, y, mod, final_w, w_out, w_gu, w_d, layer, tm, final):
    b, s, d = x.shape
    d_ff = w_d.shape[1]
    assert d_ff % COL_BLOCK == 0 and s % tm == 0
    kern = functools.partial(_post_ffn_kernel, final=final)
    tok = lambda i, j: (i, j, 0)
    layer_spec = lambda w: pl.BlockSpec((1,) + w.shape[1:], lambda i, j: (layer, 0, 0),
                                        pipeline_mode=pl.Buffered(1))
    return pl.pallas_call(
        kern,
        grid=(b, s // tm),
        in_specs=[
            pl.BlockSpec((1, tm, d), tok),
            pl.BlockSpec((1, tm, d), tok),
            pl.BlockSpec((1, 8, d), lambda i, j: (i, 0, 0)),
            _const_spec(final_w.shape),
            _const_spec(w_out.shape),
            layer_spec(w_gu),
            layer_spec(w_d),
        ],
        out_specs=pl.BlockSpec((1, tm, d), tok),
        out_shape=jax.ShapeDtypeStruct((b, s, d), F32),
        scratch_shapes=[pltpu.VMEM((tm, d_ff), BF16)],
        compiler_params=_params(("parallel", "parallel")),
        name="post_ffn_final" if final else "post_ffn",
    )(x, y, mod, final_w, w_out, w_gu, w_d)


def _softcap(z):
    return GATE_CAP * jnp.tanh(z * (1.0 / GATE_CAP))


def _mlstm_proj_kernel(x_ref, xp_ref, mod_ref, cv_ref, gb_ref, wqk_ref, wv_ref,
                       wo_ref, wgi_ref, wgf_ref,
                       q_out, k_out, v_out, og_out, gi_out, gf_out):
    tm = x_ref.shape[1]
    t = pl.program_id(1)
    sh = mod_ref[0, 0:1, :]
    sc = mod_ref[0, 1:2, :]
    h = _rms_mod(x_ref[0], sc, sh).astype(BF16)
    hp = _rms_mod(xp_ref[0], sc, sh).astype(BF16)

    half = wqk_ref.shape[1] // 2
    first_row = lax.broadcasted_iota(jnp.int32, (tm, COL_BLOCK), 0) == 0
    for c0 in range(0, wqk_ref.shape[1], COL_BLOCK):
        c = slice(c0, c0 + COL_BLOCK)
        pqk = _dot(h, wqk_ref[:, c])
        ppv = jnp.where(t > 0, _dot(hp, wqk_ref[:, c]), 0.0)
        acc = pqk * cv_ref[0:1, c]
        acc_prev = ppv * cv_ref[0:1, c]
        for j in range(1, CONV_W):
            wj = cv_ref[j:j + 1, c]
            carry = acc_prev[7:8, :]
            acc = pqk * wj + jnp.where(first_row, carry, pltpu.roll(acc, 1, 0))
            acc_prev = ppv * wj + pltpu.roll(acc_prev, 1, 0)
        conv = acc + cv_ref[CONV_W:CONV_W + 1, c]
        qk = conv * _sigmoid(conv)
        if c0 < half:
            q_out[0, :, c] = (qk * (ML_QK ** -0.5)).astype(q_out.dtype)
        else:
            k_out[0, :, c0 - half:c0 - half + COL_BLOCK] = qk.astype(k_out.dtype)
    for c0 in range(0, wv_ref.shape[1], COL_BLOCK):
        c = slice(c0, c0 + COL_BLOCK)
        v_out[0, :, c] = _dot(h, wv_ref[:, c]).astype(v_out.dtype)
    for c0 in range(0, wo_ref.shape[1], COL_BLOCK):
        c = slice(c0, c0 + COL_BLOCK)
        og_out[0, :, c] = _sigmoid(_dot(h, wo_ref[:, c])).astype(og_out.dtype)

    lane = lax.broadcasted_iota(jnp.int32, gi_out.shape[1:], 1)
    zi = _dot(h, wgi_ref[...]) + gb_ref[0:1, :]
    zf = _dot(h, wgf_ref[...]) + gb_ref[1:2, :]
    gi_out[0] = jnp.where(lane < ML_HEADS, _softcap(zi), 0.0)
    gf_out[0] = jnp.where(lane < ML_HEADS, -_softplus(-_softcap(zf)), 0.0)


def _mlstm_proj(x, mod, cv, gb, weights, tm):
    b, s, d = x.shape
    w_qk, w_v, w_o, w_gi, w_gf = weights
    nqk = w_qk.shape[1] // 2
    nv = w_v.shape[1]
    assert s % tm == 0 and w_qk.shape[1] % COL_BLOCK == 0 and nv % COL_BLOCK == 0
    ng = w_gi.shape[1]
    blk8 = tm // 8
    tok = lambda i, j: (i, j, 0)
    return pl.pallas_call(
        _mlstm_proj_kernel,
        grid=(b, s // tm),
        in_specs=[
            pl.BlockSpec((1, tm, d), tok),
            pl.BlockSpec((1, 8, d), lambda i, j: (i, jnp.maximum(j * blk8 - 1, 0), 0)),
            pl.BlockSpec((1, 8, d), lambda i, j: (i, 0, 0)),
            _const_spec(cv.shape),
            _const_spec(gb.shape),
        ] + [_const_spec(w.shape) for w in weights],
        out_specs=[
            pl.BlockSpec((1, tm, nqk), tok),
            pl.BlockSpec((1, tm, nqk), tok),
            pl.BlockSpec((1, tm, nv), tok),
            pl.BlockSpec((1, tm, nv), tok),
            pl.BlockSpec((1, tm, ng), tok),
            pl.BlockSpec((1, tm, ng), tok),
        ],
        out_shape=[
            jax.ShapeDtypeStruct((b, s, nqk), BF16),
            jax.ShapeDtypeStruct((b, s, nqk), BF16),
            jax.ShapeDtypeStruct((b, s, nv), BF16),
            jax.ShapeDtypeStruct((b, s, nv), BF16),
            jax.ShapeDtypeStruct((b, s, ng), F32),
            jax.ShapeDtypeStruct((b, s, ng), F32),
        ],
        compiler_params=_params(("parallel", "arbitrary")),
        name="mlstm_proj",
    )(x, x, mod, cv, gb, *weights)


def _split3_bf16(z):
    hi = z.astype(BF16)
    r1 = z - hi.astype(F32)
    mid = r1.astype(BF16)
    lo = (r1 - mid.astype(F32)).astype(BF16)
    return jnp.concatenate([hi, mid, lo], axis=1)


def _mlstm_core_kernel(q_ref, k_ref, v_ref, og_ref, gi_ref, gf_ref, hn_ref, e64_ref, e128_ref,
                       o_ref, c_ref, m_ref):
    tt = q_ref.shape[1]
    L = CHUNK
    nch = tt // L
    npair = ML_HEADS // 2
    t = pl.program_id(1)

    @pl.when(t == 0)
    def _():
        c_ref[...] = jnp.zeros_like(c_ref)
        m_ref[...] = jnp.zeros_like(m_ref)

    lane = lax.broadcasted_iota(jnp.int32, (L, LANES), 1)
    rowl = lax.broadcasted_iota(jnp.int32, (L, LANES), 0)
    m0 = lane < ML_QK
    src = lane % L
    rowt = lax.broadcasted_iota(jnp.int32, (tt, tt), 0)
    colt = lax.broadcasted_iota(jnp.int32, (tt, tt), 1)
    tri = (((rowt // L) == (colt // L)) & (rowt >= colt)).astype(BF16)
    row_in_chunk = lax.broadcasted_iota(jnp.int32, (tt, LANES), 0) % L
    row_first = lax.broadcasted_iota(jnp.int32, (LANES, 1), 0) < ML_QK
    ones = jnp.ones((L, LANES), BF16)

    def split_heads(zv):
        zb = zv.astype(BF16)
        zero = jnp.zeros_like(zb)
        return jnp.concatenate([jnp.where(m0, zb, zero), jnp.where(m0, zero, zb)], axis=0)

    gi = gi_ref[0]
    b3 = _dot(tri, _split3_bf16(gf_ref[0]))
    bcum = b3[:, :LANES] + b3[:, LANES:2 * LANES] + b3[:, 2 * LANES:]
    z = gi - bcum
    cmax = z
    shift = 1
    while shift < L:
        cmax = jnp.where(row_in_chunk >= shift, jnp.maximum(cmax, pltpu.roll(cmax, shift, 0)), cmax)
        shift *= 2
    m_intra = bcum + cmax

    m_prev = m_ref[0:1, :]
    mp_rows, mn_rows, gt_rows, s_a = [], [], [], []
    for c in range(nch):
        last = c * L + L - 1
        g_tot = bcum[last:last + 1, :]
        m_loc = g_tot + cmax[last:last + 1, :]
        m_new = jnp.maximum(g_tot + m_prev, m_loc)
        mp_rows.append(jnp.broadcast_to(m_prev, (L, LANES)))
        mn_rows.append(jnp.broadcast_to(m_new, (L, LANES)))
        gt_rows.append(jnp.broadcast_to(g_tot, (L, LANES)))
        s_a.append(jnp.exp(g_tot + m_prev - m_new))
        m_prev = m_new
    m_ref[0:1, :] = m_prev
    inter = bcum + jnp.concatenate(mp_rows, axis=0)
    m_t = jnp.maximum(inter, m_intra)
    w_inter = jnp.exp(inter - m_t)
    kws = jnp.exp(jnp.concatenate(gt_rows, axis=0) + z - jnp.concatenate(mn_rows, axis=0))

    head_lane = lax.broadcasted_iota(jnp.int32, (tt, LANES), 1) < ML_HEADS

    def pack3(val):
        val = jnp.where(head_lane, val, 0.0)
        hi = val.astype(BF16).astype(F32)
        r1 = val - hi
        mid = r1.astype(BF16).astype(F32)
        lo = (r1 - mid).astype(BF16).astype(F32)
        packed = hi + pltpu.roll(mid, ML_HEADS, 1) + pltpu.roll(lo, 2 * ML_HEADS, 1)
        return packed.astype(BF16)

    spread = _dot(jnp.concatenate([pack3(bcum - m_t), pack3(z), pack3(w_inter), pack3(kws)], axis=0),
                  e64_ref[...])
    a_exp, z_exp, wi_exp, kw_exp = (spread[i * tt:(i + 1) * tt] for i in range(4))
    em_exp = _dot(pack3(jnp.exp(-m_t)), e128_ref[...])

    un = []
    for c in range(nch):
        rows = slice(c * L, (c + 1) * L)
        for p in range(npair):
            lanes = slice(p * LANES, (p + 1) * LANES)
            qp = q_ref[0, rows, lanes]
            kp = k_ref[0, rows, lanes]
            v0 = v_ref[0, rows, 2 * p * ML_V:(2 * p + 1) * ML_V]
            v1 = v_ref[0, rows, (2 * p + 1) * ML_V:(2 * p + 2) * ML_V]
            z_src = jnp.sum(jnp.where(rowl == src, z_exp[rows, lanes], 0.0), axis=0, keepdims=True)
            un.append(dict(
                c=c, p=p, rows=rows, qp=qp,
                k2=split_heads(kp),
                expd=jnp.where(rowl >= src, jnp.exp(a_exp[rows, lanes] + z_src), 0.0),
                qs=split_heads(qp * wi_exp[rows, lanes]),
                kw=split_heads(kp * kw_exp[rows, lanes]),
                vst=jnp.concatenate([jnp.concatenate([v0, ones], axis=1),
                                     jnp.concatenate([v1, ones], axis=1)], axis=0),
            ))

    for u in un:
        scores = _dot_nt(u["qp"], u["k2"])
        u["w_intra"] = split_heads(u["expd"] * scores)
    for u in un:
        u["c_loc"] = _dot_tn(u["kw"], u["vst"])

    states = [c_ref[p] for p in range(npair)]
    for u in un:
        c, p = u["c"], u["p"]
        u["c_old"] = states[p].astype(BF16)
        sa = jnp.where(row_first, s_a[c][:, 2 * p:2 * p + 1], s_a[c][:, 2 * p + 1:2 * p + 2])
        states[p] = sa * states[p] + u["c_loc"]
    for p in range(npair):
        c_ref[p] = states[p]

    for u in un:
        u["nd"] = _dot(jnp.concatenate([u["qs"], u["w_intra"]], axis=1),
                       jnp.concatenate([u["c_old"], u["vst"]], axis=0))

    for u in un:
        rows = u["rows"]
        for hh in range(2):
            head = 2 * u["p"] + hh
            cols = slice(head * ML_V, (head + 1) * ML_V)
            num = u["nd"][hh * L:(hh + 1) * L, :LANES]
            den = u["nd"][hh * L:(hh + 1) * L, LANES:]
            den = jnp.maximum(jnp.abs(den), em_exp[rows, cols])
            hv = num / den
            hv = hv * lax.rsqrt(jnp.mean(hv * hv, axis=-1, keepdims=True) + EPS)
            hv = hv * hn_ref[0:1, cols]
            o_ref[0, rows, cols] = (hv * og_ref[0, rows, cols]).astype(o_ref.dtype)


def _expand_matrix(width):
    rows = jnp.arange(LANES)[:, None]
    cols = jnp.arange(ML_HEADS * width)[None, :]
    return ((rows < 3 * ML_HEADS) & (rows % ML_HEADS == cols // width)).astype(BF16)


def _mlstm_core(q, k, v, og, gi, gf, hn_w, tt):
    b, s, dv = v.shape
    assert s % tt == 0 and tt % CHUNK == 0 and dv == ML_HEADS * ML_V
    nqk = q.shape[2]
    ng = gi.shape[2]
    e64 = _expand_matrix(ML_QK)
    e128 = _expand_matrix(ML_V)
    tok = lambda i, j: (i, j, 0)
    return pl.pallas_call(
        _mlstm_core_kernel,
        grid=(b, s // tt),
        in_specs=[
            pl.BlockSpec((1, tt, nqk), tok),
            pl.BlockSpec((1, tt, nqk), tok),
            pl.BlockSpec((1, tt, dv), tok),
            pl.BlockSpec((1, tt, dv), tok),
            pl.BlockSpec((1, tt, ng), tok),
            pl.BlockSpec((1, tt, ng), tok),
            _const_spec(hn_w.shape),
            _const_spec(e64.shape),
            _const_spec(e128.shape),
        ],
        out_specs=pl.BlockSpec((1, tt, dv), tok),
        out_shape=jax.ShapeDtypeStruct((b, s, dv), BF16),
        scratch_shapes=[
            pltpu.VMEM((ML_HEADS // 2, LANES, 2 * LANES), F32),
            pltpu.VMEM((8, LANES), F32),
        ],
        compiler_params=_params(("parallel", "arbitrary")),
        name="mlstm_core",
    )(q, k, v, og, gi, gf, hn_w, e64, e128)


def _pad_rows(a, rows=8):
    return jnp.pad(a, ((0, rows - a.shape[0]), (0, 0)))


def kernel(x, c, ada_w, ada_b, rw_mix, rw_w_in, rw_w0, rw_w2, rw_a0, rw_a2, rw_g2, rw_k_k, rw_k_a, rw_r_k, rw_gn_w, rw_gn_b, rw_w_out, ml_w_in, ml_conv_w, ml_conv_b, ml_b_i, ml_b_f, ml_hn_w, ml_w_out, ffn_w_gu, ffn_w_down, final_w):
    b, s, d = x.shape
    bf = lambda w: w.astype(BF16)
    final_w2 = final_w.reshape(1, d)

    mod = _adaln_mod(c, ada_w, ada_b)

    w_in = rw_w_in[0]
    o1 = d
    o2 = o1 + LORA_W
    o3 = o2 + d
    o4 = o3 + d
    o5 = o4 + LORA_A
    rw_weights = [bf(w_in[:, :o1]), bf(w_in[:, o2:o3]), bf(w_in[:, o3:o4]),
                  bf(w_in[:, o1:o2]), bf(w_in[:, o4:o5]), bf(w_in[:, o5:]),
                  bf(rw_w2[0]), bf(rw_a2[0]), bf(rw_g2[0])]
    mix = _pad_rows(rw_mix[0])
    vecs = _pad_rows(jnp.stack([rw_w0[0], rw_a0[0], rw_k_k[0], rw_k_a[0]]))
    proj = _rwkv_proj(x, mod[0], mix, vecs, rw_weights, TM_RWKV_PROJ)
    core_vecs = _pad_rows(jnp.stack([rw_r_k[0].reshape(d), rw_gn_w[0], rw_gn_b[0]]))
    y = _rwkv_core(proj, core_vecs, b, s, d, TT_RWKV, PAIRS_RWKV)
    w_gu = bf(ffn_w_gu)
    w_down = bf(ffn_w_down)
    x = _post_ffn(x, y, mod[0], final_w2, bf(rw_w_out[0]), w_gu, w_down, 0, TM_DENSE, False)

    w_in = ml_w_in[0]
    nqk = 2 * ML_HEADS * ML_QK
    nv = ML_HEADS * ML_V
    pad_lanes = lambda w: jnp.pad(w, ((0, 0), (0, LANES - w.shape[1])))
    o_g = nqk + 2 * nv
    ml_weights = [bf(w_in[:, :nqk]), bf(w_in[:, nqk:nqk + nv]), bf(w_in[:, nqk + nv:o_g]),
                  bf(pad_lanes(w_in[:, o_g:o_g + ML_HEADS])), bf(pad_lanes(w_in[:, o_g + ML_HEADS:]))]
    cv = _pad_rows(jnp.concatenate([ml_conv_w[0], ml_conv_b[0][None, :]], axis=0))
    gb = _pad_rows(pad_lanes(jnp.stack([ml_b_i[0], ml_b_f[0]])))
    q, k, v, og, gi, gf = _mlstm_proj(x, mod[1], cv, gb, ml_weights, TM_DENSE)
    y = _mlstm_core(q, k, v, og, gi, gf, ml_hn_w[0].reshape(1, nv), TT_MLSTM)
    x = _post_ffn(x, y, mod[1], final_w2, bf(ml_w_out[0]), w_gu, w_down, 1, TM_DENSE, True)
    return x
```

```python
import functools

import jax
import jax.numpy as jnp
from jax import lax
from jax.experimental import pallas as pl
from jax.experimental.pallas import tpu as pltpu

F32 = jnp.float32
BF16 = jnp.bfloat16

EPS = 1e-6
GN_EPS = 64e-5
GATE_CAP = 15.0
DECAY_SCALE = 0.6065306597126334
CHUNK = 64
LANES = 128
RW_HEAD = 64
LORA_W = 64
LORA_A = 64
LORA_G = 160
ML_HEADS = 8
ML_QK = 64
ML_V = 128
CONV_W = 4
COL_BLOCK = 256
VMEM_LIMIT = 56 * 1024 * 1024
TM_RWKV_PROJ = 512
TM_DENSE = 1024
TT_RWKV = 256
PAIRS_RWKV = 8
TT_MLSTM = 512

NT_DIMS = (((1,), (1,)), ((), ()))
TN_DIMS = (((0,), (0,)), ((), ()))


def _dot(a, b):
    return jnp.dot(a, b, preferred_element_type=F32)


def _dot_nt(a, b):
    return lax.dot_general(a, b, NT_DIMS, preferred_element_type=F32)


def _dot_tn(a, b):
    return lax.dot_general(a, b, TN_DIMS, preferred_element_type=F32)


def _sigmoid(z):
    return 0.5 * jnp.tanh(0.5 * z) + 0.5


def _softplus(z):
    return jnp.maximum(z, 0.0) + jnp.log(1.0 + jnp.exp(-jnp.abs(z)))


def _rms_mod(xv, sc, sh):
    inv = lax.rsqrt(jnp.mean(xv * xv, axis=-1, keepdims=True) + EPS)
    return xv * inv * (1.0 + sc) + sh


def _split_bf16(z):
    hi = z.astype(BF16)
    lo = (z - hi.astype(F32)).astype(BF16)
    return hi, lo


def _params(sem):
    return pltpu.CompilerParams(dimension_semantics=sem, vmem_limit_bytes=VMEM_LIMIT)


def _const_spec(shape):
    nd = len(shape)
    return pl.BlockSpec(shape, lambda *_: (0,) * nd, pipeline_mode=pl.Buffered(1))


def _mod_kernel(c_ref, w_ref, b_ref, o_ref):
    c = c_ref[...]
    ca = (c * _sigmoid(c)).astype(BF16)
    o_ref[0] = _dot(ca, w_ref[0].astype(BF16)) + b_ref[0]


def _adaln_mod(c, ada_w, ada_b):
    depth, d, d6 = ada_w.shape
    b = c.shape[0]
    bp = 8
    assert b <= bp and d6 % d == 0
    c_pad = jnp.zeros((bp, d), F32).at[:b].set(c)
    nblk = d6 // d
    out = pl.pallas_call(
        _mod_kernel,
        grid=(depth, nblk),
        in_specs=[
            pl.BlockSpec((bp, d), lambda i, j: (0, 0)),
            pl.BlockSpec((1, d, d), lambda i, j: (i, 0, j)),
            pl.BlockSpec((1, 1, d), lambda i, j: (i, 0, j)),
        ],
        out_specs=pl.BlockSpec((1, bp, d), lambda i, j: (i, 0, j)),
        out_shape=jax.ShapeDtypeStruct((depth, bp, d6), F32),
        compiler_params=_params(("arbitrary", "arbitrary")),
        name="adaln_mod",
    )(c_pad, ada_w, ada_b.reshape(depth, 1, d6))
    mod = out[:, :b].reshape(depth, b, nblk, d)
    return jnp.pad(mod, ((0, 0), (0, 0), (0, 8 - nblk), (0, 0)))


def _rwkv_proj_kernel(x_ref, xp_ref, mod_ref, mix_ref, vec_ref, wr_ref, wk_ref, wv_ref,
                      wwl_ref, wal_ref, wgl_ref, w2_ref, a2_ref, g2_ref,
                      r_out, k_out, v_out, kk_out, a_out, lw_out, g_out, hbuf, xbuf):
    tm = x_ref.shape[1]
    t = pl.program_id(1)
    sh = mod_ref[0, 0:1, :]
    sc = mod_ref[0, 1:2, :]
    hp = _rms_mod(xp_ref[0], sc, sh)
    hbuf[0:8, :] = jnp.where(t > 0, hp, 0.0)
    w0 = vec_ref[0:1, :]
    a0 = vec_ref[1:2, :]
    k_k = vec_ref[2:3, :]
    k_a = vec_ref[3:4, :]

    h = _rms_mod(x_ref[0], sc, sh)
    hbuf[8:, :] = h
    xbuf[...] = hbuf[pl.ds(7, tm), :] - h

    def mixed(i):
        return (hbuf[8:, :] + xbuf[...] * mix_ref[i:i + 1, :]).astype(BF16)

    def put(out, val, cb):
        for q in range(COL_BLOCK // LANES):
            out[0, cb * (COL_BLOCK // LANES) + q] = val[:, q * LANES:(q + 1) * LANES].astype(out.dtype)

    ncb = r_out.shape[1] * LANES // COL_BLOCK
    col = lambda cb: slice(cb * COL_BLOCK, (cb + 1) * COL_BLOCK)
    lhs = mixed(0)
    for cb in range(ncb):
        put(r_out, _dot(lhs, wr_ref[:, col(cb)]), cb)
    lhs = mixed(3)
    for cb in range(ncb):
        put(v_out, _dot(lhs, wv_ref[:, col(cb)]), cb)
    twl = jnp.tanh(_dot(mixed(1), wwl_ref[...])).astype(BF16)
    alb = _dot(mixed(4), wal_ref[...]).astype(BF16)
    sgl = _sigmoid(_dot(mixed(5), wgl_ref[...])).astype(BF16)
    a_blocks = []
    for cb in range(ncb):
        c = col(cb)
        put(lw_out, -DECAY_SCALE * _sigmoid(w0[:, c] + _dot(twl, w2_ref[:, c])), cb)
        a = _sigmoid(a0[:, c] + _dot(alb, a2_ref[:, c]))
        put(a_out, a, cb)
        a_blocks.append(a)
        put(g_out, _dot(sgl, g2_ref[:, c]), cb)
    lhs = mixed(2)
    for cb in range(ncb):
        c = col(cb)
        k = _dot(lhs, wk_ref[:, c])
        put(kk_out, k * k_k[:, c], cb)
        put(k_out, k * (1.0 + (a_blocks[cb] - 1.0) * k_a[:, c]), cb)


def _rwkv_proj(x, mod, mix, vecs, weights, tm):
    b, s, d = x.shape
    assert s % tm == 0 and d % COL_BLOCK == 0
    npair = d // LANES
    nt = s // tm
    blk8 = tm // 8
    pair_shape = lambda dt: jax.ShapeDtypeStruct((b, npair, s, LANES), dt)
    pair_spec = pl.BlockSpec((1, npair, tm, LANES), lambda i, j: (i, 0, j, 0))
    return pl.pallas_call(
        _rwkv_proj_kernel,
        grid=(b, nt),
        in_specs=[
            pl.BlockSpec((1, tm, d), lambda i, j: (i, j, 0)),
            pl.BlockSpec((1, 8, d), lambda i, j: (i, jnp.maximum(j * blk8 - 1, 0), 0)),
            pl.BlockSpec((1, 8, d), lambda i, j: (i, 0, 0)),
            _const_spec(mix.shape),
            _const_spec(vecs.shape),
        ] + [_const_spec(w.shape) for w in weights],
        out_specs=[pair_spec] * 7,
        out_shape=[pair_shape(BF16)] * 5 + [pair_shape(F32), pair_shape(BF16)],
        scratch_shapes=[pltpu.VMEM((tm + 8, d), F32), pltpu.VMEM((tm, d), F32)],
        compiler_params=_params(("parallel", "arbitrary")),
        name="rwkv_proj",
    )(x, x, mod, mix, vecs, *weights)


def _rwkv_core_kernel(r_ref, k_ref, v_ref, kk_ref, a_ref, lw_ref, g_ref, vec_ref, vec_prev_ref,
                      o_ref, s_ref, rbar_s, y0_s, mlrt_s, s0_s, wl_s, g_s, bg_s, *, tiles_per_seq):
    npg = r_ref.shape[1]
    tt = r_ref.shape[2]
    L = CHUNK
    nch = tt // L
    nunit = npg * nch
    step = pl.program_id(0)

    @pl.when(step == 0)
    def _():
        for ref in (s_ref, rbar_s, y0_s, mlrt_s, s0_s, wl_s, g_s, bg_s):
            ref[...] = jnp.zeros_like(ref)

    fresh = (jnp.maximum(step - 1, 0) % tiles_per_seq) == 0

    lane = lax.broadcasted_iota(jnp.int32, (L, LANES), 1)
    m0 = lane < RW_HEAD
    rowl = lax.broadcasted_iota(jnp.int32, (L, LANES), 0)
    strict = rowl > (lane % L)
    incl = rowl >= (lane % L)
    in_block = [(rowl // b) == ((lane % L) // b) for b in (4, 16)]
    eye_cat = (rowl == (lane % L)).astype(F32)
    row2 = lax.broadcasted_iota(jnp.int32, (LANES, LANES), 0)
    col2 = lax.broadcasted_iota(jnp.int32, (LANES, LANES), 1)
    blockdiag = (row2 // RW_HEAD) == (col2 // RW_HEAD)
    rowt = lax.broadcasted_iota(jnp.int32, (tt, tt), 0)
    colt = lax.broadcasted_iota(jnp.int32, (tt, tt), 1)
    tri = (((rowt // L) == (colt // L)) & (rowt >= colt)).astype(BF16)
    zeros_l = jnp.zeros((L, LANES), BF16)

    def head_sum(z):
        lo = jnp.sum(jnp.where(m0, z, 0.0), axis=-1, keepdims=True)
        hi = jnp.sum(jnp.where(m0, 0.0, z), axis=-1, keepdims=True)
        return jnp.where(m0, lo, hi)

    def split_heads(z):
        zb = z.astype(BF16)
        zero = jnp.zeros_like(zb)
        return jnp.concatenate([jnp.where(m0, zb, zero), jnp.where(m0, zero, zb)], axis=0)

    def cat_dot(xc, yc):
        return _dot(xc.astype(BF16), split_heads(yc))

    units = [(p, c) for p in range(npg) for c in range(nch)]

    cums = []
    for p in range(npg):
        hi, lo = _split_bf16(lw_ref[0, p])
        both = _dot(tri, jnp.concatenate([hi, lo], axis=1))
        cums.append(both[:, :LANES] + both[:, LANES:])

    pu = []
    for (p, c) in units:
        rows = slice(c * L, (c + 1) * L)
        r = r_ref[0, p, rows, :].astype(F32)
        k = k_ref[0, p, rows, :].astype(F32)
        v = v_ref[0, p, rows, :]
        kk = kk_ref[0, p, rows, :].astype(F32)
        a = a_ref[0, p, rows, :].astype(F32)
        g = g_ref[0, p, rows, :]
        lw = lw_ref[0, p, rows, :]
        cum = cums[p][rows, :]
        kkn = kk * jnp.minimum(lax.rsqrt(head_sum(kk * kk)), 1e12)
        bvec = kkn * a
        c_last = cum[L - 1:L, :]
        at = -kkn * jnp.exp(cum - lw)
        rt = r * jnp.exp(cum)
        inv_w = jnp.exp(-cum)
        d_last = jnp.exp(c_last - cum)
        lanes = slice(p * LANES, (p + 1) * LANES)
        bonus = head_sum(r * k * vec_ref[0:1, lanes]) * v.astype(F32)
        pu.append(dict(
            rt=rt, v=v, w_last=jnp.exp(c_last), g=g, bg=bonus * g.astype(F32),
            lhs=jnp.concatenate([at, rt], axis=0).astype(BF16),
            rhs=jnp.concatenate([split_heads(bvec * inv_w), split_heads(k * inv_w)], axis=0),
            dk=jnp.concatenate([bvec * d_last, k * d_last], axis=0).astype(BF16),
            at_s=split_heads(at), v_s=split_heads(v),
        ))

    states = [jnp.where(fresh, 0.0, s_ref[p]) for p in range(npg)]
    ys = {}

    def scan_level(c):
        for p in range(npg):
            j = p * nch + c
            s_old = states[p]
            sb = s_old.astype(BF16)
            ys[j] = _dot_nt(rbar_s[j], sb) + y0_s[j]
            states[p] = s_old * wl_s[j] + _dot_nt(sb, mlrt_s[j]) + s0_s[j]

    per_level = -(-nunit // nch)
    for j, u in enumerate(pu):
        if j % per_level == 0:
            scan_level(j // per_level)
        gram = _dot_nt(u["lhs"], u["rhs"])
        a_ab = jnp.where(strict, gram[:L, :LANES], 0.0)
        u["a_lvl"] = [jnp.where(in_block[0], a_ab, 0.0),
                      jnp.where(in_block[1] & ~in_block[0], a_ab, 0.0),
                      jnp.where(in_block[1], 0.0, a_ab)]
        u["r_b"] = jnp.where(incl, gram[L:, :LANES], 0.0).astype(BF16)
        u["ak_rk"] = jnp.concatenate([jnp.where(strict, gram[:L, LANES:], 0.0),
                                      jnp.where(incl, gram[L:, LANES:], 0.0)], axis=0).astype(BF16)
    for p in range(npg):
        s_ref[p] = states[p]

    for j in range(nunit):
        p, c = divmod(j, nch)
        lanes = slice(p * LANES, (p + 1) * LANES)
        y = ys[j]
        mu = head_sum(y) * (1.0 / RW_HEAD)
        yc = y - mu
        var = head_sum(yc * yc) * (1.0 / RW_HEAD)
        yn = yc * lax.rsqrt(var + GN_EPS) * vec_prev_ref[1:2, lanes] + vec_prev_ref[2:3, lanes]
        o_ref[0, c * L:(c + 1) * L, lanes] = (yn * g_s[j].astype(F32) + bg_s[j]).astype(o_ref.dtype)

    for u in pu:
        both = _dot(u["ak_rk"], u["v_s"])
        u["akv"] = both[:L]
        u["rkv"] = both[L:]

    def short_inverse(key_in, key_out):
        for u in pu:
            u["_sq"] = cat_dot(u[key_in], u[key_in])
        for u in pu:
            first = eye_cat + u[key_in]
            u[key_out] = first + cat_dot(first, u["_sq"])

    for u in pu:
        u["p0"] = u["a_lvl"][0]
    short_inverse("p0", "tinv")
    for lvl in (1, 2):
        for u in pu:
            u["n_lvl"] = cat_dot(u["tinv"], u["a_lvl"][lvl])
        short_inverse("n_lvl", "r_lvl")
        for u in pu:
            u["tinv"] = cat_dot(u["r_lvl"], u["tinv"])

    for u in pu:
        xin = jnp.concatenate([u["at_s"], split_heads(u["akv"])], axis=1)
        u["z"] = _dot(u["tinv"].astype(BF16), xin)
    for u in pu:
        z = u["z"]
        rhs = jnp.concatenate([split_heads(z[:, :LANES]), split_heads(z[:, LANES:])], axis=1)
        u["w"] = _dot(u["r_b"], rhs)
    for u in pu:
        bot = jnp.concatenate([zeros_l, u["v"]], axis=1)
        u["ms"] = _dot_tn(u["dk"], jnp.concatenate([u["z"].astype(BF16), bot], axis=0))

    for j, u in enumerate(pu):
        rbar_s[j] = (u["rt"] + u["w"][:, :LANES]).astype(BF16)
        y0_s[j] = u["w"][:, LANES:] + u["rkv"]
        mlrt_s[j] = jnp.where(blockdiag, u["ms"][:, :LANES], 0.0).astype(BF16)
        s0_s[j] = jnp.where(blockdiag, u["ms"][:, LANES:].T, 0.0)
        wl_s[j] = u["w_last"]
        g_s[j] = u["g"]
        bg_s[j] = u["bg"]


def _rwkv_core(proj, vecs, b, s, d, tt, npg):
    assert d % (2 * RW_HEAD) == 0 and (d // LANES) % npg == 0 and s % tt == 0 and tt % CHUNK == 0
    npair = d // LANES
    ngrp = npair // npg
    nt = s // tt
    ntiles = b * ngrp * nt
    nunit = npg * (tt // CHUNK)

    def tile_coords(tile):
        return tile // (ngrp * nt), (tile // nt) % ngrp, tile % nt

    def in_map(i):
        bi, gi, ti = tile_coords(jnp.minimum(i, ntiles - 1))
        return (bi, gi, ti, 0)

    def out_map(i):
        bi, gi, ti = tile_coords(jnp.maximum(i - 1, 0))
        return (bi, ti, gi)

    in_spec = pl.BlockSpec((1, npg, tt, LANES), in_map)
    vec_spec = pl.BlockSpec((8, npg * LANES), lambda i: (0, tile_coords(jnp.minimum(i, ntiles - 1))[1]))
    vec_prev_spec = pl.BlockSpec((8, npg * LANES), lambda i: (0, tile_coords(jnp.maximum(i - 1, 0))[1]))
    L = CHUNK
    return pl.pallas_call(
        functools.partial(_rwkv_core_kernel, tiles_per_seq=nt),
        grid=(ntiles + 1,),
        in_specs=[in_spec] * 7 + [vec_spec, vec_prev_spec],
        out_specs=pl.BlockSpec((1, tt, npg * LANES), out_map),
        out_shape=jax.ShapeDtypeStruct((b, s, d), BF16),
        scratch_shapes=[
            pltpu.VMEM((npg, LANES, LANES), F32),
            pltpu.VMEM((nunit, L, LANES), BF16),
            pltpu.VMEM((nunit, L, LANES), F32),
            pltpu.VMEM((nunit, LANES, LANES), BF16),
            pltpu.VMEM((nunit, LANES, LANES), F32),
            pltpu.VMEM((nunit, 1, LANES), F32),
            pltpu.VMEM((nunit, L, LANES), BF16),
            pltpu.VMEM((nunit, L, LANES), F32),
        ],
        compiler_params=_params(("arbitrary",)),
        name="rwkv_core",
    )(*proj, vecs, vecs)


def _post_ffn_kernel(x_ref, y_ref, mod_ref, fw_ref, wo_ref, wgu_ref, wd_ref, o_ref, act_ref,
                     *, final):
    g1 = mod_ref[0, 2:3, :]
    sh2 = mod_ref[0, 3:4, :]
    sc2 = mod_ref[0, 4:5, :]
    g2 = mod_ref[0, 5:6, :]
    x1 = x_ref[0] + g1 * _dot(y_ref[0], wo_ref[...])
    h = _rms_mod(x1, sc2, sh2).astype(BF16)
    d_ff = act_ref.shape[1]
    for c0 in range(0, d_ff, COL_BLOCK):
        cols = slice(c0, c0 + COL_BLOCK)
        gate = _dot(h, wgu_ref[0, :, cols])
        up = _dot(h, wgu_ref[0, :, d_ff + c0:d_ff + c0 + COL_BLOCK])
        act_ref[:, cols] = (gate * _sigmoid(gate) * up).astype(BF16)
    x2 = x1 + g2 * _dot(act_ref[...], wd_ref[0])
    if final:
        inv = lax.rsqrt(jnp.mean(x2 * x2, axis=-1, keepdims=True) + EPS)
        x2 = x2 * inv * fw_ref[0:1, :]
    o_ref[0] = x2


def _post_ffn(x, y, mod, final_w, w_out, w_gu, w_d, layer, tm, final):
    b, s, d = x.shape
    d_ff = w_d.shape[1]
    assert d_ff % COL_BLOCK == 0 and s % tm == 0
    kern = functools.partial(_post_ffn_kernel, final=final)
    tok = lambda i, j: (i, j, 0)
    layer_spec = lambda w: pl.BlockSpec((1,) + w.shape[1:], lambda i, j: (layer, 0, 0),
                                        pipeline_mode=pl.Buffered(1))
    return pl.pallas_call(
        kern,
        grid=(b, s // tm),
        in_specs=[
            pl.BlockSpec((1, tm, d), tok),
            pl.BlockSpec((1, tm, d), tok),
            pl.BlockSpec((1, 8, d), lambda i, j: (i, 0, 0)),
            _const_spec(final_w.shape),
            _const_spec(w_out.shape),
            layer_spec(w_gu),
            layer_spec(w_d),
        ],
        out_specs=pl.BlockSpec((1, tm, d), tok),
        out_shape=jax.ShapeDtypeStruct((b, s, d), F32),
        scratch_shapes=[pltpu.VMEM((tm, d_ff), BF16)],
        compiler_params=_params(("parallel", "parallel")),
        name="post_ffn_final" if final else "post_ffn",
    )(x, y, mod, final_w, w_out, w_gu, w_d)


def _softcap(z):
    return GATE_CAP * jnp.tanh(z * (1.0 / GATE_CAP))


def _mlstm_proj_kernel(x_ref, xp_ref, mod_ref, cv_ref, gb_ref, wqk_ref, wv_ref,
                       wo_ref, wgi_ref, wgf_ref,
                       q_out, k_out, v_out, og_out, gi_out, gf_out):
    tm = x_ref.shape[1]
    t = pl.program_id(1)
    sh = mod_ref[0, 0:1, :]
    sc = mod_ref[0, 1:2, :]
    h = _rms_mod(x_ref[0], sc, sh).astype(BF16)
    hp = _rms_mod(xp_ref[0], sc, sh).astype(BF16)

    half = wqk_ref.shape[1] // 2
    first_row = lax.broadcasted_iota(jnp.int32, (tm, COL_BLOCK), 0) == 0
    for c0 in range(0, wqk_ref.shape[1], COL_BLOCK):
        c = slice(c0, c0 + COL_BLOCK)
        pqk = _dot(h, wqk_ref[:, c])
        ppv = jnp.where(t > 0, _dot(hp, wqk_ref[:, c]), 0.0)
        acc = pqk * cv_ref[0:1, c]
        acc_prev = ppv * cv_ref[0:1, c]
        for j in range(1, CONV_W):
            wj = cv_ref[j:j + 1, c]
            carry = acc_prev[7:8, :]
            acc = pqk * wj + jnp.where(first_row, carry, pltpu.roll(acc, 1, 0))
            acc_prev = ppv * wj + pltpu.roll(acc_prev, 1, 0)
        conv = acc + cv_ref[CONV_W:CONV_W + 1, c]
        qk = conv * _sigmoid(conv)
        if c0 < half:
            q_out[0, :, c] = (qk * (ML_QK ** -0.5)).astype(q_out.dtype)
        else:
            k_out[0, :, c0 - half:c0 - half + COL_BLOCK] = qk.astype(k_out.dtype)
    for c0 in range(0, wv_ref.shape[1], COL_BLOCK):
        c = slice(c0, c0 + COL_BLOCK)
        v_out[0, :, c] = _dot(h, wv_ref[:, c]).astype(v_out.dtype)
    for c0 in range(0, wo_ref.shape[1], COL_BLOCK):
        c = slice(c0, c0 + COL_BLOCK)
        og_out[0, :, c] = _sigmoid(_dot(h, wo_ref[:, c])).astype(og_out.dtype)

    lane = lax.broadcasted_iota(jnp.int32, gi_out.shape[1:], 1)
    zi = _dot(h, wgi_ref[...]) + gb_ref[0:1, :]
    zf = _dot(h, wgf_ref[...]) + gb_ref[1:2, :]
    gi_out[0] = jnp.where(lane < ML_HEADS, _softcap(zi), 0.0)
    gf_out[0] = jnp.where(lane < ML_HEADS, -_softplus(-_softcap(zf)), 0.0)


def _mlstm_proj(x, mod, cv, gb, weights, tm):
    b, s, d = x.shape
    w_qk, w_v, w_o, w_gi, w_gf = weights
    nqk = w_qk.shape[1] // 2
    nv = w_v.shape[1]
    assert s % tm == 0 and w_qk.shape[1] % COL_BLOCK == 0 and nv % COL_BLOCK == 0
    ng = w_gi.shape[1]
    blk8 = tm // 8
    tok = lambda i, j: (i, j, 0)
    return pl.pallas_call(
        _mlstm_proj_kernel,
        grid=(b, s // tm),
        in_specs=[
            pl.BlockSpec((1, tm, d), tok),
            pl.BlockSpec((1, 8, d), lambda i, j: (i, jnp.maximum(j * blk8 - 1, 0), 0)),
            pl.BlockSpec((1, 8, d), lambda i, j: (i, 0, 0)),
            _const_spec(cv.shape),
            _const_spec(gb.shape),
        ] + [_const_spec(w.shape) for w in weights],
        out_specs=[
            pl.BlockSpec((1, tm, nqk), tok),
            pl.BlockSpec((1, tm, nqk), tok),
            pl.BlockSpec((1, tm, nv), tok),
            pl.BlockSpec((1, tm, nv), tok),
            pl.BlockSpec((1, tm, ng), tok),
            pl.BlockSpec((1, tm, ng), tok),
        ],
        out_shape=[
            jax.ShapeDtypeStruct((b, s, nqk), BF16),
            jax.ShapeDtypeStruct((b, s, nqk), BF16),
            jax.ShapeDtypeStruct((b, s, nv), BF16),
            jax.ShapeDtypeStruct((b, s, nv), BF16),
            jax.ShapeDtypeStruct((b, s, ng), F32),
            jax.ShapeDtypeStruct((b, s, ng), F32),
        ],
        compiler_params=_params(("parallel", "arbitrary")),
        name="mlstm_proj",
    )(x, x, mod, cv, gb, *weights)


def _split3_bf16(z):
    hi = z.astype(BF16)
    r1 = z - hi.astype(F32)
    mid = r1.astype(BF16)
    lo = (r1 - mid.astype(F32)).astype(BF16)
    return jnp.concatenate([hi, mid, lo], axis=1)


def _mlstm_core_kernel(q_ref, k_ref, v_ref, og_ref, gi_ref, gf_ref, hn_ref, e64_ref, e128_ref,
                       o_ref, c_ref, m_ref):
    tt = q_ref.shape[1]
    L = CHUNK
    nch = tt // L
    npair = ML_HEADS // 2
    t = pl.program_id(1)

    @pl.when(t == 0)
    def _():
        c_ref[...] = jnp.zeros_like(c_ref)
        m_ref[...] = jnp.zeros_like(m_ref)

    lane = lax.broadcasted_iota(jnp.int32, (L, LANES), 1)
    rowl = lax.broadcasted_iota(jnp.int32, (L, LANES), 0)
    m0 = lane < ML_QK
    src = lane % L
    rowt = lax.broadcasted_iota(jnp.int32, (tt, tt), 0)
    colt = lax.broadcasted_iota(jnp.int32, (tt, tt), 1)
    tri = (((rowt // L) == (colt // L)) & (rowt >= colt)).astype(BF16)
    row_in_chunk = lax.broadcasted_iota(jnp.int32, (tt, LANES), 0) % L
    row_first = lax.broadcasted_iota(jnp.int32, (LANES, 1), 0) < ML_QK
    ones = jnp.ones((L, LANES), BF16)

    def split_heads(zv):
        zb = zv.astype(BF16)
        zero = jnp.zeros_like(zb)
        return jnp.concatenate([jnp.where(m0, zb, zero), jnp.where(m0, zero, zb)], axis=0)

    gi = gi_ref[0]
    b3 = _dot(tri, _split3_bf16(gf_ref[0]))
    bcum = b3[:, :LANES] + b3[:, LANES:2 * LANES] + b3[:, 2 * LANES:]
    z = gi - bcum
    cmax = z
    shift = 1
    while shift < L:
        cmax = jnp.where(row_in_chunk >= shift, jnp.maximum(cmax, pltpu.roll(cmax, shift, 0)), cmax)
        shift *= 2
    m_intra = bcum + cmax

    m_prev = m_ref[0:1, :]
    mp_rows, mn_rows, gt_rows, s_a = [], [], [], []
    for c in range(nch):
        last = c * L + L - 1
        g_tot = bcum[last:last + 1, :]
        m_loc = g_tot + cmax[last:last + 1, :]
        m_new = jnp.maximum(g_tot + m_prev, m_loc)
        mp_rows.append(jnp.broadcast_to(m_prev, (L, LANES)))
        mn_rows.append(jnp.broadcast_to(m_new, (L, LANES)))
        gt_rows.append(jnp.broadcast_to(g_tot, (L, LANES)))
        s_a.append(jnp.exp(g_tot + m_prev - m_new))
        m_prev = m_new
    m_ref[0:1, :] = m_prev
    inter = bcum + jnp.concatenate(mp_rows, axis=0)
    m_t = jnp.maximum(inter, m_intra)
    w_inter = jnp.exp(inter - m_t)
    kws = jnp.exp(jnp.concatenate(gt_rows, axis=0) + z - jnp.concatenate(mn_rows, axis=0))

    head_lane = lax.broadcasted_iota(jnp.int32, (tt, LANES), 1) < ML_HEADS

    def pack3(val):
        val = jnp.where(head_lane, val, 0.0)
        hi = val.astype(BF16).astype(F32)
        r1 = val - hi
        mid = r1.astype(BF16).astype(F32)
        lo = (r1 - mid).astype(BF16).astype(F32)
        packed = hi + pltpu.roll(mid, ML_HEADS, 1) + pltpu.roll(lo, 2 * ML_HEADS, 1)
        return packed.astype(BF16)

    spread = _dot(jnp.concatenate([pack3(bcum - m_t), pack3(z), pack3(w_inter), pack3(kws)], axis=0),
                  e64_ref[...])
    a_exp, z_exp, wi_exp, kw_exp = (spread[i * tt:(i + 1) * tt] for i in range(4))
    em_exp = _dot(pack3(jnp.exp(-m_t)), e128_ref[...])

    un = []
    for c in range(nch):
        rows = slice(c * L, (c + 1) * L)
        for p in range(npair):
            lanes = slice(p * LANES, (p + 1) * LANES)
            qp = q_ref[0, rows, lanes]
            kp = k_ref[0, rows, lanes]
            v0 = v_ref[0, rows, 2 * p * ML_V:(2 * p + 1) * ML_V]
            v1 = v_ref[0, rows, (2 * p + 1) * ML_V:(2 * p + 2) * ML_V]
            z_src = jnp.sum(jnp.where(rowl == src, z_exp[rows, lanes], 0.0), axis=0, keepdims=True)
            un.append(dict(
                c=c, p=p, rows=rows, qp=qp,
                k2=split_heads(kp),
                expd=jnp.where(rowl >= src, jnp.exp(a_exp[rows, lanes] + z_src), 0.0),
                qs=split_heads(qp * wi_exp[rows, lanes]),
                kw=split_heads(kp * kw_exp[rows, lanes]),
                vst=jnp.concatenate([jnp.concatenate([v0, ones], axis=1),
                                     jnp.concatenate([v1, ones], axis=1)], axis=0),
            ))

    for u in un:
        scores = _dot_nt(u["qp"], u["k2"])
        u["w_intra"] = split_heads(u["expd"] * scores)
    for u in un:
        u["c_loc"] = _dot_tn(u["kw"], u["vst"])

    states = [c_ref[p] for p in range(npair)]
    for u in un:
        c, p = u["c"], u["p"]
        u["c_old"] = states[p].astype(BF16)
        sa = jnp.where(row_first, s_a[c][:, 2 * p:2 * p + 1], s_a[c][:, 2 * p + 1:2 * p + 2])
        states[p] = sa * states[p] + u["c_loc"]
    for p in range(npair):
        c_ref[p] = states[p]

    for u in un:
        u["nd"] = _dot(jnp.concatenate([u["qs"], u["w_intra"]], axis=1),
                       jnp.concatenate([u["c_old"], u["vst"]], axis=0))

    for u in un:
        rows = u["rows"]
        for hh in range(2):
            head = 2 * u["p"] + hh
            cols = slice(head * ML_V, (head + 1) * ML_V)
            num = u["nd"][hh * L:(hh + 1) * L, :LANES]
            den = u["nd"][hh * L:(hh + 1) * L, LANES:]
            den = jnp.maximum(jnp.abs(den), em_exp[rows, cols])
            hv = num / den
            hv = hv * lax.rsqrt(jnp.mean(hv * hv, axis=-1, keepdims=True) + EPS)
            hv = hv * hn_ref[0:1, cols]
            o_ref[0, rows, cols] = (hv * og_ref[0, rows, cols]).astype(o_ref.dtype)


def _expand_matrix(width):
    rows = jnp.arange(LANES)[:, None]
    cols = jnp.arange(ML_HEADS * width)[None, :]
    return ((rows < 3 * ML_HEADS) & (rows % ML_HEADS == cols // width)).astype(BF16)


def _mlstm_core(q, k, v, og, gi, gf, hn_w, tt):
    b, s, dv = v.shape
    assert s % tt == 0 and tt % CHUNK == 0 and dv == ML_HEADS * ML_V
    nqk = q.shape[2]
    ng = gi.shape[2]
    e64 = _expand_matrix(ML_QK)
    e128 = _expand_matrix(ML_V)
    tok = lambda i, j: (i, j, 0)
    return pl.pallas_call(
        _mlstm_core_kernel,
        grid=(b, s // tt),
        in_specs=[
            pl.BlockSpec((1, tt, nqk), tok),
            pl.BlockSpec((1, tt, nqk), tok),
            pl.BlockSpec((1, tt, dv), tok),
            pl.BlockSpec((1, tt, dv), tok),
            pl.BlockSpec((1, tt, ng), tok),
            pl.BlockSpec((1, tt, ng), tok),
            _const_spec(hn_w.shape),
            _const_spec(e64.shape),
            _const_spec(e128.shape),
        ],
        out_specs=pl.BlockSpec((1, tt, dv), tok),
        out_shape=jax.ShapeDtypeStruct((b, s, dv), BF16),
        scratch_shapes=[
            pltpu.VMEM((ML_HEADS // 2, LANES, 2 * LANES), F32),
            pltpu.VMEM((8, LANES), F32),
        ],
        compiler_params=_params(("parallel", "arbitrary")),
        name="mlstm_core",
    )(q, k, v, og, gi, gf, hn_w, e64, e128)


def _pad_rows(a, rows=8):
    return jnp.pad(a, ((0, rows - a.shape[0]), (0, 0)))


def kernel(x, c, ada_w, ada_b, rw_mix, rw_w_in, rw_w0, rw_w2, rw_a0, rw_a2, rw_g2, rw_k_k, rw_k_a, rw_r_k, rw_gn_w, rw_gn_b, rw_w_out, ml_w_in, ml_conv_w, ml_conv_b, ml_b_i, ml_b_f, ml_hn_w, ml_w_out, ffn_w_gu, ffn_w_down, final_w):
    b, s, d = x.shape
    bf = lambda w: w.astype(BF16)
    final_w2 = final_w.reshape(1, d)

    mod = _adaln_mod(c, ada_w, ada_b)

    w_in = rw_w_in[0]
    o1 = d
    o2 = o1 + LORA_W
    o3 = o2 + d
    o4 = o3 + d
    o5 = o4 + LORA_A
    rw_weights = [bf(w_in[:, :o1]), bf(w_in[:, o2:o3]), bf(w_in[:, o3:o4]),
                  bf(w_in[:, o1:o2]), bf(w_in[:, o4:o5]), bf(w_in[:, o5:]),
                  bf(rw_w2[0]), bf(rw_a2[0]), bf(rw_g2[0])]
    mix = _pad_rows(rw_mix[0])
    vecs = _pad_rows(jnp.stack([rw_w0[0], rw_a0[0], rw_k_k[0], rw_k_a[0]]))
    proj = _rwkv_proj(x, mod[0], mix, vecs, rw_weights, TM_RWKV_PROJ)
    core_vecs = _pad_rows(jnp.stack([rw_r_k[0].reshape(d), rw_gn_w[0], rw_gn_b[0]]))
    y = _rwkv_core(proj, core_vecs, b, s, d, TT_RWKV, PAIRS_RWKV)
    w_gu = bf(ffn_w_gu)
    w_down = bf(ffn_w_down)
    x = _post_ffn(x, y, mod[0], final_w2, bf(rw_w_out[0]), w_gu, w_down, 0, TM_DENSE, False)

    w_in = ml_w_in[0]
    nqk = 2 * ML_HEADS * ML_QK
    nv = ML_HEADS * ML_V
    pad_lanes = lambda w: jnp.pad(w, ((0, 0), (0, LANES - w.shape[1])))
    o_g = nqk + 2 * nv
    ml_weights = [bf(w_in[:, :nqk]), bf(w_in[:, nqk:nqk + nv]), bf(w_in[:, nqk + nv:o_g]),
                  bf(pad_lanes(w_in[:, o_g:o_g + ML_HEADS])), bf(pad_lanes(w_in[:, o_g + ML_HEADS:]))]
    cv = _pad_rows(jnp.concatenate([ml_conv_w[0], ml_conv_b[0][None, :]], axis=0))
    gb = _pad_rows(pad_lanes(jnp.stack([ml_b_i[0], ml_b_f[0]])))
    q, k, v, og, gi, gf = _mlstm_proj(x, mod[1], cv, gb, ml_weights, TM_DENSE)
    y = _mlstm_core(q, k, v, og, gi, gf, ml_hn_w[0].reshape(1, nv), TT_MLSTM)
    x = _post_ffn(x, y, mod[1], final_w2, bf(ml_w_out[0]), w_gu, w_down, 1, TM_DENSE, True)
    return x
```

```python
import functools

import jax
import jax.numpy as jnp
from jax import lax
from jax.experimental import pallas as pl
from jax.experimental.pallas import tpu as pltpu

F32 = jnp.float32
BF16 = jnp.bfloat16

EPS = 1e-6
GN_EPS = 64e-5
GATE_CAP = 15.0
DECAY_SCALE = 0.6065306597126334
CHUNK = 64
LANES = 128
RW_HEAD = 64
LORA_W = 64
LORA_A = 64
ML_HEADS = 8
ML_QK = 64
ML_V = 128
CONV_W = 4
COL_BLOCK = 256
VMEM_LIMIT = 56 * 1024 * 1024
TM_RWKV_PROJ = 512
TM_DENSE = 1024
TT_RWKV = 256
PAIRS_RWKV = 8
TT_MLSTM = 512

NT_DIMS = (((1,), (1,)), ((), ()))
TN_DIMS = (((0,), (0,)), ((), ()))


def _dot(a, b):
    return jnp.dot(a, b, preferred_element_type=F32)


def _dot_nt(a, b):
    return lax.dot_general(a, b, NT_DIMS, preferred_element_type=F32)


def _dot_tn(a, b):
    return lax.dot_general(a, b, TN_DIMS, preferred_element_type=F32)


def _sigmoid(z):
    return 0.5 * jnp.tanh(0.5 * z) + 0.5


def _softplus(z):
    return jnp.maximum(z, 0.0) + jnp.log(1.0 + jnp.exp(-jnp.abs(z)))


def _rms_mod(xv, sc, sh):
    inv = lax.rsqrt(jnp.mean(xv * xv, axis=-1, keepdims=True) + EPS)
    return xv * inv * (1.0 + sc) + sh


def _split_bf16(z):
    hi = z.astype(BF16)
    lo = (z - hi.astype(F32)).astype(BF16)
    return hi, lo


def _params(sem):
    return pltpu.CompilerParams(dimension_semantics=sem, vmem_limit_bytes=VMEM_LIMIT)


def _const_spec(shape):
    nd = len(shape)
    return pl.BlockSpec(shape, lambda *_: (0,) * nd, pipeline_mode=pl.Buffered(1))


def _mod_kernel(c_ref, w_ref, b_ref, o_ref):
    c = c_ref[...]
    ca = (c * _sigmoid(c)).astype(BF16)
    o_ref[0] = _dot(ca, w_ref[0].astype(BF16)) + b_ref[0]


def _adaln_mod(c, ada_w, ada_b):
    depth, d, d6 = ada_w.shape
    b = c.shape[0]
    bp = 8
    assert b <= bp and d6 % d == 0
    c_pad = jnp.zeros((bp, d), F32).at[:b].set(c)
    nblk = d6 // d
    out = pl.pallas_call(
        _mod_kernel,
        grid=(depth, nblk),
        in_specs=[
            pl.BlockSpec((bp, d), lambda i, j: (0, 0)),
            pl.BlockSpec((1, d, d), lambda i, j: (i, 0, j)),
            pl.BlockSpec((1, 1, d), lambda i, j: (i, 0, j)),
        ],
        out_specs=pl.BlockSpec((1, bp, d), lambda i, j: (i, 0, j)),
        out_shape=jax.ShapeDtypeStruct((depth, bp, d6), F32),
        compiler_params=_params(("arbitrary", "arbitrary")),
        name="adaln_mod",
    )(c_pad, ada_w, ada_b.reshape(depth, 1, d6))
    mod = out[:, :b].reshape(depth, b, nblk, d)
    return jnp.pad(mod, ((0, 0), (0, 0), (0, 8 - nblk), (0, 0)))


def _rwkv_proj_kernel(x_ref, xp_ref, mod_ref, mix_ref, vec_ref, wr_ref, wk_ref, wv_ref,
                      wwl_ref, wal_ref, wgl_ref, w2_ref, a2_ref, g2_ref,
                      r_out, k_out, v_out, kk_out, a_out, lw_out, g_out, hbuf, xbuf):
    tm = x_ref.shape[1]
    t = pl.program_id(1)
    sh = mod_ref[0, 0:1, :]
    sc = mod_ref[0, 1:2, :]
    hp = _rms_mod(xp_ref[0], sc, sh)
    hbuf[0:8, :] = jnp.where(t > 0, hp, 0.0)
    w0 = vec_ref[0:1, :]
    a0 = vec_ref[1:2, :]
    k_k = vec_ref[2:3, :]
    k_a = vec_ref[3:4, :]

    h = _rms_mod(x_ref[0], sc, sh)
    hbuf[8:, :] = h
    xbuf[...] = hbuf[pl.ds(7, tm), :] - h

    def mixed(i):
        return (hbuf[8:, :] + xbuf[...] * mix_ref[i:i + 1, :]).astype(BF16)

    def put(out, val, cb):
        for q in range(COL_BLOCK // LANES):
            out[0, cb * (COL_BLOCK // LANES) + q] = val[:, q * LANES:(q + 1) * LANES].astype(out.dtype)

    ncb = r_out.shape[1] * LANES // COL_BLOCK
    col = lambda cb: slice(cb * COL_BLOCK, (cb + 1) * COL_BLOCK)
    lhs = mixed(0)
    for cb in range(ncb):
        put(r_out, _dot(lhs, wr_ref[:, col(cb)]), cb)
    lhs = mixed(3)
    for cb in range(ncb):
        put(v_out, _dot(lhs, wv_ref[:, col(cb)]), cb)
    twl = jnp.tanh(_dot(mixed(1), wwl_ref[...])).astype(BF16)
    alb = _dot(mixed(4), wal_ref[...]).astype(BF16)
    sgl = _sigmoid(_dot(mixed(5), wgl_ref[...])).astype(BF16)
    a_blocks = []
    for cb in range(ncb):
        c = col(cb)
        put(lw_out, -DECAY_SCALE * _sigmoid(w0[:, c] + _dot(twl, w2_ref[:, c])), cb)
        a = _sigmoid(a0[:, c] + _dot(alb, a2_ref[:, c]))
        put(a_out, a, cb)
        a_blocks.append(a)
        put(g_out, _dot(sgl, g2_ref[:, c]), cb)
    lhs = mixed(2)
    for cb in range(ncb):
        c = col(cb)
        k = _dot(lhs, wk_ref[:, c])
        put(kk_out, k * k_k[:, c], cb)
        put(k_out, k * (1.0 + (a_blocks[cb] - 1.0) * k_a[:, c]), cb)


def _rwkv_proj(x, mod, mix, vecs, weights, tm):
    b, s, d = x.shape
    assert s % tm == 0 and d % COL_BLOCK == 0
    npair = d // LANES
    nt = s // tm
    blk8 = tm // 8
    pair_shape = lambda dt: jax.ShapeDtypeStruct((b, npair, s, LANES), dt)
    pair_spec = pl.BlockSpec((1, npair, tm, LANES), lambda i, j: (i, 0, j, 0))
    return pl.pallas_call(
        _rwkv_proj_kernel,
        grid=(b, nt),
        in_specs=[
            pl.BlockSpec((1, tm, d), lambda i, j: (i, j, 0)),
            pl.BlockSpec((1, 8, d), lambda i, j: (i, jnp.maximum(j * blk8 - 1, 0), 0)),
            pl.BlockSpec((1, 8, d), lambda i, j: (i, 0, 0)),
            _const_spec(mix.shape),
            _const_spec(vecs.shape),
        ] + [_const_spec(w.shape) for w in weights],
        out_specs=[pair_spec] * 7,
        out_shape=[pair_shape(BF16)] * 5 + [pair_shape(F32), pair_shape(BF16)],
        scratch_shapes=[pltpu.VMEM((tm + 8, d), F32), pltpu.VMEM((tm, d), F32)],
        compiler_params=_params(("parallel", "arbitrary")),
        name="rwkv_proj",
    )(x, x, mod, mix, vecs, *weights)


def _rwkv_core_kernel(r_ref, k_ref, v_ref, kk_ref, a_ref, lw_ref, g_ref, vec_ref, vec_prev_ref,
                      o_ref, s_ref, rbar_s, y0_s, mlrt_s, s0_s, wl_s, g_s, bg_s, *, tiles_per_seq):
    npg = r_ref.shape[1]
    tt = r_ref.shape[2]
    L = CHUNK
    nch = tt // L
    nunit = npg * nch
    step = pl.program_id(0)

    @pl.when(step == 0)
    def _():
        for ref in (s_ref, rbar_s, y0_s, mlrt_s, s0_s, wl_s, g_s, bg_s):
            ref[...] = jnp.zeros_like(ref)

    fresh = (jnp.maximum(step - 1, 0) % tiles_per_seq) == 0

    lane = lax.broadcasted_iota(jnp.int32, (L, LANES), 1)
    m0 = lane < RW_HEAD
    rowl = lax.broadcasted_iota(jnp.int32, (L, LANES), 0)
    strict = rowl > (lane % L)
    incl = rowl >= (lane % L)
    in_block = [(rowl // b) == ((lane % L) // b) for b in (4, 16)]
    eye_cat = (rowl == (lane % L)).astype(F32)
    row2 = lax.broadcasted_iota(jnp.int32, (LANES, LANES), 0)
    col2 = lax.broadcasted_iota(jnp.int32, (LANES, LANES), 1)
    blockdiag = (row2 // RW_HEAD) == (col2 // RW_HEAD)
    rowt = lax.broadcasted_iota(jnp.int32, (tt, tt), 0)
    colt = lax.broadcasted_iota(jnp.int32, (tt, tt), 1)
    tri = (((rowt // L) == (colt // L)) & (rowt >= colt)).astype(BF16)
    zeros_l = jnp.zeros((L, LANES), BF16)

    def head_sum(z):
        lo = jnp.sum(jnp.where(m0, z, 0.0), axis=-1, keepdims=True)
        hi = jnp.sum(jnp.where(m0, 0.0, z), axis=-1, keepdims=True)
        return jnp.where(m0, lo, hi)

    def split_heads(z):
        zb = z.astype(BF16)
        zero = jnp.zeros_like(zb)
        return jnp.concatenate([jnp.where(m0, zb, zero), jnp.where(m0, zero, zb)], axis=0)

    def cat_dot(xc, yc):
        return _dot(xc.astype(BF16), split_heads(yc))

    units = [(p, c) for p in range(npg) for c in range(nch)]

    cums = []
    for p in range(npg):
        hi, lo = _split_bf16(lw_ref[0, p])
        both = _dot(tri, jnp.concatenate([hi, lo], axis=1))
        cums.append(both[:, :LANES] + both[:, LANES:])

    pu = []
    for (p, c) in units:
        rows = slice(c * L, (c + 1) * L)
        r = r_ref[0, p, rows, :].astype(F32)
        k = k_ref[0, p, rows, :].astype(F32)
        v = v_ref[0, p, rows, :]
        kk = kk_ref[0, p, rows, :].astype(F32)
        a = a_ref[0, p, rows, :].astype(F32)
        g = g_ref[0, p, rows, :]
        lw = lw_ref[0, p, rows, :]
        cum = cums[p][rows, :]
        kkn = kk * jnp.minimum(lax.rsqrt(head_sum(kk * kk)), 1e12)
        bvec = kkn * a
        c_last = cum[L - 1:L, :]
        at = -kkn * jnp.exp(cum - lw)
        rt = r * jnp.exp(cum)
        inv_w = jnp.exp(-cum)
        d_last = jnp.exp(c_last - cum)
        lanes = slice(p * LANES, (p + 1) * LANES)
        bonus = head_sum(r * k * vec_ref[0:1, lanes]) * v.astype(F32)
        pu.append(dict(
            rt=rt, v=v, w_last=jnp.exp(c_last), g=g, bg=bonus * g.astype(F32),
            lhs=jnp.concatenate([at, rt], axis=0).astype(BF16),
            rhs=jnp.concatenate([split_heads(bvec * inv_w), split_heads(k * inv_w)], axis=0),
            dk=jnp.concatenate([bvec * d_last, k * d_last], axis=0).astype(BF16),
            at_s=split_heads(at), v_s=split_heads(v),
        ))

    states = [jnp.where(fresh, 0.0, s_ref[p]) for p in range(npg)]
    ys = {}

    def scan_level(c):
        for p in range(npg):
            j = p * nch + c
            s_old = states[p]
            sb = s_old.astype(BF16)
            ys[j] = _dot_nt(rbar_s[j], sb) + y0_s[j]
            states[p] = s_old * wl_s[j] + _dot_nt(sb, mlrt_s[j]) + s0_s[j]

    per_level = -(-nunit // nch)
    for j, u in enumerate(pu):
        if j % per_level == 0:
            scan_level(j // per_level)
        gram = _dot_nt(u["lhs"], u["rhs"])
        a_ab = jnp.where(strict, gram[:L, :LANES], 0.0)
        u["a_lvl"] = [jnp.where(in_block[0], a_ab, 0.0),
                      jnp.where(in_block[1] & ~in_block[0], a_ab, 0.0),
                      jnp.where(in_block[1], 0.0, a_ab)]
        u["r_b"] = jnp.where(incl, gram[L:, :LANES], 0.0).astype(BF16)
        u["ak_rk"] = jnp.concatenate([jnp.where(strict, gram[:L, LANES:], 0.0),
                                      jnp.where(incl, gram[L:, LANES:], 0.0)], axis=0).astype(BF16)
    for p in range(npg):
        s_ref[p] = states[p]

    for j in range(nunit):
        p, c = divmod(j, nch)
        lanes = slice(p * LANES, (p + 1) * LANES)
        y = ys[j]
        mu = head_sum(y) * (1.0 / RW_HEAD)
        yc = y - mu
        var = head_sum(yc * yc) * (1.0 / RW_HEAD)
        yn = yc * lax.rsqrt(var + GN_EPS) * vec_prev_ref[1:2, lanes] + vec_prev_ref[2:3, lanes]
        o_ref[0, c * L:(c + 1) * L, lanes] = (yn * g_s[j].astype(F32) + bg_s[j]).astype(o_ref.dtype)

    for u in pu:
        both = _dot(u["ak_rk"], u["v_s"])
        u["akv"] = both[:L]
        u["rkv"] = both[L:]

    def short_inverse(key_in, key_out):
        for u in pu:
            u["_sq"] = cat_dot(u[key_in], u[key_in])
        for u in pu:
            first = eye_cat + u[key_in]
            u[key_out] = first + cat_dot(first, u["_sq"])

    for u in pu:
        u["p0"] = u["a_lvl"][0]
    short_inverse("p0", "tinv")
    for lvl in (1, 2):
        for u in pu:
            u["n_lvl"] = cat_dot(u["tinv"], u["a_lvl"][lvl])
        short_inverse("n_lvl", "r_lvl")
        for u in pu:
            u["tinv"] = cat_dot(u["r_lvl"], u["tinv"])

    for u in pu:
        xin = jnp.concatenate([u["at_s"], split_heads(u["akv"])], axis=1)
        u["z"] = _dot(u["tinv"].astype(BF16), xin)
    for u in pu:
        z = u["z"]
        rhs = jnp.concatenate([split_heads(z[:, :LANES]), split_heads(z[:, LANES:])], axis=1)
        u["w"] = _dot(u["r_b"], rhs)
    for u in pu:
        bot = jnp.concatenate([zeros_l, u["v"]], axis=1)
        u["ms"] = _dot_tn(u["dk"], jnp.concatenate([u["z"].astype(BF16), bot], axis=0))

    for j, u in enumerate(pu):
        rbar_s[j] = (u["rt"] + u["w"][:, :LANES]).astype(BF16)
        y0_s[j] = u["w"][:, LANES:] + u["rkv"]
        mlrt_s[j] = jnp.where(blockdiag, u["ms"][:, :LANES], 0.0).astype(BF16)
        s0_s[j] = jnp.where(blockdiag, u["ms"][:, LANES:].T, 0.0)
        wl_s[j] = u["w_last"]
        g_s[j] = u["g"]
        bg_s[j] = u["bg"]


def _rwkv_core(proj, vecs, b, s, d, tt, npg):
    assert d % (2 * RW_HEAD) == 0 and (d // LANES) % npg == 0 and s % tt == 0 and tt % CHUNK == 0
    npair = d // LANES
    ngrp = npair // npg
    nt = s // tt
    ntiles = b * ngrp * nt
    nunit = npg * (tt // CHUNK)

    def tile_coords(tile):
        return tile // (ngrp * nt), (tile // nt) % ngrp, tile % nt

    def in_map(i):
        bi, gi, ti = tile_coords(jnp.minimum(i, ntiles - 1))
        return (bi, gi, ti, 0)

    def out_map(i):
        bi, gi, ti = tile_coords(jnp.maximum(i - 1, 0))
        return (bi, ti, gi)

    in_spec = pl.BlockSpec((1, npg, tt, LANES), in_map)
    vec_spec = pl.BlockSpec((8, npg * LANES), lambda i: (0, tile_coords(jnp.minimum(i, ntiles - 1))[1]))
    vec_prev_spec = pl.BlockSpec((8, npg * LANES), lambda i: (0, tile_coords(jnp.maximum(i - 1, 0))[1]))
    L = CHUNK
    return pl.pallas_call(
        functools.partial(_rwkv_core_kernel, tiles_per_seq=nt),
        grid=(ntiles + 1,),
        in_specs=[in_spec] * 7 + [vec_spec, vec_prev_spec],
        out_specs=pl.BlockSpec((1, tt, npg * LANES), out_map),
        out_shape=jax.ShapeDtypeStruct((b, s, d), BF16),
        scratch_shapes=[
            pltpu.VMEM((npg, LANES, LANES), F32),
            pltpu.VMEM((nunit, L, LANES), BF16),
            pltpu.VMEM((nunit, L, LANES), F32),
            pltpu.VMEM((nunit, LANES, LANES), BF16),
            pltpu.VMEM((nunit, LANES, LANES), F32),
            pltpu.VMEM((nunit, 1, LANES), F32),
            pltpu.VMEM((nunit, L, LANES), BF16),
            pltpu.VMEM((nunit, L, LANES), F32),
        ],
        compiler_params=_params(("arbitrary",)),
        name="rwkv_core",
    )(*proj, vecs, vecs)


def _post_ffn_kernel(x_ref, y_ref, mod_ref, fw_ref, wo_ref, wgu_ref, wd_ref, o_ref, act_ref,
                     *, final):
    g1 = mod_ref[0, 2:3, :]
    sh2 = mod_ref[0, 3:4, :]
    sc2 = mod_ref[0, 4:5, :]
    g2 = mod_ref[0, 5:6, :]
    x1 = x_ref[0] + g1 * _dot(y_ref[0], wo_ref[...])
    h = _rms_mod(x1, sc2, sh2).astype(BF16)
    d_ff = act_ref.shape[1]
    for c0 in range(0, d_ff, COL_BLOCK):
        cols = slice(c0, c0 + COL_BLOCK)
        gate = _dot(h, wgu_ref[0, :, cols])
        up = _dot(h, wgu_ref[0, :, d_ff + c0:d_ff + c0 + COL_BLOCK])
        act_ref[:, cols] = (gate * _sigmoid(gate) * up).astype(BF16)
    x2 = x1 + g2 * _dot(act_ref[...], wd_ref[0])
    if final:
        inv = lax.rsqrt(jnp.mean(x2 * x2, axis=-1, keepdims=True) + EPS)
        x2 = x2 * inv * fw_ref[0:1, :]
    o_ref[0] = x2


def _post_ffn(x, y, mod, final_w, w_out, w_gu, w_d, layer, tm, final):
    b, s, d = x.shape
    d_ff = w_d.shape[1]
    assert d_ff % COL_BLOCK == 0 and s % tm == 0
    kern = functools.partial(_post_ffn_kernel, final=final)
    tok = lambda i, j: (i, j, 0)
    layer_spec = lambda w: pl.BlockSpec((1,) + w.shape[1:], lambda i, j: (layer, 0, 0),
                                        pipeline_mode=pl.Buffered(1))
    return pl.pallas_call(
        kern,
        grid=(b, s // tm),
        in_specs=[
            pl.BlockSpec((1, tm, d), tok),
            pl.BlockSpec((1, tm, d), tok),
            pl.BlockSpec((1, 8, d), lambda i, j: (i, 0, 0)),
            _const_spec(final_w.shape),
            _const_spec(w_out.shape),
            layer_spec(w_gu),
            layer_spec(w_d),
        ],
        out_specs=pl.BlockSpec((1, tm, d), tok),
        out_shape=jax.ShapeDtypeStruct((b, s, d), F32),
        scratch_shapes=[pltpu.VMEM((tm, d_ff), BF16)],
        compiler_params=_params(("parallel", "parallel")),
        name="post_ffn_final" if final else "post_ffn",
    )(x, y, mod, final_w, w_out, w_gu, w_d)


def _softcap(z):
    return GATE_CAP * jnp.tanh(z * (1.0 / GATE_CAP))


def _mlstm_proj_kernel(x_ref, xp_ref, mod_ref, cv_ref, gb_ref, wqk_ref, wv_ref,
                       wo_ref, wgi_ref, wgf_ref,
                       q_out, k_out, v_out, og_out, gi_out, gf_out):
    tm = x_ref.shape[1]
    t = pl.program_id(1)
    sh = mod_ref[0, 0:1, :]
    sc = mod_ref[0, 1:2, :]
    h = _rms_mod(x_ref[0], sc, sh).astype(BF16)
    hp = _rms_mod(xp_ref[0], sc, sh).astype(BF16)

    half = wqk_ref.shape[1] // 2
    first_row = lax.broadcasted_iota(jnp.int32, (tm, COL_BLOCK), 0) == 0
    for c0 in range(0, wqk_ref.shape[1], COL_BLOCK):
        c = slice(c0, c0 + COL_BLOCK)
        pqk = _dot(h, wqk_ref[:, c])
        ppv = jnp.where(t > 0, _dot(hp, wqk_ref[:, c]), 0.0)
        acc = pqk * cv_ref[0:1, c]
        acc_prev = ppv * cv_ref[0:1, c]
        for j in range(1, CONV_W):
            wj = cv_ref[j:j + 1, c]
            carry = acc_prev[7:8, :]
            acc = pqk * wj + jnp.where(first_row, carry, pltpu.roll(acc, 1, 0))
            acc_prev = ppv * wj + pltpu.roll(acc_prev, 1, 0)
        conv = acc + cv_ref[CONV_W:CONV_W + 1, c]
        qk = conv * _sigmoid(conv)
        if c0 < half:
            q_out[0, :, c] = (qk * (ML_QK ** -0.5)).astype(q_out.dtype)
        else:
            k_out[0, :, c0 - half:c0 - half + COL_BLOCK] = qk.astype(k_out.dtype)
    for c0 in range(0, wv_ref.shape[1], COL_BLOCK):
        c = slice(c0, c0 + COL_BLOCK)
        v_out[0, :, c] = _dot(h, wv_ref[:, c]).astype(v_out.dtype)
    for c0 in range(0, wo_ref.shape[1], COL_BLOCK):
        c = slice(c0, c0 + COL_BLOCK)
        og_out[0, :, c] = _sigmoid(_dot(h, wo_ref[:, c])).astype(og_out.dtype)

    lane = lax.broadcasted_iota(jnp.int32, gi_out.shape[1:], 1)
    zi = _dot(h, wgi_ref[...]) + gb_ref[0:1, :]
    zf = _dot(h, wgf_ref[...]) + gb_ref[1:2, :]
    gi_out[0] = jnp.where(lane < ML_HEADS, _softcap(zi), 0.0)
    gf_out[0] = jnp.where(lane < ML_HEADS, -_softplus(-_softcap(zf)), 0.0)


def _mlstm_proj(x, mod, cv, gb, weights, tm):
    b, s, d = x.shape
    w_qk, w_v, w_o, w_gi, w_gf = weights
    nqk = w_qk.shape[1] // 2
    nv = w_v.shape[1]
    assert s % tm == 0 and w_qk.shape[1] % COL_BLOCK == 0 and nv % COL_BLOCK == 0
    ng = w_gi.shape[1]
    blk8 = tm // 8
    tok = lambda i, j: (i, j, 0)
    return pl.pallas_call(
        _mlstm_proj_kernel,
        grid=(b, s // tm),
        in_specs=[
            pl.BlockSpec((1, tm, d), tok),
            pl.BlockSpec((1, 8, d), lambda i, j: (i, jnp.maximum(j * blk8 - 1, 0), 0)),
            pl.BlockSpec((1, 8, d), lambda i, j: (i, 0, 0)),
            _const_spec(cv.shape),
            _const_spec(gb.shape),
        ] + [_const_spec(w.shape) for w in weights],
        out_specs=[
            pl.BlockSpec((1, tm, nqk), tok),
            pl.BlockSpec((1, tm, nqk), tok),
            pl.BlockSpec((1, tm, nv), tok),
            pl.BlockSpec((1, tm, nv), tok),
            pl.BlockSpec((1, tm, ng), tok),
            pl.BlockSpec((1, tm, ng), tok),
        ],
        out_shape=[
            jax.ShapeDtypeStruct((b, s, nqk), BF16),
            jax.ShapeDtypeStruct((b, s, nqk), BF16),
            jax.ShapeDtypeStruct((b, s, nv), BF16),
            jax.ShapeDtypeStruct((b, s, nv), BF16),
            jax.ShapeDtypeStruct((b, s, ng), F32),
            jax.ShapeDtypeStruct((b, s, ng), F32),
        ],
        compiler_params=_params(("parallel", "arbitrary")),
        name="mlstm_proj",
    )(x, x, mod, cv, gb, *weights)


def _split3_bf16(z):
    hi = z.astype(BF16)
    r1 = z - hi.astype(F32)
    mid = r1.astype(BF16)
    lo = (r1 - mid.astype(F32)).astype(BF16)
    return jnp.concatenate([hi, mid, lo], axis=1)


def _mlstm_core_kernel(q_ref, k_ref, v_ref, og_ref, gi_ref, gf_ref, hn_ref, e64_ref, e128_ref,
                       o_ref, c_ref, m_ref):
    tt = q_ref.shape[1]
    L = CHUNK
    nch = tt // L
    npair = ML_HEADS // 2
    t = pl.program_id(1)

    @pl.when(t == 0)
    def _():
        c_ref[...] = jnp.zeros_like(c_ref)
        m_ref[...] = jnp.zeros_like(m_ref)

    lane = lax.broadcasted_iota(jnp.int32, (L, LANES), 1)
    rowl = lax.broadcasted_iota(jnp.int32, (L, LANES), 0)
    m0 = lane < ML_QK
    src = lane % L
    rowt = lax.broadcasted_iota(jnp.int32, (tt, tt), 0)
    colt = lax.broadcasted_iota(jnp.int32, (tt, tt), 1)
    tri = (((rowt // L) == (colt // L)) & (rowt >= colt)).astype(BF16)
    row_in_chunk = lax.broadcasted_iota(jnp.int32, (tt, LANES), 0) % L
    row_first = lax.broadcasted_iota(jnp.int32, (LANES, 1), 0) < ML_QK
    ones = jnp.ones((L, LANES), BF16)

    def split_heads(zv):
        zb = zv.astype(BF16)
        zero = jnp.zeros_like(zb)
        return jnp.concatenate([jnp.where(m0, zb, zero), jnp.where(m0, zero, zb)], axis=0)

    gi = gi_ref[0]
    b3 = _dot(tri, _split3_bf16(gf_ref[0]))
    bcum = b3[:, :LANES] + b3[:, LANES:2 * LANES] + b3[:, 2 * LANES:]
    z = gi - bcum
    cmax = z
    shift = 1
    while shift < L:
        cmax = jnp.where(row_in_chunk >= shift, jnp.maximum(cmax, pltpu.roll(cmax, shift, 0)), cmax)
        shift *= 2
    m_intra = bcum + cmax

    m_prev = m_ref[0:1, :]
    mp_rows, mn_rows, gt_rows, s_a = [], [], [], []
    for c in range(nch):
        last = c * L + L - 1
        g_tot = bcum[last:last + 1, :]
        m_loc = g_tot + cmax[last:last + 1, :]
        m_new = jnp.maximum(g_tot + m_prev, m_loc)
        mp_rows.append(jnp.broadcast_to(m_prev, (L, LANES)))
        mn_rows.append(jnp.broadcast_to(m_new, (L, LANES)))
        gt_rows.append(jnp.broadcast_to(g_tot, (L, LANES)))
        s_a.append(jnp.exp(g_tot + m_prev - m_new))
        m_prev = m_new
    m_ref[0:1, :] = m_prev
    inter = bcum + jnp.concatenate(mp_rows, axis=0)
    m_t = jnp.maximum(inter, m_intra)
    w_inter = jnp.exp(inter - m_t)
    kws = jnp.exp(jnp.concatenate(gt_rows, axis=0) + z - jnp.concatenate(mn_rows, axis=0))

    head_lane = lax.broadcasted_iota(jnp.int32, (tt, LANES), 1) < ML_HEADS

    def pack3(val):
        val = jnp.where(head_lane, val, 0.0)
        hi = val.astype(BF16).astype(F32)
        r1 = val - hi
        mid = r1.astype(BF16).astype(F32)
        lo = (r1 - mid).astype(BF16).astype(F32)
        packed = hi + pltpu.roll(mid, ML_HEADS, 1) + pltpu.roll(lo, 2 * ML_HEADS, 1)
        return packed.astype(BF16)

    spread = _dot(jnp.concatenate([pack3(bcum - m_t), pack3(z), pack3(w_inter), pack3(kws)], axis=0),
                  e64_ref[...])
    a_exp, z_exp, wi_exp, kw_exp = (spread[i * tt:(i + 1) * tt] for i in range(4))
    em_exp = _dot(pack3(jnp.exp(-m_t)), e128_ref[...])

    un = []
    for c in range(nch):
        rows = slice(c * L, (c + 1) * L)
        for p in range(npair):
            lanes = slice(p * LANES, (p + 1) * LANES)
            qp = q_ref[0, rows, lanes]
            kp = k_ref[0, rows, lanes]
            v0 = v_ref[0, rows, 2 * p * ML_V:(2 * p + 1) * ML_V]
            v1 = v_ref[0, rows, (2 * p + 1) * ML_V:(2 * p + 2) * ML_V]
            z_src = jnp.sum(jnp.where(rowl == src, z_exp[rows, lanes], 0.0), axis=0, keepdims=True)
            un.append(dict(
                c=c, p=p, rows=rows, qp=qp,
                k2=split_heads(kp),
                expd=jnp.where(rowl >= src, jnp.exp(a_exp[rows, lanes] + z_src), 0.0),
                qs=split_heads(qp * wi_exp[rows, lanes]),
                kw=split_heads(kp * kw_exp[rows, lanes]),
                vst=jnp.concatenate([jnp.concatenate([v0, ones], axis=1),
                                     jnp.concatenate([v1, ones], axis=1)], axis=0),
            ))

    for u in un:
        scores = _dot_nt(u["qp"], u["k2"])
        u["w_intra"] = split_heads(u["expd"] * scores)
    for u in un:
        u["c_loc"] = _dot_tn(u["kw"], u["vst"])

    states = [c_ref[p] for p in range(npair)]
    for u in un:
        c, p = u["c"], u["p"]
        u["c_old"] = states[p].astype(BF16)
        sa = jnp.where(row_first, s_a[c][:, 2 * p:2 * p + 1], s_a[c][:, 2 * p + 1:2 * p + 2])
        states[p] = sa * states[p] + u["c_loc"]
    for p in range(npair):
        c_ref[p] = states[p]

    for u in un:
        u["nd"] = _dot(jnp.concatenate([u["qs"], u["w_intra"]], axis=1),
                       jnp.concatenate([u["c_old"], u["vst"]], axis=0))

    for u in un:
        rows = u["rows"]
        for hh in range(2):
            head = 2 * u["p"] + hh
            cols = slice(head * ML_V, (head + 1) * ML_V)
            num = u["nd"][hh * L:(hh + 1) * L, :LANES]
            den = u["nd"][hh * L:(hh + 1) * L, LANES:]
            den = jnp.maximum(jnp.abs(den), em_exp[rows, cols])
            hv = num / den
            hv = hv * lax.rsqrt(jnp.mean(hv * hv, axis=-1, keepdims=True) + EPS)
            hv = hv * hn_ref[0:1, cols]
            o_ref[0, rows, cols] = (hv * og_ref[0, rows, cols]).astype(o_ref.dtype)


def _expand_matrix(width):
    rows = jnp.arange(LANES)[:, None]
    cols = jnp.arange(ML_HEADS * width)[None, :]
    return ((rows < 3 * ML_HEADS) & (rows % ML_HEADS == cols // width)).astype(BF16)


def _mlstm_core(q, k, v, og, gi, gf, hn_w, tt):
    b, s, dv = v.shape
    assert s % tt == 0 and tt % CHUNK == 0 and dv == ML_HEADS * ML_V
    nqk = q.shape[2]
    ng = gi.shape[2]
    e64 = _expand_matrix(ML_QK)
    e128 = _expand_matrix(ML_V)
    tok = lambda i, j: (i, j, 0)
    return pl.pallas_call(
        _mlstm_core_kernel,
        grid=(b, s // tt),
        in_specs=[
            pl.BlockSpec((1, tt, nqk), tok),
            pl.BlockSpec((1, tt, nqk), tok),
            pl.BlockSpec((1, tt, dv), tok),
            pl.BlockSpec((1, tt, dv), tok),
            pl.BlockSpec((1, tt, ng), tok),
            pl.BlockSpec((1, tt, ng), tok),
            _const_spec(hn_w.shape),
            _const_spec(e64.shape),
            _const_spec(e128.shape),
        ],
        out_specs=pl.BlockSpec((1, tt, dv), tok),
        out_shape=jax.ShapeDtypeStruct((b, s, dv), BF16),
        scratch_shapes=[
            pltpu.VMEM((ML_HEADS // 2, LANES, 2 * LANES), F32),
            pltpu.VMEM((8, LANES), F32),
        ],
        compiler_params=_params(("parallel", "arbitrary")),
        name="mlstm_core",
    )(q, k, v, og, gi, gf, hn_w, e64, e128)


def _pad_rows(a, rows=8):
    return jnp.pad(a, ((0, rows - a.shape[0]), (0, 0)))


def kernel(x, c, ada_w, ada_b, rw_mix, rw_w_in, rw_w0, rw_w2, rw_a0, rw_a2, rw_g2, rw_k_k, rw_k_a, rw_r_k, rw_gn_w, rw_gn_b, rw_w_out, ml_w_in, ml_conv_w, ml_conv_b, ml_b_i, ml_b_f, ml_hn_w, ml_w_out, ffn_w_gu, ffn_w_down, final_w):
    b, s, d = x.shape
    bf = lambda w: w.astype(BF16)
    final_w2 = final_w.reshape(1, d)

    mod = _adaln_mod(c, ada_w, ada_b)

    w_in = rw_w_in[0]
    o1 = d
    o2 = o1 + LORA_W
    o3 = o2 + d
    o4 = o3 + d
    o5 = o4 + LORA_A
    rw_weights = [bf(w_in[:, :o1]), bf(w_in[:, o2:o3]), bf(w_in[:, o3:o4]),
                  bf(w_in[:, o1:o2]), bf(w_in[:, o4:o5]), bf(w_in[:, o5:]),
                  bf(rw_w2[0]), bf(rw_a2[0]), bf(rw_g2[0])]
    mix = _pad_rows(rw_mix[0])
    vecs = _pad_rows(jnp.stack([rw_w0[0], rw_a0[0], rw_k_k[0], rw_k_a[0]]))
    proj = _rwkv_proj(x, mod[0], mix, vecs, rw_weights, TM_RWKV_PROJ)
    core_vecs = _pad_rows(jnp.stack([rw_r_k[0].reshape(d), rw_gn_w[0], rw_gn_b[0]]))
    y = _rwkv_core(proj, core_vecs, b, s, d, TT_RWKV, PAIRS_RWKV)
    w_gu = bf(ffn_w_gu)
    w_down = bf(ffn_w_down)
    x = _post_ffn(x, y, mod[0], final_w2, bf(rw_w_out[0]), w_gu, w_down, 0, TM_DENSE, False)

    w_in = ml_w_in[0]
    nqk = 2 * ML_HEADS * ML_QK
    nv = ML_HEADS * ML_V
    pad_lanes = lambda w: jnp.pad(w, ((0, 0), (0, LANES - w.shape[1])))
    o_g = nqk + 2 * nv
    ml_weights = [bf(w_in[:, :nqk]), bf(w_in[:, nqk:nqk + nv]), bf(w_in[:, nqk + nv:o_g]),
                  bf(pad_lanes(w_in[:, o_g:o_g + ML_HEADS])), bf(pad_lanes(w_in[:, o_g + ML_HEADS:]))]
    cv = _pad_rows(jnp.concatenate([ml_conv_w[0], ml_conv_b[0][None, :]], axis=0))
    gb = _pad_rows(pad_lanes(jnp.stack([ml_b_i[0], ml_b_f[0]])))
    q, k, v, og, gi, gf = _mlstm_proj(x, mod[1], cv, gb, ml_weights, TM_DENSE)
    y = _mlstm_core(q, k, v, og, gi, gf, ml_hn_w[0].reshape(1, nv), TT_MLSTM)
    x = _post_ffn(x, y, mod[1], final_w2, bf(ml_w_out[0]), w_gu, w_down, 1, TM_DENSE, True)
    return x
```

```python
import functools

import jax
import jax.numpy as jnp
from jax import lax
from jax.experimental import pallas as pl
from jax.experimental.pallas import tpu as pltpu

F32 = jnp.float32
BF16 = jnp.bfloat16

EPS = 1e-6
GN_EPS = 64e-5
GATE_CAP = 15.0
DECAY_SCALE = 0.6065306597126334
CHUNK = 64
LANES = 128
RW_HEAD = 64
LORA_W = 64
LORA_A = 64
ML_HEADS = 8
ML_QK = 64
ML_V = 128
CONV_W = 4
COL_BLOCK = 256
VMEM_LIMIT = 56 * 1024 * 1024
TM_RWKV_PROJ = 512
TM_DENSE = 1024
TT_RWKV = 256
PAIRS_RWKV = 8
TT_MLSTM = 512

NT_DIMS = (((1,), (1,)), ((), ()))
TN_DIMS = (((0,), (0,)), ((), ()))


def _dot(a, b):
    return jnp.dot(a, b, preferred_element_type=F32)


def _dot_nt(a, b):
    return lax.dot_general(a, b, NT_DIMS, preferred_element_type=F32)


def _dot_tn(a, b):
    return lax.dot_general(a, b, TN_DIMS, preferred_element_type=F32)


def _sigmoid(z):
    return 0.5 * jnp.tanh(0.5 * z) + 0.5


def _softplus(z):
    return jnp.maximum(z, 0.0) + jnp.log(1.0 + jnp.exp(-jnp.abs(z)))


def _rms_mod(xv, sc, sh):
    inv = lax.rsqrt(jnp.mean(xv * xv, axis=-1, keepdims=True) + EPS)
    return xv * inv * (1.0 + sc) + sh


def _split_bf16(z):
    hi = z.astype(BF16)
    lo = (z - hi.astype(F32)).astype(BF16)
    return hi, lo


def _params(sem):
    return pltpu.CompilerParams(dimension_semantics=sem, vmem_limit_bytes=VMEM_LIMIT)


def _const_spec(shape):
    nd = len(shape)
    return pl.BlockSpec(shape, lambda *_: (0,) * nd, pipeline_mode=pl.Buffered(1))


def _mod_kernel(c_ref, w_ref, b_ref, o_ref):
    c = c_ref[...]
    ca = (c * _sigmoid(c)).astype(BF16)
    o_ref[0] = _dot(ca, w_ref[0].astype(BF16)) + b_ref[0]


def _adaln_mod(c, ada_w, ada_b):
    depth, d, d6 = ada_w.shape
    b = c.shape[0]
    bp = 8
    assert b <= bp and d6 % d == 0
    c_pad = jnp.zeros((bp, d), F32).at[:b].set(c)
    nblk = d6 // d
    out = pl.pallas_call(
        _mod_kernel,
        grid=(depth, nblk),
        in_specs=[
            pl.BlockSpec((bp, d), lambda i, j: (0, 0)),
            pl.BlockSpec((1, d, d), lambda i, j: (i, 0, j)),
            pl.BlockSpec((1, 1, d), lambda i, j: (i, 0, j)),
        ],
        out_specs=pl.BlockSpec((1, bp, d), lambda i, j: (i, 0, j)),
        out_shape=jax.ShapeDtypeStruct((depth, bp, d6), F32),
        compiler_params=_params(("arbitrary", "arbitrary")),
        name="adaln_mod",
    )(c_pad, ada_w, ada_b.reshape(depth, 1, d6))
    mod = out[:, :b].reshape(depth, b, nblk, d)
    return jnp.pad(mod, ((0, 0), (0, 0), (0, 8 - nblk), (0, 0)))


def _rwkv_proj_kernel(x_ref, xp_ref, mod_ref, mix_ref, vec_ref, wr_ref, wk_ref, wv_ref,
                      wwl_ref, wal_ref, wgl_ref, w2_ref, a2_ref, g2_ref,
                      r_out, k_out, v_out, kk_out, a_out, lw_out, g_out, hbuf, xbuf):
    tm = x_ref.shape[1]
    t = pl.program_id(1)
    sh = mod_ref[0, 0:1, :]
    sc = mod_ref[0, 1:2, :]
    hp = _rms_mod(xp_ref[0], sc, sh)
    hbuf[0:8, :] = jnp.where(t > 0, hp, 0.0)
    w0 = vec_ref[0:1, :]
    a0 = vec_ref[1:2, :]
    k_k = vec_ref[2:3, :]
    k_a = vec_ref[3:4, :]

    h = _rms_mod(x_ref[0], sc, sh)
    hbuf[8:, :] = h
    xbuf[...] = hbuf[pl.ds(7, tm), :] - h

    def mixed(i):
        return (hbuf[8:, :] + xbuf[...] * mix_ref[i:i + 1, :]).astype(BF16)

    def put(out, val, cb):
        for q in range(COL_BLOCK // LANES):
            out[0, cb * (COL_BLOCK // LANES) + q] = val[:, q * LANES:(q + 1) * LANES].astype(out.dtype)

    ncb = r_out.shape[1] * LANES // COL_BLOCK
    col = lambda cb: slice(cb * COL_BLOCK, (cb + 1) * COL_BLOCK)
    lhs = mixed(0)
    for cb in range(ncb):
        put(r_out, _dot(lhs, wr_ref[:, col(cb)]), cb)
    lhs = mixed(3)
    for cb in range(ncb):
        put(v_out, _dot(lhs, wv_ref[:, col(cb)]), cb)
    twl = jnp.tanh(_dot(mixed(1), wwl_ref[...])).astype(BF16)
    alb = _dot(mixed(4), wal_ref[...]).astype(BF16)
    sgl = _sigmoid(_dot(mixed(5), wgl_ref[...])).astype(BF16)
    a_blocks = []
    for cb in range(ncb):
        c = col(cb)
        put(lw_out, -DECAY_SCALE * _sigmoid(w0[:, c] + _dot(twl, w2_ref[:, c])), cb)
        a = _sigmoid(a0[:, c] + _dot(alb, a2_ref[:, c]))
        put(a_out, a, cb)
        a_blocks.append(a)
        put(g_out, _dot(sgl, g2_ref[:, c]), cb)
    lhs = mixed(2)
    for cb in range(ncb):
        c = col(cb)
        k = _dot(lhs, wk_ref[:, c])
        put(kk_out, k * k_k[:, c], cb)
        put(k_out, k * (1.0 + (a_blocks[cb] - 1.0) * k_a[:, c]), cb)


def _rwkv_proj(x, mod, mix, vecs, weights, tm):
    b, s, d = x.shape
    assert s % tm == 0 and d % COL_BLOCK == 0
    npair = d // LANES
    nt = s // tm
    blk8 = tm // 8
    pair_shape = lambda dt: jax.ShapeDtypeStruct((b, npair, s, LANES), dt)
    pair_spec = pl.BlockSpec((1, npair, tm, LANES), lambda i, j: (i, 0, j, 0))
    return pl.pallas_call(
        _rwkv_proj_kernel,
        grid=(b, nt),
        in_specs=[
            pl.BlockSpec((1, tm, d), lambda i, j: (i, j, 0)),
            pl.BlockSpec((1, 8, d), lambda i, j: (i, jnp.maximum(j * blk8 - 1, 0), 0)),
            pl.BlockSpec((1, 8, d), lambda i, j: (i, 0, 0)),
            _const_spec(mix.shape),
            _const_spec(vecs.shape),
        ] + [_const_spec(w.shape) for w in weights],
        out_specs=[pair_spec] * 7,
        out_shape=[pair_shape(BF16)] * 5 + [pair_shape(F32), pair_shape(BF16)],
        scratch_shapes=[pltpu.VMEM((tm + 8, d), F32), pltpu.VMEM((tm, d), F32)],
        compiler_params=_params(("parallel", "arbitrary")),
        name="rwkv_proj",
    )(x, x, mod, mix, vecs, *weights)


def _rwkv_core_kernel(r_ref, k_ref, v_ref, kk_ref, a_ref, lw_ref, g_ref, vec_ref, vec_prev_ref,
                      o_ref, s_ref, rbar_s, y0_s, mlrt_s, s0_s, wl_s, g_s, bg_s, *, tiles_per_seq):
    npg = r_ref.shape[1]
    tt = r_ref.shape[2]
    L = CHUNK
    nch = tt // L
    nunit = npg * nch
    step = pl.program_id(0)

    @pl.when(step == 0)
    def _():
        for ref in (s_ref, rbar_s, y0_s, mlrt_s, s0_s, wl_s, g_s, bg_s):
            ref[...] = jnp.zeros_like(ref)

    fresh = (jnp.maximum(step - 1, 0) % tiles_per_seq) == 0

    lane = lax.broadcasted_iota(jnp.int32, (L, LANES), 1)
    m0 = lane < RW_HEAD
    rowl = lax.broadcasted_iota(jnp.int32, (L, LANES), 0)
    strict = rowl > (lane % L)
    incl = rowl >= (lane % L)
    in_block = [(rowl // b) == ((lane % L) // b) for b in (4, 16)]
    eye_cat = (rowl == (lane % L)).astype(F32)
    row2 = lax.broadcasted_iota(jnp.int32, (LANES, LANES), 0)
    col2 = lax.broadcasted_iota(jnp.int32, (LANES, LANES), 1)
    blockdiag = (row2 // RW_HEAD) == (col2 // RW_HEAD)
    rowt = lax.broadcasted_iota(jnp.int32, (tt, tt), 0)
    colt = lax.broadcasted_iota(jnp.int32, (tt, tt), 1)
    tri = (((rowt // L) == (colt // L)) & (rowt >= colt)).astype(BF16)
    zeros_l = jnp.zeros((L, LANES), BF16)

    def head_sum(z):
        lo = jnp.sum(jnp.where(m0, z, 0.0), axis=-1, keepdims=True)
        hi = jnp.sum(jnp.where(m0, 0.0, z), axis=-1, keepdims=True)
        return jnp.where(m0, lo, hi)

    def split_heads(z):
        zb = z.astype(BF16)
        zero = jnp.zeros_like(zb)
        return jnp.concatenate([jnp.where(m0, zb, zero), jnp.where(m0, zero, zb)], axis=0)

    def cat_dot(xc, yc):
        return _dot(xc.astype(BF16), split_heads(yc))

    units = [(p, c) for p in range(npg) for c in range(nch)]

    cums = []
    for p in range(npg):
        hi, lo = _split_bf16(lw_ref[0, p])
        both = _dot(tri, jnp.concatenate([hi, lo], axis=1))
        cums.append(both[:, :LANES] + both[:, LANES:])

    pu = []
    for (p, c) in units:
        rows = slice(c * L, (c + 1) * L)
        r = r_ref[0, p, rows, :].astype(F32)
        k = k_ref[0, p, rows, :].astype(F32)
        v = v_ref[0, p, rows, :]
        kk = kk_ref[0, p, rows, :].astype(F32)
        a = a_ref[0, p, rows, :].astype(F32)
        g = g_ref[0, p, rows, :]
        lw = lw_ref[0, p, rows, :]
        cum = cums[p][rows, :]
        kkn = kk * jnp.minimum(lax.rsqrt(head_sum(kk * kk)), 1e12)
        bvec = kkn * a
        c_last = cum[L - 1:L, :]
        at = -kkn * jnp.exp(cum - lw)
        rt = r * jnp.exp(cum)
        inv_w = jnp.exp(-cum)
        d_last = jnp.exp(c_last - cum)
        lanes = slice(p * LANES, (p + 1) * LANES)
        bonus = head_sum(r * k * vec_ref[0:1, lanes]) * v.astype(F32)
        pu.append(dict(
            rt=rt, v=v, w_last=jnp.exp(c_last), g=g, bg=bonus * g.astype(F32),
            lhs=jnp.concatenate([at, rt], axis=0).astype(BF16),
            rhs=jnp.concatenate([split_heads(bvec * inv_w), split_heads(k * inv_w)], axis=0),
            dk=jnp.concatenate([bvec * d_last, k * d_last], axis=0).astype(BF16),
            at_s=split_heads(at), v_s=split_heads(v),
        ))

    states = [jnp.where(fresh, 0.0, s_ref[p]) for p in range(npg)]
    ys = {}

    def scan_level(c):
        for p in range(npg):
            j = p * nch + c
            s_old = states[p]
            sb = s_old.astype(BF16)
            ys[j] = _dot_nt(rbar_s[j], sb) + y0_s[j]
            states[p] = s_old * wl_s[j] + _dot_nt(sb, mlrt_s[j]) + s0_s[j]

    per_level = -(-nunit // nch)
    for j, u in enumerate(pu):
        if j % per_level == 0:
            scan_level(j // per_level)
        gram = _dot_nt(u["lhs"], u["rhs"])
        a_ab = jnp.where(strict, gram[:L, :LANES], 0.0)
        u["a_lvl"] = [jnp.where(in_block[0], a_ab, 0.0),
                      jnp.where(in_block[1] & ~in_block[0], a_ab, 0.0),
                      jnp.where(in_block[1], 0.0, a_ab)]
        u["r_b"] = jnp.where(incl, gram[L:, :LANES], 0.0).astype(BF16)
        u["ak_rk"] = jnp.concatenate([jnp.where(strict, gram[:L, LANES:], 0.0),
                                      jnp.where(incl, gram[L:, LANES:], 0.0)], axis=0).astype(BF16)
    for p in range(npg):
        s_ref[p] = states[p]

    for j in range(nunit):
        p, c = divmod(j, nch)
        lanes = slice(p * LANES, (p + 1) * LANES)
        y = ys[j]
        mu = head_sum(y) * (1.0 / RW_HEAD)
        yc = y - mu
        var = head_sum(yc * yc) * (1.0 / RW_HEAD)
        yn = yc * lax.rsqrt(var + GN_EPS) * vec_prev_ref[1:2, lanes] + vec_prev_ref[2:3, lanes]
        o_ref[0, c * L:(c + 1) * L, lanes] = (yn * g_s[j].astype(F32) + bg_s[j]).astype(o_ref.dtype)

    for u in pu:
        both = _dot(u["ak_rk"], u["v_s"])
        u["akv"] = both[:L]
        u["rkv"] = both[L:]

    def short_inverse(key_in, key_out):
        for u in pu:
            u["_sq"] = cat_dot(u[key_in], u[key_in])
        for u in pu:
            first = eye_cat + u[key_in]
            u[key_out] = first + cat_dot(first, u["_sq"])

    def cat_dot2(xc, y1, y2):
        both = _dot(xc.astype(BF16), jnp.concatenate([split_heads(y1), split_heads(y2)], axis=1))
        return both[:, :LANES], both[:, LANES:]

    for u in pu:
        u["p0"] = u["a_lvl"][0]
    short_inverse("p0", "t0")
    for u in pu:
        u["n1"], u["x2"] = cat_dot2(u["t0"], u["a_lvl"][1], u["a_lvl"][2])
    short_inverse("n1", "r1")
    for u in pu:
        u["t1"], u["n2"] = cat_dot2(u["r1"], u["t0"], u["x2"])
    short_inverse("n2", "r2")
    for u in pu:
        u["tinv"] = cat_dot(u["r2"], u["t1"])

    for u in pu:
        xin = jnp.concatenate([u["at_s"], split_heads(u["akv"])], axis=1)
        u["z"] = _dot(u["tinv"].astype(BF16), xin)
    for u in pu:
        z = u["z"]
        rhs = jnp.concatenate([split_heads(z[:, :LANES]), split_heads(z[:, LANES:])], axis=1)
        u["w"] = _dot(u["r_b"], rhs)
    for u in pu:
        bot = jnp.concatenate([zeros_l, u["v"]], axis=1)
        u["ms"] = _dot_tn(u["dk"], jnp.concatenate([u["z"].astype(BF16), bot], axis=0))

    for j, u in enumerate(pu):
        rbar_s[j] = (u["rt"] + u["w"][:, :LANES]).astype(BF16)
        y0_s[j] = u["w"][:, LANES:] + u["rkv"]
        mlrt_s[j] = jnp.where(blockdiag, u["ms"][:, :LANES], 0.0).astype(BF16)
        s0_s[j] = jnp.where(blockdiag, u["ms"][:, LANES:].T, 0.0)
        wl_s[j] = u["w_last"]
        g_s[j] = u["g"]
        bg_s[j] = u["bg"]


def _rwkv_core(proj, vecs, b, s, d, tt, npg):
    assert d % (2 * RW_HEAD) == 0 and (d // LANES) % npg == 0 and s % tt == 0 and tt % CHUNK == 0
    npair = d // LANES
    ngrp = npair // npg
    nt = s // tt
    ntiles = b * ngrp * nt
    nunit = npg * (tt // CHUNK)

    def tile_coords(tile):
        return tile // (ngrp * nt), (tile // nt) % ngrp, tile % nt

    def in_map(i):
        bi, gi, ti = tile_coords(jnp.minimum(i, ntiles - 1))
        return (bi, gi, ti, 0)

    def out_map(i):
        bi, gi, ti = tile_coords(jnp.maximum(i - 1, 0))
        return (bi, ti, gi)

    in_spec = pl.BlockSpec((1, npg, tt, LANES), in_map)
    vec_spec = pl.BlockSpec((8, npg * LANES), lambda i: (0, tile_coords(jnp.minimum(i, ntiles - 1))[1]))
    vec_prev_spec = pl.BlockSpec((8, npg * LANES), lambda i: (0, tile_coords(jnp.maximum(i - 1, 0))[1]))
    L = CHUNK
    return pl.pallas_call(
        functools.partial(_rwkv_core_kernel, tiles_per_seq=nt),
        grid=(ntiles + 1,),
        in_specs=[in_spec] * 7 + [vec_spec, vec_prev_spec],
        out_specs=pl.BlockSpec((1, tt, npg * LANES), out_map),
        out_shape=jax.ShapeDtypeStruct((b, s, d), BF16),
        scratch_shapes=[
            pltpu.VMEM((npg, LANES, LANES), F32),
            pltpu.VMEM((nunit, L, LANES), BF16),
            pltpu.VMEM((nunit, L, LANES), F32),
            pltpu.VMEM((nunit, LANES, LANES), BF16),
            pltpu.VMEM((nunit, LANES, LANES), F32),
            pltpu.VMEM((nunit, 1, LANES), F32),
            pltpu.VMEM((nunit, L, LANES), BF16),
            pltpu.VMEM((nunit, L, LANES), F32),
        ],
        compiler_params=_params(("arbitrary",)),
        name="rwkv_core",
    )(*proj, vecs, vecs)


def _post_ffn_kernel(x_ref, y_ref, mod_ref, fw_ref, wo_ref, wgu_ref, wd_ref, o_ref, act_ref,
                     *, final):
    g1 = mod_ref[0, 2:3, :]
    sh2 = mod_ref[0, 3:4, :]
    sc2 = mod_ref[0, 4:5, :]
    g2 = mod_ref[0, 5:6, :]
    x1 = x_ref[0] + g1 * _dot(y_ref[0], wo_ref[...])
    h = _rms_mod(x1, sc2, sh2).astype(BF16)
    d_ff = act_ref.shape[1]
    for c0 in range(0, d_ff, COL_BLOCK):
        cols = slice(c0, c0 + COL_BLOCK)
        gate = _dot(h, wgu_ref[0, :, cols])
        up = _dot(h, wgu_ref[0, :, d_ff + c0:d_ff + c0 + COL_BLOCK])
        act_ref[:, cols] = (gate * _sigmoid(gate) * up).astype(BF16)
    x2 = x1 + g2 * _dot(act_ref[...], wd_ref[0])
    if final:
        inv = lax.rsqrt(jnp.mean(x2 * x2, axis=-1, keepdims=True) + EPS)
        x2 = x2 * inv * fw_ref[0:1, :]
    o_ref[0] = x2


def _post_ffn(x, y, mod, final_w, w_out, w_gu, w_d, layer, tm, final):
    b, s, d = x.shape
    d_ff = w_d.shape[1]
    assert d_ff % COL_BLOCK == 0 and s % tm == 0
    kern = functools.partial(_post_ffn_kernel, final=final)
    tok = lambda i, j: (i, j, 0)
    layer_spec = lambda w: pl.BlockSpec((1,) + w.shape[1:], lambda i, j: (layer, 0, 0),
                                        pipeline_mode=pl.Buffered(1))
    return pl.pallas_call(
        kern,
        grid=(b, s // tm),
        in_specs=[
            pl.BlockSpec((1, tm, d), tok),
            pl.BlockSpec((1, tm, d), tok),
            pl.BlockSpec((1, 8, d), lambda i, j: (i, 0, 0)),
            _const_spec(final_w.shape),
            _const_spec(w_out.shape),
            layer_spec(w_gu),
            layer_spec(w_d),
        ],
        out_specs=pl.BlockSpec((1, tm, d), tok),
        out_shape=jax.ShapeDtypeStruct((b, s, d), F32),
        scratch_shapes=[pltpu.VMEM((tm, d_ff), BF16)],
        compiler_params=_params(("parallel", "parallel")),
        name="post_ffn_final" if final else "post_ffn",
    )(x, y, mod, final_w, w_out, w_gu, w_d)


def _softcap(z):
    return GATE_CAP * jnp.tanh(z * (1.0 / GATE_CAP))


def _mlstm_proj_kernel(x_ref, xp_ref, mod_ref, cv_ref, gb_ref, wqk_ref, wv_ref,
                       wo_ref, wgi_ref, wgf_ref,
                       q_out, k_out, v_out, og_out, gi_out, gf_out):
    tm = x_ref.shape[1]
    t = pl.program_id(1)
    sh = mod_ref[0, 0:1, :]
    sc = mod_ref[0, 1:2, :]
    h = _rms_mod(x_ref[0], sc, sh).astype(BF16)
    hp = _rms_mod(xp_ref[0], sc, sh).astype(BF16)

    half = wqk_ref.shape[1] // 2
    first_row = lax.broadcasted_iota(jnp.int32, (tm, COL_BLOCK), 0) == 0
    for c0 in range(0, wqk_ref.shape[1], COL_BLOCK):
        c = slice(c0, c0 + COL_BLOCK)
        pqk = _dot(h, wqk_ref[:, c])
        ppv = jnp.where(t > 0, _dot(hp, wqk_ref[:, c]), 0.0)
        acc = pqk * cv_ref[0:1, c]
        acc_prev = ppv * cv_ref[0:1, c]
        for j in range(1, CONV_W):
            wj = cv_ref[j:j + 1, c]
            carry = acc_prev[7:8, :]
            acc = pqk * wj + jnp.where(first_row, carry, pltpu.roll(acc, 1, 0))
            acc_prev = ppv * wj + pltpu.roll(acc_prev, 1, 0)
        conv = acc + cv_ref[CONV_W:CONV_W + 1, c]
        qk = conv * _sigmoid(conv)
        if c0 < half:
            q_out[0, :, c] = (qk * (ML_QK ** -0.5)).astype(q_out.dtype)
        else:
            k_out[0, :, c0 - half:c0 - half + COL_BLOCK] = qk.astype(k_out.dtype)
    for c0 in range(0, wv_ref.shape[1], COL_BLOCK):
        c = slice(c0, c0 + COL_BLOCK)
        v_out[0, :, c] = _dot(h, wv_ref[:, c]).astype(v_out.dtype)
    for c0 in range(0, wo_ref.shape[1], COL_BLOCK):
        c = slice(c0, c0 + COL_BLOCK)
        og_out[0, :, c] = _sigmoid(_dot(h, wo_ref[:, c])).astype(og_out.dtype)

    lane = lax.broadcasted_iota(jnp.int32, gi_out.shape[1:], 1)
    zi = _dot(h, wgi_ref[...]) + gb_ref[0:1, :]
    zf = _dot(h, wgf_ref[...]) + gb_ref[1:2, :]
    gi_out[0] = jnp.where(lane < ML_HEADS, _softcap(zi), 0.0)
    gf_out[0] = jnp.where(lane < ML_HEADS, -_softplus(-_softcap(zf)), 0.0)


def _mlstm_proj(x, mod, cv, gb, weights, tm):
    b, s, d = x.shape
    w_qk, w_v, w_o, w_gi, w_gf = weights
    nqk = w_qk.shape[1] // 2
    nv = w_v.shape[1]
    assert s % tm == 0 and w_qk.shape[1] % COL_BLOCK == 0 and nv % COL_BLOCK == 0
    ng = w_gi.shape[1]
    blk8 = tm // 8
    tok = lambda i, j: (i, j, 0)
    return pl.pallas_call(
        _mlstm_proj_kernel,
        grid=(b, s // tm),
        in_specs=[
            pl.BlockSpec((1, tm, d), tok),
            pl.BlockSpec((1, 8, d), lambda i, j: (i, jnp.maximum(j * blk8 - 1, 0), 0)),
            pl.BlockSpec((1, 8, d), lambda i, j: (i, 0, 0)),
            _const_spec(cv.shape),
            _const_spec(gb.shape),
        ] + [_const_spec(w.shape) for w in weights],
        out_specs=[
            pl.BlockSpec((1, tm, nqk), tok),
            pl.BlockSpec((1, tm, nqk), tok),
            pl.BlockSpec((1, tm, nv), tok),
            pl.BlockSpec((1, tm, nv), tok),
            pl.BlockSpec((1, tm, ng), tok),
            pl.BlockSpec((1, tm, ng), tok),
        ],
        out_shape=[
            jax.ShapeDtypeStruct((b, s, nqk), BF16),
            jax.ShapeDtypeStruct((b, s, nqk), BF16),
            jax.ShapeDtypeStruct((b, s, nv), BF16),
            jax.ShapeDtypeStruct((b, s, nv), BF16),
            jax.ShapeDtypeStruct((b, s, ng), F32),
            jax.ShapeDtypeStruct((b, s, ng), F32),
        ],
        compiler_params=_params(("parallel", "arbitrary")),
        name="mlstm_proj",
    )(x, x, mod, cv, gb, *weights)


def _split3_bf16(z):
    hi = z.astype(BF16)
    r1 = z - hi.astype(F32)
    mid = r1.astype(BF16)
    lo = (r1 - mid.astype(F32)).astype(BF16)
    return jnp.concatenate([hi, mid, lo], axis=1)


def _mlstm_core_kernel(q_ref, k_ref, v_ref, og_ref, gi_ref, gf_ref, hn_ref, e64_ref, e128_ref,
                       o_ref, c_ref, m_ref):
    tt = q_ref.shape[1]
    L = CHUNK
    nch = tt // L
    npair = ML_HEADS // 2
    t = pl.program_id(1)

    @pl.when(t == 0)
    def _():
        c_ref[...] = jnp.zeros_like(c_ref)
        m_ref[...] = jnp.zeros_like(m_ref)

    lane = lax.broadcasted_iota(jnp.int32, (L, LANES), 1)
    rowl = lax.broadcasted_iota(jnp.int32, (L, LANES), 0)
    m0 = lane < ML_QK
    src = lane % L
    rowt = lax.broadcasted_iota(jnp.int32, (tt, tt), 0)
    colt = lax.broadcasted_iota(jnp.int32, (tt, tt), 1)
    tri = (((rowt // L) == (colt // L)) & (rowt >= colt)).astype(BF16)
    row_in_chunk = lax.broadcasted_iota(jnp.int32, (tt, LANES), 0) % L
    row_first = lax.broadcasted_iota(jnp.int32, (LANES, 1), 0) < ML_QK
    ones = jnp.ones((L, LANES), BF16)

    def split_heads(zv):
        zb = zv.astype(BF16)
        zero = jnp.zeros_like(zb)
        return jnp.concatenate([jnp.where(m0, zb, zero), jnp.where(m0, zero, zb)], axis=0)

    gi = gi_ref[0]
    b3 = _dot(tri, _split3_bf16(gf_ref[0]))
    bcum = b3[:, :LANES] + b3[:, LANES:2 * LANES] + b3[:, 2 * LANES:]
    z = gi - bcum
    cmax = z
    shift = 1
    while shift < L:
        cmax = jnp.where(row_in_chunk >= shift, jnp.maximum(cmax, pltpu.roll(cmax, shift, 0)), cmax)
        shift *= 2
    m_intra = bcum + cmax

    m_prev = m_ref[0:1, :]
    mp_rows, mn_rows, gt_rows, s_a = [], [], [], []
    for c in range(nch):
        last = c * L + L - 1
        g_tot = bcum[last:last + 1, :]
        m_loc = g_tot + cmax[last:last + 1, :]
        m_new = jnp.maximum(g_tot + m_prev, m_loc)
        mp_rows.append(jnp.broadcast_to(m_prev, (L, LANES)))
        mn_rows.append(jnp.broadcast_to(m_new, (L, LANES)))
        gt_rows.append(jnp.broadcast_to(g_tot, (L, LANES)))
        s_a.append(jnp.exp(g_tot + m_prev - m_new))
        m_prev = m_new
    m_ref[0:1, :] = m_prev
    inter = bcum + jnp.concatenate(mp_rows, axis=0)
    m_t = jnp.maximum(inter, m_intra)
    w_inter = jnp.exp(inter - m_t)
    kws = jnp.exp(jnp.concatenate(gt_rows, axis=0) + z - jnp.concatenate(mn_rows, axis=0))

    head_lane = lax.broadcasted_iota(jnp.int32, (tt, LANES), 1) < ML_HEADS

    def pack3(val):
        val = jnp.where(head_lane, val, 0.0)
        hi = val.astype(BF16).astype(F32)
        r1 = val - hi
        mid = r1.astype(BF16).astype(F32)
        lo = (r1 - mid).astype(BF16).astype(F32)
        packed = hi + pltpu.roll(mid, ML_HEADS, 1) + pltpu.roll(lo, 2 * ML_HEADS, 1)
        return packed.astype(BF16)

    spread = _dot(jnp.concatenate([pack3(bcum - m_t), pack3(z), pack3(w_inter), pack3(kws)], axis=0),
                  e64_ref[...])
    a_exp, z_exp, wi_exp, kw_exp = (spread[i * tt:(i + 1) * tt] for i in range(4))
    em_exp = _dot(pack3(jnp.exp(-m_t)), e128_ref[...])

    un = []
    for c in range(nch):
        rows = slice(c * L, (c + 1) * L)
        for p in range(npair):
            lanes = slice(p * LANES, (p + 1) * LANES)
            qp = q_ref[0, rows, lanes]
            kp = k_ref[0, rows, lanes]
            v0 = v_ref[0, rows, 2 * p * ML_V:(2 * p + 1) * ML_V]
            v1 = v_ref[0, rows, (2 * p + 1) * ML_V:(2 * p + 2) * ML_V]
            z_src = jnp.sum(jnp.where(rowl == src, z_exp[rows, lanes], 0.0), axis=0, keepdims=True)
            un.append(dict(
                c=c, p=p, rows=rows, qp=qp,
                k2=split_heads(kp),
                expd=jnp.where(rowl >= src, jnp.exp(a_exp[rows, lanes] + z_src), 0.0),
                qs=split_heads(qp * wi_exp[rows, lanes]),
                kw=split_heads(kp * kw_exp[rows, lanes]),
                vst=jnp.concatenate([jnp.concatenate([v0, ones], axis=1),
                                     jnp.concatenate([v1, ones], axis=1)], axis=0),
            ))

    for u in un:
        scores = _dot_nt(u["qp"], u["k2"])
        u["w_intra"] = split_heads(u["expd"] * scores)
    for u in un:
        u["c_loc"] = _dot_tn(u["kw"], u["vst"])

    states = [c_ref[p] for p in range(npair)]
    for u in un:
        c, p = u["c"], u["p"]
        u["c_old"] = states[p].astype(BF16)
        sa = jnp.where(row_first, s_a[c][:, 2 * p:2 * p + 1], s_a[c][:, 2 * p + 1:2 * p + 2])
        states[p] = sa * states[p] + u["c_loc"]
    for p in range(npair):
        c_ref[p] = states[p]

    for u in un:
        u["nd"] = _dot(jnp.concatenate([u["qs"], u["w_intra"]], axis=1),
                       jnp.concatenate([u["c_old"], u["vst"]], axis=0))

    for u in un:
        rows = u["rows"]
        for hh in range(2):
            head = 2 * u["p"] + hh
            cols = slice(head * ML_V, (head + 1) * ML_V)
            num = u["nd"][hh * L:(hh + 1) * L, :LANES]
            den = u["nd"][hh * L:(hh + 1) * L, LANES:]
            den = jnp.maximum(jnp.abs(den), em_exp[rows, cols])
            hv = num / den
            hv = hv * lax.rsqrt(jnp.mean(hv * hv, axis=-1, keepdims=True) + EPS)
            hv = hv * hn_ref[0:1, cols]
            o_ref[0, rows, cols] = (hv * og_ref[0, rows, cols]).astype(o_ref.dtype)


def _expand_matrix(width):
    rows = jnp.arange(LANES)[:, None]
    cols = jnp.arange(ML_HEADS * width)[None, :]
    return ((rows < 3 * ML_HEADS) & (rows % ML_HEADS == cols // width)).astype(BF16)


def _mlstm_core(q, k, v, og, gi, gf, hn_w, tt):
    b, s, dv = v.shape
    assert s % tt == 0 and tt % CHUNK == 0 and dv == ML_HEADS * ML_V
    nqk = q.shape[2]
    ng = gi.shape[2]
    e64 = _expand_matrix(ML_QK)
    e128 = _expand_matrix(ML_V)
    tok = lambda i, j: (i, j, 0)
    return pl.pallas_call(
        _mlstm_core_kernel,
        grid=(b, s // tt),
        in_specs=[
            pl.BlockSpec((1, tt, nqk), tok),
            pl.BlockSpec((1, tt, nqk), tok),
            pl.BlockSpec((1, tt, dv), tok),
            pl.BlockSpec((1, tt, dv), tok),
            pl.BlockSpec((1, tt, ng), tok),
            pl.BlockSpec((1, tt, ng), tok),
            _const_spec(hn_w.shape),
            _const_spec(e64.shape),
            _const_spec(e128.shape),
        ],
        out_specs=pl.BlockSpec((1, tt, dv), tok),
        out_shape=jax.ShapeDtypeStruct((b, s, dv), BF16),
        scratch_shapes=[
            pltpu.VMEM((ML_HEADS // 2, LANES, 2 * LANES), F32),
            pltpu.VMEM((8, LANES), F32),
        ],
        compiler_params=_params(("parallel", "arbitrary")),
        name="mlstm_core",
    )(q, k, v, og, gi, gf, hn_w, e64, e128)


def _pad_rows(a, rows=8):
    return jnp.pad(a, ((0, rows - a.shape[0]), (0, 0)))


def kernel(x, c, ada_w, ada_b, rw_mix, rw_w_in, rw_w0, rw_w2, rw_a0, rw_a2, rw_g2, rw_k_k, rw_k_a, rw_r_k, rw_gn_w, rw_gn_b, rw_w_out, ml_w_in, ml_conv_w, ml_conv_b, ml_b_i, ml_b_f, ml_hn_w, ml_w_out, ffn_w_gu, ffn_w_down, final_w):
    b, s, d = x.shape
    bf = lambda w: w.astype(BF16)
    final_w2 = final_w.reshape(1, d)

    mod = _adaln_mod(c, ada_w, ada_b)

    w_in = rw_w_in[0]
    o1 = d
    o2 = o1 + LORA_W
    o3 = o2 + d
    o4 = o3 + d
    o5 = o4 + LORA_A
    rw_weights = [bf(w_in[:, :o1]), bf(w_in[:, o2:o3]), bf(w_in[:, o3:o4]),
                  bf(w_in[:, o1:o2]), bf(w_in[:, o4:o5]), bf(w_in[:, o5:]),
                  bf(rw_w2[0]), bf(rw_a2[0]), bf(rw_g2[0])]
    mix = _pad_rows(rw_mix[0])
    vecs = _pad_rows(jnp.stack([rw_w0[0], rw_a0[0], rw_k_k[0], rw_k_a[0]]))
    proj = _rwkv_proj(x, mod[0], mix, vecs, rw_weights, TM_RWKV_PROJ)
    core_vecs = _pad_rows(jnp.stack([rw_r_k[0].reshape(d), rw_gn_w[0], rw_gn_b[0]]))
    y = _rwkv_core(proj, core_vecs, b, s, d, TT_RWKV, PAIRS_RWKV)
    w_gu = bf(ffn_w_gu)
    w_down = bf(ffn_w_down)
    x = _post_ffn(x, y, mod[0], final_w2, bf(rw_w_out[0]), w_gu, w_down, 0, TM_DENSE, False)

    w_in = ml_w_in[0]
    nqk = 2 * ML_HEADS * ML_QK
    nv = ML_HEADS * ML_V
    pad_lanes = lambda w: jnp.pad(w, ((0, 0), (0, LANES - w.shape[1])))
    o_g = nqk + 2 * nv
    ml_weights = [bf(w_in[:, :nqk]), bf(w_in[:, nqk:nqk + nv]), bf(w_in[:, nqk + nv:o_g]),
                  bf(pad_lanes(w_in[:, o_g:o_g + ML_HEADS])), bf(pad_lanes(w_in[:, o_g + ML_HEADS:]))]
    cv = _pad_rows(jnp.concatenate([ml_conv_w[0], ml_conv_b[0][None, :]], axis=0))
    gb = _pad_rows(pad_lanes(jnp.stack([ml_b_i[0], ml_b_f[0]])))
    q, k, v, og, gi, gf = _mlstm_proj(x, mod[1], cv, gb, ml_weights, TM_DENSE)
    y = _mlstm_core(q, k, v, og, gi, gf, ml_hn_w[0].reshape(1, nv), TT_MLSTM)
    x = _post_ffn(x, y, mod[1], final_w2, bf(ml_w_out[0]), w_gu, w_down, 1, TM_DENSE, True)
    return x
```

```python
import functools

import jax
import jax.numpy as jnp
from jax import lax
from jax.experimental import pallas as pl
from jax.experimental.pallas import tpu as pltpu

F32 = jnp.float32
BF16 = jnp.bfloat16

EPS = 1e-6
GN_EPS = 64e-5
GATE_CAP = 15.0
DECAY_SCALE = 0.6065306597126334
CHUNK = 64
LANES = 128
RW_HEAD = 64
LORA_W = 64
LORA_A = 64
ML_HEADS = 8
ML_QK = 64
ML_V = 128
CONV_W = 4
PREV_ROWS = 16
COL_BLOCK = 256
VMEM_LIMIT = 56 * 1024 * 1024
TM_RWKV_PROJ = 512
TM_DENSE = 1024
TT_RWKV = 256
PAIRS_RWKV = 8
TT_MLSTM = 512

NT_DIMS = (((1,), (1,)), ((), ()))
TN_DIMS = (((0,), (0,)), ((), ()))


def _dot(a, b):
    return jnp.dot(a, b, preferred_element_type=F32)


def _dot_nt(a, b):
    return lax.dot_general(a, b, NT_DIMS, preferred_element_type=F32)


def _dot_tn(a, b):
    return lax.dot_general(a, b, TN_DIMS, preferred_element_type=F32)


def _sigmoid(z):
    return 0.5 * jnp.tanh(0.5 * z) + 0.5


def _softplus(z):
    return jnp.maximum(z, 0.0) + jnp.log(1.0 + jnp.exp(-jnp.abs(z)))


def _rms_mod(xv, sc, sh):
    inv = lax.rsqrt(jnp.mean(xv * xv, axis=-1, keepdims=True) + EPS)
    return xv * inv * (1.0 + sc) + sh


def _split_bf16(z):
    hi = z.astype(BF16)
    lo = (z - hi.astype(F32)).astype(BF16)
    return hi, lo


def _params(sem):
    return pltpu.CompilerParams(dimension_semantics=sem, vmem_limit_bytes=VMEM_LIMIT)


def _const_spec(shape):
    nd = len(shape)
    return pl.BlockSpec(shape, lambda *_: (0,) * nd, pipeline_mode=pl.Buffered(1))


def _mod_kernel(c_ref, w_ref, b_ref, o_ref):
    c = c_ref[...]
    ca = (c * _sigmoid(c)).astype(BF16)
    o_ref[0] = _dot(ca, w_ref[0].astype(BF16)) + b_ref[0]


def _adaln_mod(c, ada_w, ada_b):
    depth, d, d6 = ada_w.shape
    b = c.shape[0]
    bp = 8
    assert b <= bp and d6 % d == 0
    c_pad = jnp.zeros((bp, d), F32).at[:b].set(c)
    nblk = d6 // d
    out = pl.pallas_call(
        _mod_kernel,
        grid=(depth, nblk),
        in_specs=[
            pl.BlockSpec((bp, d), lambda i, j: (0, 0)),
            pl.BlockSpec((1, d, d), lambda i, j: (i, 0, j)),
            pl.BlockSpec((1, 1, d), lambda i, j: (i, 0, j)),
        ],
        out_specs=pl.BlockSpec((1, bp, d), lambda i, j: (i, 0, j)),
        out_shape=jax.ShapeDtypeStruct((depth, bp, d6), F32),
        compiler_params=_params(("arbitrary", "arbitrary")),
        name="adaln_mod",
    )(c_pad, ada_w, ada_b.reshape(depth, 1, d6))
    mod = out[:, :b].reshape(depth, b, nblk, d)
    return jnp.pad(mod, ((0, 0), (0, 0), (0, 8 - nblk), (0, 0)))


def _rwkv_proj_kernel(x_ref, xp_ref, mod_ref, mix_ref, vec_ref, wr_ref, wk_ref, wv_ref,
                      wwl_ref, wal_ref, wgl_ref, w2_ref, a2_ref, g2_ref,
                      r_out, k_out, v_out, kk_out, a_out, lw_out, g_out, hbuf, xbuf):
    tm = x_ref.shape[1]
    t = pl.program_id(1)
    sh = mod_ref[0, 0:1, :]
    sc = mod_ref[0, 1:2, :]
    hp = _rms_mod(xp_ref[0], sc, sh)
    hbuf[0:8, :] = jnp.where(t > 0, hp, 0.0)
    w0 = vec_ref[0:1, :]
    a0 = vec_ref[1:2, :]
    k_k = vec_ref[2:3, :]
    k_a = vec_ref[3:4, :]

    h = _rms_mod(x_ref[0], sc, sh)
    hbuf[8:, :] = h
    xbuf[...] = hbuf[pl.ds(7, tm), :] - h

    def mixed(i):
        return (hbuf[8:, :] + xbuf[...] * mix_ref[i:i + 1, :]).astype(BF16)

    def put(out, val, cb):
        for q in range(COL_BLOCK // LANES):
            out[0, cb * (COL_BLOCK // LANES) + q] = val[:, q * LANES:(q + 1) * LANES].astype(out.dtype)

    ncb = r_out.shape[1] * LANES // COL_BLOCK
    col = lambda cb: slice(cb * COL_BLOCK, (cb + 1) * COL_BLOCK)
    lhs = mixed(0)
    for cb in range(ncb):
        put(r_out, _dot(lhs, wr_ref[:, col(cb)]), cb)
    lhs = mixed(3)
    for cb in range(ncb):
        put(v_out, _dot(lhs, wv_ref[:, col(cb)]), cb)
    twl = jnp.tanh(_dot(mixed(1), wwl_ref[...])).astype(BF16)
    alb = _dot(mixed(4), wal_ref[...]).astype(BF16)
    sgl = _sigmoid(_dot(mixed(5), wgl_ref[...])).astype(BF16)
    a_blocks = []
    for cb in range(ncb):
        c = col(cb)
        put(lw_out, -DECAY_SCALE * _sigmoid(w0[:, c] + _dot(twl, w2_ref[:, c])), cb)
        a = _sigmoid(a0[:, c] + _dot(alb, a2_ref[:, c]))
        put(a_out, a, cb)
        a_blocks.append(a)
        put(g_out, _dot(sgl, g2_ref[:, c]), cb)
    lhs = mixed(2)
    for cb in range(ncb):
        c = col(cb)
        k = _dot(lhs, wk_ref[:, c])
        put(kk_out, k * k_k[:, c], cb)
        put(k_out, k * (1.0 + (a_blocks[cb] - 1.0) * k_a[:, c]), cb)


def _rwkv_proj(x, mod, mix, vecs, weights, tm):
    b, s, d = x.shape
    assert s % tm == 0 and d % COL_BLOCK == 0
    npair = d // LANES
    nt = s // tm
    blk8 = tm // 8
    pair_shape = lambda dt: jax.ShapeDtypeStruct((b, npair, s, LANES), dt)
    pair_spec = pl.BlockSpec((1, npair, tm, LANES), lambda i, j: (i, 0, j, 0))
    return pl.pallas_call(
        _rwkv_proj_kernel,
        grid=(b, nt),
        in_specs=[
            pl.BlockSpec((1, tm, d), lambda i, j: (i, j, 0)),
            pl.BlockSpec((1, 8, d), lambda i, j: (i, jnp.maximum(j * blk8 - 1, 0), 0)),
            pl.BlockSpec((1, 8, d), lambda i, j: (i, 0, 0)),
            _const_spec(mix.shape),
            _const_spec(vecs.shape),
        ] + [_const_spec(w.shape) for w in weights],
        out_specs=[pair_spec] * 7,
        out_shape=[pair_shape(BF16)] * 5 + [pair_shape(F32), pair_shape(BF16)],
        scratch_shapes=[pltpu.VMEM((tm + 8, d), F32), pltpu.VMEM((tm, d), F32)],
        compiler_params=_params(("parallel", "arbitrary")),
        name="rwkv_proj",
    )(x, x, mod, mix, vecs, *weights)


def _rwkv_core_kernel(r_ref, k_ref, v_ref, kk_ref, a_ref, lw_ref, g_ref, vec_ref, vec_prev_ref,
                      o_ref, s_ref, rbar_s, y0_s, mlrt_s, s0_s, wl_s, g_s, bg_s, *, tiles_per_seq):
    npg = r_ref.shape[1]
    tt = r_ref.shape[2]
    L = CHUNK
    nch = tt // L
    nunit = npg * nch
    step = pl.program_id(0)

    @pl.when(step == 0)
    def _():
        for ref in (s_ref, rbar_s, y0_s, mlrt_s, s0_s, wl_s, g_s, bg_s):
            ref[...] = jnp.zeros_like(ref)

    fresh = (jnp.maximum(step - 1, 0) % tiles_per_seq) == 0

    lane = lax.broadcasted_iota(jnp.int32, (L, LANES), 1)
    m0 = lane < RW_HEAD
    rowl = lax.broadcasted_iota(jnp.int32, (L, LANES), 0)
    strict = rowl > (lane % L)
    incl = rowl >= (lane % L)
    in_block = [(rowl // b) == ((lane % L) // b) for b in (4, 16)]
    eye_cat = (rowl == (lane % L)).astype(F32)
    row2 = lax.broadcasted_iota(jnp.int32, (LANES, LANES), 0)
    col2 = lax.broadcasted_iota(jnp.int32, (LANES, LANES), 1)
    blockdiag = (row2 // RW_HEAD) == (col2 // RW_HEAD)
    rowt = lax.broadcasted_iota(jnp.int32, (tt, tt), 0)
    colt = lax.broadcasted_iota(jnp.int32, (tt, tt), 1)
    tri = (((rowt // L) == (colt // L)) & (rowt >= colt)).astype(BF16)
    zeros_l = jnp.zeros((L, LANES), BF16)

    def head_sum(z):
        lo = jnp.sum(jnp.where(m0, z, 0.0), axis=-1, keepdims=True)
        hi = jnp.sum(jnp.where(m0, 0.0, z), axis=-1, keepdims=True)
        return jnp.where(m0, lo, hi)

    def split_heads(z):
        zb = z.astype(BF16)
        zero = jnp.zeros_like(zb)
        return jnp.concatenate([jnp.where(m0, zb, zero), jnp.where(m0, zero, zb)], axis=0)

    def cat_dot(xc, yc):
        return _dot(xc.astype(BF16), split_heads(yc))

    units = [(p, c) for p in range(npg) for c in range(nch)]

    cums = []
    for p in range(npg):
        hi, lo = _split_bf16(lw_ref[0, p])
        both = _dot(tri, jnp.concatenate([hi, lo], axis=1))
        cums.append(both[:, :LANES] + both[:, LANES:])

    pu = []
    for (p, c) in units:
        rows = slice(c * L, (c + 1) * L)
        r = r_ref[0, p, rows, :].astype(F32)
        k = k_ref[0, p, rows, :].astype(F32)
        v = v_ref[0, p, rows, :]
        kk = kk_ref[0, p, rows, :].astype(F32)
        a = a_ref[0, p, rows, :].astype(F32)
        g = g_ref[0, p, rows, :]
        lw = lw_ref[0, p, rows, :]
        cum = cums[p][rows, :]
        kkn = kk * jnp.minimum(lax.rsqrt(head_sum(kk * kk)), 1e12)
        bvec = kkn * a
        c_last = cum[L - 1:L, :]
        at = -kkn * jnp.exp(cum - lw)
        rt = r * jnp.exp(cum)
        inv_w = jnp.exp(-cum)
        d_last = jnp.exp(c_last - cum)
        lanes = slice(p * LANES, (p + 1) * LANES)
        bonus = head_sum(r * k * vec_ref[0:1, lanes]) * v.astype(F32)
        pu.append(dict(
            rt=rt, v=v, w_last=jnp.exp(c_last), g=g, bg=bonus * g.astype(F32),
            lhs=jnp.concatenate([at, rt], axis=0).astype(BF16),
            rhs=jnp.concatenate([split_heads(bvec * inv_w), split_heads(k * inv_w)], axis=0),
            dk=jnp.concatenate([bvec * d_last, k * d_last], axis=0).astype(BF16),
            at_s=split_heads(at), v_s=split_heads(v),
        ))

    states = [jnp.where(fresh, 0.0, s_ref[p]) for p in range(npg)]
    ys = {}

    def scan_level(c):
        for p in range(npg):
            j = p * nch + c
            s_old = states[p]
            sb = s_old.astype(BF16)
            ys[j] = _dot_nt(rbar_s[j], sb) + y0_s[j]
            states[p] = s_old * wl_s[j] + _dot_nt(sb, mlrt_s[j]) + s0_s[j]

    per_level = -(-nunit // nch)
    for j, u in enumerate(pu):
        if j % per_level == 0:
            scan_level(j // per_level)
        gram = _dot_nt(u["lhs"], u["rhs"])
        a_ab = jnp.where(strict, gram[:L, :LANES], 0.0)
        u["a_lvl"] = [jnp.where(in_block[0], a_ab, 0.0),
                      jnp.where(in_block[1] & ~in_block[0], a_ab, 0.0),
                      jnp.where(in_block[1], 0.0, a_ab)]
        u["r_b"] = jnp.where(incl, gram[L:, :LANES], 0.0).astype(BF16)
        u["ak_rk"] = jnp.concatenate([jnp.where(strict, gram[:L, LANES:], 0.0),
                                      jnp.where(incl, gram[L:, LANES:], 0.0)], axis=0).astype(BF16)
    for p in range(npg):
        s_ref[p] = states[p]

    for j in range(nunit):
        p, c = divmod(j, nch)
        lanes = slice(p * LANES, (p + 1) * LANES)
        y = ys[j]
        mu = head_sum(y) * (1.0 / RW_HEAD)
        yc = y - mu
        var = head_sum(yc * yc) * (1.0 / RW_HEAD)
        yn = yc * lax.rsqrt(var + GN_EPS) * vec_prev_ref[1:2, lanes] + vec_prev_ref[2:3, lanes]
        o_ref[0, c * L:(c + 1) * L, lanes] = (yn * g_s[j].astype(F32) + bg_s[j]).astype(o_ref.dtype)

    for u in pu:
        both = _dot(u["ak_rk"], u["v_s"])
        u["akv"] = both[:L]
        u["rkv"] = both[L:]

    def short_inverse(key_in, key_out):
        for u in pu:
            u["_sq"] = cat_dot(u[key_in], u[key_in])
        for u in pu:
            first = eye_cat + u[key_in]
            u[key_out] = first + cat_dot(first, u["_sq"])

    def cat_dot2(xc, y1, y2):
        both = _dot(xc.astype(BF16), jnp.concatenate([split_heads(y1), split_heads(y2)], axis=1))
        return both[:, :LANES], both[:, LANES:]

    for u in pu:
        u["p0"] = u["a_lvl"][0]
    short_inverse("p0", "t0")
    for u in pu:
        u["n1"], u["x2"] = cat_dot2(u["t0"], u["a_lvl"][1], u["a_lvl"][2])
    short_inverse("n1", "r1")
    for u in pu:
        u["t1"], u["n2"] = cat_dot2(u["r1"], u["t0"], u["x2"])
    short_inverse("n2", "r2")
    for u in pu:
        u["tinv"] = cat_dot(u["r2"], u["t1"])

    for u in pu:
        xin = jnp.concatenate([u["at_s"], split_heads(u["akv"])], axis=1)
        u["z"] = _dot(u["tinv"].astype(BF16), xin)
    for u in pu:
        z = u["z"]
        rhs = jnp.concatenate([split_heads(z[:, :LANES]), split_heads(z[:, LANES:])], axis=1)
        u["w"] = _dot(u["r_b"], rhs)
    for u in pu:
        bot = jnp.concatenate([zeros_l, u["v"]], axis=1)
        u["ms"] = _dot_tn(u["dk"], jnp.concatenate([u["z"].astype(BF16), bot], axis=0))

    for j, u in enumerate(pu):
        rbar_s[j] = (u["rt"] + u["w"][:, :LANES]).astype(BF16)
        y0_s[j] = u["w"][:, LANES:] + u["rkv"]
        mlrt_s[j] = jnp.where(blockdiag, u["ms"][:, :LANES], 0.0).astype(BF16)
        s0_s[j] = jnp.where(blockdiag, u["ms"][:, LANES:].T, 0.0)
        wl_s[j] = u["w_last"]
        g_s[j] = u["g"]
        bg_s[j] = u["bg"]


def _rwkv_core(proj, vecs, b, s, d, tt, npg):
    assert d % (2 * RW_HEAD) == 0 and (d // LANES) % npg == 0 and s % tt == 0 and tt % CHUNK == 0
    npair = d // LANES
    ngrp = npair // npg
    nt = s // tt
    ntiles = b * ngrp * nt
    nunit = npg * (tt // CHUNK)

    def tile_coords(tile):
        return tile // (ngrp * nt), (tile // nt) % ngrp, tile % nt

    def in_map(i):
        bi, gi, ti = tile_coords(jnp.minimum(i, ntiles - 1))
        return (bi, gi, ti, 0)

    def out_map(i):
        bi, gi, ti = tile_coords(jnp.maximum(i - 1, 0))
        return (bi, ti, gi)

    in_spec = pl.BlockSpec((1, npg, tt, LANES), in_map)
    vec_spec = pl.BlockSpec((8, npg * LANES), lambda i: (0, tile_coords(jnp.minimum(i, ntiles - 1))[1]))
    vec_prev_spec = pl.BlockSpec((8, npg * LANES), lambda i: (0, tile_coords(jnp.maximum(i - 1, 0))[1]))
    L = CHUNK
    return pl.pallas_call(
        functools.partial(_rwkv_core_kernel, tiles_per_seq=nt),
        grid=(ntiles + 1,),
        in_specs=[in_spec] * 7 + [vec_spec, vec_prev_spec],
        out_specs=pl.BlockSpec((1, tt, npg * LANES), out_map),
        out_shape=jax.ShapeDtypeStruct((b, s, d), BF16),
        scratch_shapes=[
            pltpu.VMEM((npg, LANES, LANES), F32),
            pltpu.VMEM((nunit, L, LANES), BF16),
            pltpu.VMEM((nunit, L, LANES), F32),
            pltpu.VMEM((nunit, LANES, LANES), BF16),
            pltpu.VMEM((nunit, LANES, LANES), F32),
            pltpu.VMEM((nunit, 1, LANES), F32),
            pltpu.VMEM((nunit, L, LANES), BF16),
            pltpu.VMEM((nunit, L, LANES), F32),
        ],
        compiler_params=_params(("arbitrary",)),
        name="rwkv_core",
    )(*proj, vecs, vecs)


def _post_ffn_kernel(x_ref, y_ref, mod_ref, modn_ref, fw_ref, wo_ref, wgu_ref, wd_ref, *rest,
                     final, emit_next):
    if emit_next:
        o_ref, hn_ref, act_ref = rest
    else:
        o_ref, act_ref = rest
    g1 = mod_ref[0, 2:3, :]
    sh2 = mod_ref[0, 3:4, :]
    sc2 = mod_ref[0, 4:5, :]
    g2 = mod_ref[0, 5:6, :]
    x1 = x_ref[0] + g1 * _dot(y_ref[0], wo_ref[...])
    h = _rms_mod(x1, sc2, sh2).astype(BF16)
    d_ff = act_ref.shape[1]
    for c0 in range(0, d_ff, COL_BLOCK):
        cols = slice(c0, c0 + COL_BLOCK)
        gate = _dot(h, wgu_ref[0, :, cols])
        up = _dot(h, wgu_ref[0, :, d_ff + c0:d_ff + c0 + COL_BLOCK])
        act_ref[:, cols] = (gate * _sigmoid(gate) * up).astype(BF16)
    x2 = x1 + g2 * _dot(act_ref[...], wd_ref[0])
    if final:
        inv = lax.rsqrt(jnp.mean(x2 * x2, axis=-1, keepdims=True) + EPS)
        x2 = x2 * inv * fw_ref[0:1, :]
    o_ref[0] = x2
    if emit_next:
        hn_ref[0] = _rms_mod(x2, modn_ref[0, 1:2, :], modn_ref[0, 0:1, :]).astype(hn_ref.dtype)


def _post_ffn(x, y, mod, final_w, w_out, w_gu, w_d, layer, tm, final, mod_next=None):
    b, s, d = x.shape
    d_ff = w_d.shape[1]
    assert d_ff % COL_BLOCK == 0 and s % tm == 0
    emit_next = mod_next is not None
    kern = functools.partial(_post_ffn_kernel, final=final, emit_next=emit_next)
    tok = lambda i, j: (i, j, 0)
    out_specs = [pl.BlockSpec((1, tm, d), tok)]
    out_shape = [jax.ShapeDtypeStruct((b, s, d), F32)]
    if emit_next:
        out_specs.append(pl.BlockSpec((1, tm, d), tok))
        out_shape.append(jax.ShapeDtypeStruct((b, s, d), BF16))
    layer_spec = lambda w: pl.BlockSpec((1,) + w.shape[1:], lambda i, j: (layer, 0, 0),
                                        pipeline_mode=pl.Buffered(1))
    outs = pl.pallas_call(
        kern,
        grid=(b, s // tm),
        in_specs=[
            pl.BlockSpec((1, tm, d), tok),
            pl.BlockSpec((1, tm, d), tok),
            pl.BlockSpec((1, 8, d), lambda i, j: (i, 0, 0)),
            pl.BlockSpec((1, 8, d), lambda i, j: (i, 0, 0)),
            _const_spec(final_w.shape),
            _const_spec(w_out.shape),
            layer_spec(w_gu),
            layer_spec(w_d),
        ],
        out_specs=out_specs,
        out_shape=out_shape,
        scratch_shapes=[pltpu.VMEM((tm, d_ff), BF16)],
        compiler_params=_params(("parallel", "parallel")),
        name="post_ffn_final" if final else "post_ffn",
    )(x, y, mod, mod_next if emit_next else mod, final_w, w_out, w_gu, w_d)
    return tuple(outs) if emit_next else outs[0]


def _softcap(z):
    return GATE_CAP * jnp.tanh(z * (1.0 / GATE_CAP))


def _mlstm_proj_kernel(h_ref, hp_ref, cv_ref, gb_ref, wqk_ref, wv_ref,
                       wo_ref, wgi_ref, wgf_ref,
                       q_out, k_out, v_out, og_out, gi_out, gf_out):
    tm = h_ref.shape[1]
    t = pl.program_id(1)
    h = h_ref[0]
    hp = hp_ref[0]

    half = wqk_ref.shape[1] // 2
    first_row = lax.broadcasted_iota(jnp.int32, (tm, COL_BLOCK), 0) == 0
    for c0 in range(0, wqk_ref.shape[1], COL_BLOCK):
        c = slice(c0, c0 + COL_BLOCK)
        pqk = _dot(h, wqk_ref[:, c])
        ppv = jnp.where(t > 0, _dot(hp, wqk_ref[:, c]), 0.0)
        acc = pqk * cv_ref[0:1, c]
        acc_prev = ppv * cv_ref[0:1, c]
        for j in range(1, CONV_W):
            wj = cv_ref[j:j + 1, c]
            carry = acc_prev[PREV_ROWS - 1:PREV_ROWS, :]
            acc = pqk * wj + jnp.where(first_row, carry, pltpu.roll(acc, 1, 0))
            acc_prev = ppv * wj + pltpu.roll(acc_prev, 1, 0)
        conv = acc + cv_ref[CONV_W:CONV_W + 1, c]
        qk = conv * _sigmoid(conv)
        if c0 < half:
            q_out[0, :, c] = (qk * (ML_QK ** -0.5)).astype(q_out.dtype)
        else:
            k_out[0, :, c0 - half:c0 - half + COL_BLOCK] = qk.astype(k_out.dtype)
    for c0 in range(0, wv_ref.shape[1], COL_BLOCK):
        c = slice(c0, c0 + COL_BLOCK)
        v_out[0, :, c] = _dot(h, wv_ref[:, c]).astype(v_out.dtype)
    for c0 in range(0, wo_ref.shape[1], COL_BLOCK):
        c = slice(c0, c0 + COL_BLOCK)
        og_out[0, :, c] = _sigmoid(_dot(h, wo_ref[:, c])).astype(og_out.dtype)

    lane = lax.broadcasted_iota(jnp.int32, gi_out.shape[1:], 1)
    zi = _dot(h, wgi_ref[...]) + gb_ref[0:1, :]
    zf = _dot(h, wgf_ref[...]) + gb_ref[1:2, :]
    gi_out[0] = jnp.where(lane < ML_HEADS, _softcap(zi), 0.0)
    gf_out[0] = jnp.where(lane < ML_HEADS, -_softplus(-_softcap(zf)), 0.0)


def _mlstm_proj(h, cv, gb, weights, tm):
    b, s, d = h.shape
    w_qk, w_v, w_o, w_gi, w_gf = weights
    nqk = w_qk.shape[1] // 2
    nv = w_v.shape[1]
    assert s % tm == 0 and w_qk.shape[1] % COL_BLOCK == 0 and nv % COL_BLOCK == 0
    ng = w_gi.shape[1]
    nprev = tm // PREV_ROWS
    tok = lambda i, j: (i, j, 0)
    return pl.pallas_call(
        _mlstm_proj_kernel,
        grid=(b, s // tm),
        in_specs=[
            pl.BlockSpec((1, tm, d), tok),
            pl.BlockSpec((1, PREV_ROWS, d), lambda i, j: (i, jnp.maximum(j * nprev - 1, 0), 0)),
            _const_spec(cv.shape),
            _const_spec(gb.shape),
        ] + [_const_spec(w.shape) for w in weights],
        out_specs=[
            pl.BlockSpec((1, tm, nqk), tok),
            pl.BlockSpec((1, tm, nqk), tok),
            pl.BlockSpec((1, tm, nv), tok),
            pl.BlockSpec((1, tm, nv), tok),
            pl.BlockSpec((1, tm, ng), tok),
            pl.BlockSpec((1, tm, ng), tok),
        ],
        out_shape=[
            jax.ShapeDtypeStruct((b, s, nqk), BF16),
            jax.ShapeDtypeStruct((b, s, nqk), BF16),
            jax.ShapeDtypeStruct((b, s, nv), BF16),
            jax.ShapeDtypeStruct((b, s, nv), BF16),
            jax.ShapeDtypeStruct((b, s, ng), F32),
            jax.ShapeDtypeStruct((b, s, ng), F32),
        ],
        compiler_params=_params(("parallel", "arbitrary")),
        name="mlstm_proj",
    )(h, h, cv, gb, *weights)


def _split3_bf16(z):
    hi = z.astype(BF16)
    r1 = z - hi.astype(F32)
    mid = r1.astype(BF16)
    lo = (r1 - mid.astype(F32)).astype(BF16)
    return jnp.concatenate([hi, mid, lo], axis=1)


def _mlstm_core_kernel(q_ref, k_ref, v_ref, og_ref, gi_ref, gf_ref, hn_ref, e64_ref, e128_ref,
                       o_ref, c_ref, m_ref):
    tt = q_ref.shape[1]
    L = CHUNK
    nch = tt // L
    npair = ML_HEADS // 2
    t = pl.program_id(1)

    @pl.when(t == 0)
    def _():
        c_ref[...] = jnp.zeros_like(c_ref)
        m_ref[...] = jnp.zeros_like(m_ref)

    lane = lax.broadcasted_iota(jnp.int32, (L, LANES), 1)
    rowl = lax.broadcasted_iota(jnp.int32, (L, LANES), 0)
    m0 = lane < ML_QK
    src = lane % L
    rowt = lax.broadcasted_iota(jnp.int32, (tt, tt), 0)
    colt = lax.broadcasted_iota(jnp.int32, (tt, tt), 1)
    tri = (((rowt // L) == (colt // L)) & (rowt >= colt)).astype(BF16)
    row_in_chunk = lax.broadcasted_iota(jnp.int32, (tt, LANES), 0) % L
    row_first = lax.broadcasted_iota(jnp.int32, (LANES, 1), 0) < ML_QK
    ones = jnp.ones((L, LANES), BF16)

    def split_heads(zv):
        zb = zv.astype(BF16)
        zero = jnp.zeros_like(zb)
        return jnp.concatenate([jnp.where(m0, zb, zero), jnp.where(m0, zero, zb)], axis=0)

    gi = gi_ref[0]
    b3 = _dot(tri, _split3_bf16(gf_ref[0]))
    bcum = b3[:, :LANES] + b3[:, LANES:2 * LANES] + b3[:, 2 * LANES:]
    z = gi - bcum
    cmax = z
    shift = 1
    while shift < L:
        cmax = jnp.where(row_in_chunk >= shift, jnp.maximum(cmax, pltpu.roll(cmax, shift, 0)), cmax)
        shift *= 2
    m_intra = bcum + cmax

    m_prev = m_ref[0:1, :]
    mp_rows, mn_rows, gt_rows, s_a = [], [], [], []
    for c in range(nch):
        last = c * L + L - 1
        g_tot = bcum[last:last + 1, :]
        m_loc = g_tot + cmax[last:last + 1, :]
        m_new = jnp.maximum(g_tot + m_prev, m_loc)
        mp_rows.append(jnp.broadcast_to(m_prev, (L, LANES)))
        mn_rows.append(jnp.broadcast_to(m_new, (L, LANES)))
        gt_rows.append(jnp.broadcast_to(g_tot, (L, LANES)))
        s_a.append(jnp.exp(g_tot + m_prev - m_new))
        m_prev = m_new
    m_ref[0:1, :] = m_prev
    inter = bcum + jnp.concatenate(mp_rows, axis=0)
    m_t = jnp.maximum(inter, m_intra)
    w_inter = jnp.exp(inter - m_t)
    kws = jnp.exp(jnp.concatenate(gt_rows, axis=0) + z - jnp.concatenate(mn_rows, axis=0))

    head_lane = lax.broadcasted_iota(jnp.int32, (tt, LANES), 1) < ML_HEADS

    def pack3(val):
        val = jnp.where(head_lane, val, 0.0)
        hi = val.astype(BF16).astype(F32)
        r1 = val - hi
        mid = r1.astype(BF16).astype(F32)
        lo = (r1 - mid).astype(BF16).astype(F32)
        packed = hi + pltpu.roll(mid, ML_HEADS, 1) + pltpu.roll(lo, 2 * ML_HEADS, 1)
        return packed.astype(BF16)

    spread = _dot(jnp.concatenate([pack3(bcum - m_t), pack3(z), pack3(w_inter), pack3(kws)], axis=0),
                  e64_ref[...])
    a_exp, z_exp, wi_exp, kw_exp = (spread[i * tt:(i + 1) * tt] for i in range(4))
    em_exp = _dot(pack3(jnp.exp(-m_t)), e128_ref[...])

    un = []
    for c in range(nch):
        rows = slice(c * L, (c + 1) * L)
        for p in range(npair):
            lanes = slice(p * LANES, (p + 1) * LANES)
            qp = q_ref[0, rows, lanes]
            kp = k_ref[0, rows, lanes]
            v0 = v_ref[0, rows, 2 * p * ML_V:(2 * p + 1) * ML_V]
            v1 = v_ref[0, rows, (2 * p + 1) * ML_V:(2 * p + 2) * ML_V]
            z_src = jnp.sum(jnp.where(rowl == src, z_exp[rows, lanes], 0.0), axis=0, keepdims=True)
            un.append(dict(
                c=c, p=p, rows=rows, qp=qp,
                k2=split_heads(kp),
                expd=jnp.where(rowl >= src, jnp.exp(a_exp[rows, lanes] + z_src), 0.0),
                qs=split_heads(qp * wi_exp[rows, lanes]),
                kw=split_heads(kp * kw_exp[rows, lanes]),
                vst=jnp.concatenate([jnp.concatenate([v0, ones], axis=1),
                                     jnp.concatenate([v1, ones], axis=1)], axis=0),
            ))

    for u in un:
        scores = _dot_nt(u["qp"], u["k2"])
        u["w_intra"] = split_heads(u["expd"] * scores)
    for u in un:
        u["c_loc"] = _dot_tn(u["kw"], u["vst"])

    states = [c_ref[p] for p in range(npair)]
    for u in un:
        c, p = u["c"], u["p"]
        u["c_old"] = states[p].astype(BF16)
        sa = jnp.where(row_first, s_a[c][:, 2 * p:2 * p + 1], s_a[c][:, 2 * p + 1:2 * p + 2])
        states[p] = sa * states[p] + u["c_loc"]
    for p in range(npair):
        c_ref[p] = states[p]

    for u in un:
        u["nd"] = _dot(jnp.concatenate([u["qs"], u["w_intra"]], axis=1),
                       jnp.concatenate([u["c_old"], u["vst"]], axis=0))

    for u in un:
        rows = u["rows"]
        for hh in range(2):
            head = 2 * u["p"] + hh
            cols = slice(head * ML_V, (head + 1) * ML_V)
            num = u["nd"][hh * L:(hh + 1) * L, :LANES]
            den = u["nd"][hh * L:(hh + 1) * L, LANES:]
            den = jnp.maximum(jnp.abs(den), em_exp[rows, cols])
            hv = num / den
            hv = hv * lax.rsqrt(jnp.mean(hv * hv, axis=-1, keepdims=True) + EPS)
            hv = hv * hn_ref[0:1, cols]
            o_ref[0, rows, cols] = (hv * og_ref[0, rows, cols]).astype(o_ref.dtype)


def _expand_matrix(width):
    rows = jnp.arange(LANES)[:, None]
    cols = jnp.arange(ML_HEADS * width)[None, :]
    return ((rows < 3 * ML_HEADS) & (rows % ML_HEADS == cols // width)).astype(BF16)


def _mlstm_core(q, k, v, og, gi, gf, hn_w, tt):
    b, s, dv = v.shape
    assert s % tt == 0 and tt % CHUNK == 0 and dv == ML_HEADS * ML_V
    nqk = q.shape[2]
    ng = gi.shape[2]
    e64 = _expand_matrix(ML_QK)
    e128 = _expand_matrix(ML_V)
    tok = lambda i, j: (i, j, 0)
    return pl.pallas_call(
        _mlstm_core_kernel,
        grid=(b, s // tt),
        in_specs=[
            pl.BlockSpec((1, tt, nqk), tok),
            pl.BlockSpec((1, tt, nqk), tok),
            pl.BlockSpec((1, tt, dv), tok),
            pl.BlockSpec((1, tt, dv), tok),
            pl.BlockSpec((1, tt, ng), tok),
            pl.BlockSpec((1, tt, ng), tok),
            _const_spec(hn_w.shape),
            _const_spec(e64.shape),
            _const_spec(e128.shape),
        ],
        out_specs=pl.BlockSpec((1, tt, dv), tok),
        out_shape=jax.ShapeDtypeStruct((b, s, dv), BF16),
        scratch_shapes=[
            pltpu.VMEM((ML_HEADS // 2, LANES, 2 * LANES), F32),
            pltpu.VMEM((8, LANES), F32),
        ],
        compiler_params=_params(("parallel", "arbitrary")),
        name="mlstm_core",
    )(q, k, v, og, gi, gf, hn_w, e64, e128)


def _pad_rows(a, rows=8):
    return jnp.pad(a, ((0, rows - a.shape[0]), (0, 0)))


def kernel(x, c, ada_w, ada_b, rw_mix, rw_w_in, rw_w0, rw_w2, rw_a0, rw_a2, rw_g2, rw_k_k, rw_k_a, rw_r_k, rw_gn_w, rw_gn_b, rw_w_out, ml_w_in, ml_conv_w, ml_conv_b, ml_b_i, ml_b_f, ml_hn_w, ml_w_out, ffn_w_gu, ffn_w_down, final_w):
    b, s, d = x.shape
    bf = lambda w: w.astype(BF16)
    final_w2 = final_w.reshape(1, d)

    mod = _adaln_mod(c, ada_w, ada_b)

    w_in = rw_w_in[0]
    o1 = d
    o2 = o1 + LORA_W
    o3 = o2 + d
    o4 = o3 + d
    o5 = o4 + LORA_A
    rw_weights = [bf(w_in[:, :o1]), bf(w_in[:, o2:o3]), bf(w_in[:, o3:o4]),
                  bf(w_in[:, o1:o2]), bf(w_in[:, o4:o5]), bf(w_in[:, o5:]),
                  bf(rw_w2[0]), bf(rw_a2[0]), bf(rw_g2[0])]
    mix = _pad_rows(rw_mix[0])
    vecs = _pad_rows(jnp.stack([rw_w0[0], rw_a0[0], rw_k_k[0], rw_k_a[0]]))
    proj = _rwkv_proj(x, mod[0], mix, vecs, rw_weights, TM_RWKV_PROJ)
    core_vecs = _pad_rows(jnp.stack([rw_r_k[0].reshape(d), rw_gn_w[0], rw_gn_b[0]]))
    y = _rwkv_core(proj, core_vecs, b, s, d, TT_RWKV, PAIRS_RWKV)
    w_gu = bf(ffn_w_gu)
    w_down = bf(ffn_w_down)
    x, h_next = _post_ffn(x, y, mod[0], final_w2, bf(rw_w_out[0]), w_gu, w_down, 0, TM_DENSE, False,
                          mod_next=mod[1])

    w_in = ml_w_in[0]
    nqk = 2 * ML_HEADS * ML_QK
    nv = ML_HEADS * ML_V
    pad_lanes = lambda w: jnp.pad(w, ((0, 0), (0, LANES - w.shape[1])))
    o_g = nqk + 2 * nv
    ml_weights = [bf(w_in[:, :nqk]), bf(w_in[:, nqk:nqk + nv]), bf(w_in[:, nqk + nv:o_g]),
                  bf(pad_lanes(w_in[:, o_g:o_g + ML_HEADS])), bf(pad_lanes(w_in[:, o_g + ML_HEADS:]))]
    cv = _pad_rows(jnp.concatenate([ml_conv_w[0], ml_conv_b[0][None, :]], axis=0))
    gb = _pad_rows(pad_lanes(jnp.stack([ml_b_i[0], ml_b_f[0]])))
    q, k, v, og, gi, gf = _mlstm_proj(h_next, cv, gb, ml_weights, TM_DENSE)
    y = _mlstm_core(q, k, v, og, gi, gf, ml_hn_w[0].reshape(1, nv), TT_MLSTM)
    x = _post_ffn(x, y, mod[1], final_w2, bf(ml_w_out[0]), w_gu, w_down, 1, TM_DENSE, True)
    return x
```
